```python
import math
import jax, jax.numpy as jnp
from jax import lax
import numpy as np

D_MODEL = 1024
BATCH = 2
SEQ = 8192
DEPTH = 2

ATT_HEADS = 16
ATT_KV_HEADS = 2
ATT_HEAD_DIM = 64
ATT_WIDTH = ATT_HEADS * ATT_HEAD_DIM
KV_WIDTH = ATT_KV_HEADS * ATT_HEAD_DIM
WINDOW = 128
ATT_BLOCK = 128
REL_BUCKETS = 32
REL_MAX_DIST = 128
SG_GROUPS = 8
SG_CHUNK = 128
SG_WIDTH = 1024
SG_GROUP_DIM = SG_WIDTH // SG_GROUPS
SSM_WIDTH = 2 * D_MODEL
SSM_HEAD_DIM = 64
SSM_HEADS = SSM_WIDTH // SSM_HEAD_DIM
SSM_GROUPS = 4
SSM_STATE = 128
SSM_CONV = 4
SSM_CHUNK = 128
SSM_CONV_DIM = SSM_WIDTH + 2 * SSM_GROUPS * SSM_STATE
N_BRANCHES = 3
IN_SIZES = (ATT_WIDTH, KV_WIDTH, KV_WIDTH, ATT_WIDTH,
            SG_WIDTH, SG_WIDTH, SG_WIDTH,
            SSM_WIDTH, SSM_CONV_DIM, SSM_HEADS,
            N_BRANCHES * D_MODEL)
IN_COLS = sum(IN_SIZES)
EPS = 1e-6

kernel_name = "hybrid_swa_sgu_ssd_gated_merge"


def _split_points():
    return [int(v) for v in np.cumsum(np.array(IN_SIZES))[:-1]]


def rms_norm(x, g):
    xf = x.astype(jnp.float32)
    y = xf * lax.rsqrt(jnp.mean(xf * xf, axis=-1, keepdims=True) + EPS)
    return (y * g.astype(jnp.float32)).astype(x.dtype)


def t5_causal_bucket(dist):
    max_exact = REL_BUCKETS // 2
    dist_f = jnp.maximum(dist, 1).astype(jnp.float32)
    large = max_exact + (jnp.log(dist_f / max_exact) / math.log(REL_MAX_DIST / max_exact)
                         * (REL_BUCKETS - max_exact)).astype(jnp.int32)
    large = jnp.minimum(large, REL_BUCKETS - 1)
    return jnp.where(dist < max_exact, dist, large)


def sliding_window_attention(q, k, v, sinks, rel_bias):
    bsz, seq = q.shape[:2]
    nb = seq // ATT_BLOCK
    grp = ATT_HEADS // ATT_KV_HEADS
    qb = q.reshape(bsz, nb, ATT_BLOCK, ATT_KV_HEADS, grp, ATT_HEAD_DIM) * (ATT_HEAD_DIM ** -0.5)

    def band(t):
        tb = t.reshape(bsz, nb, ATT_BLOCK, ATT_KV_HEADS, ATT_HEAD_DIM)
        prev = jnp.pad(tb, ((0, 0), (1, 0), (0, 0), (0, 0), (0, 0)))[:, :-1]
        return jnp.concatenate([prev, tb], axis=2)

    kk, vv = band(k), band(v)
    logits = jnp.einsum('bnqkgd,bnskd->bnkgqs', qb, kk).astype(jnp.float32)

    qi = jnp.arange(ATT_BLOCK, dtype=jnp.int32)[:, None]
    kj = jnp.arange(2 * ATT_BLOCK, dtype=jnp.int32)[None, :]
    dist = qi + ATT_BLOCK - kj
    in_window = (dist >= 0) & (dist < WINDOW)
    key_exists = (jnp.arange(nb)[:, None] > 0) | (kj >= ATT_BLOCK)
    mask = in_window[None] & key_exists[:, None, :]

    bias = rel_bias.astype(jnp.float32)[t5_causal_bucket(jnp.maximum(dist, 0))]
    bias = jnp.transpose(bias, (2, 0, 1)).reshape(ATT_KV_HEADS, grp, ATT_BLOCK, 2 * ATT_BLOCK)
    logits = jnp.where(mask[None, :, None, None], logits + bias[None, None], -jnp.inf)

    sink = sinks.astype(jnp.float32).reshape(ATT_KV_HEADS, grp)[None, None, :, :, None, None]
    m = jnp.maximum(jnp.max(logits, axis=-1, keepdims=True), sink)
    p = jnp.exp(logits - m)
    p = p / (jnp.sum(p, axis=-1, keepdims=True) + jnp.exp(sink - m))
    out = jnp.einsum('bnkgqs,bnskd->bnqkgd', p.astype(vv.dtype), vv)
    return out.reshape(bsz, seq, ATT_WIDTH)


def chunked_spatial_gate(u, v, ln_g, ln_b, w_s, b_s):
    bsz, seq = u.shape[:2]
    nc = seq // SG_CHUNK
    vf = v.astype(jnp.float32)
    mu = jnp.mean(vf, axis=-1, keepdims=True)
    var = jnp.mean(jnp.square(vf - mu), axis=-1, keepdims=True)
    vn = ((vf - mu) * lax.rsqrt(var + EPS) * ln_g.astype(jnp.float32) + ln_b.astype(jnp.float32)).astype(v.dtype)
    vc = vn.reshape(bsz, nc, SG_CHUNK, SG_GROUPS, SG_GROUP_DIM)
    causal = jnp.tril(jnp.ones((SG_CHUNK, SG_CHUNK), dtype=bool))
    w = jnp.where(causal[None], w_s, jnp.zeros_like(w_s))
    mixed = jnp.einsum('gts,bcsgd->bctgd', w, vc) + jnp.transpose(b_s)[None, None, :, :, None]
    return u * mixed.reshape(bsz, seq, SG_WIDTH)


def causal_depthwise_conv(x, w, b):
    ch = x.shape[-1]
    y = lax.conv_general_dilated(x, w[:, None, :].astype(x.dtype), window_strides=(1,),
                                 padding=[(SSM_CONV - 1, 0)],
                                 dimension_numbers=('NWC', 'WIO', 'NWC'),
                                 feature_group_count=ch)
    return y + b


def ssd_mixer(z, xbc, dt_raw, conv_w, conv_b, dt_bias, a_log, d_skip, norm_g):
    bsz, seq = z.shape[:2]
    nc = seq // SSM_CHUNK
    hpg = SSM_HEADS // SSM_GROUPS
    L = SSM_CHUNK
    xbc = jax.nn.silu(causal_depthwise_conv(xbc, conv_w, conv_b))
    gn = SSM_GROUPS * SSM_STATE
    xs = xbc[..., :SSM_WIDTH]
    b_in = xbc[..., SSM_WIDTH:SSM_WIDTH + gn]
    c_in = xbc[..., SSM_WIDTH + gn:]

    dt = jax.nn.softplus(dt_raw.astype(jnp.float32) + dt_bias.astype(jnp.float32))
    a = -jnp.exp(a_log.astype(jnp.float32))
    x_heads = xs.astype(jnp.float32).reshape(bsz, seq, SSM_HEADS, SSM_HEAD_DIM)
    xdt = (x_heads * dt[..., None]).reshape(bsz, nc, L, SSM_GROUPS, hpg, SSM_HEAD_DIM)
    bc = b_in.astype(jnp.float32).reshape(bsz, nc, L, SSM_GROUPS, SSM_STATE)
    cc = c_in.astype(jnp.float32).reshape(bsz, nc, L, SSM_GROUPS, SSM_STATE)

    a_dt = (dt * a).reshape(bsz, nc, L, SSM_GROUPS, hpg).transpose(0, 3, 4, 1, 2)
    a_cs = jnp.cumsum(a_dt, axis=-1)

    causal = jnp.tril(jnp.ones((L, L), dtype=bool))
    seg = a_cs[..., :, None] - a_cs[..., None, :]
    decay_in = jnp.exp(jnp.where(causal, seg, -jnp.inf))
    cb = jnp.einsum('bclgn,bcsgn->bcgls', cc, bc)
    y_diag = jnp.einsum('bcgls,bgjcls,bcsgjp->bclgjp', cb, decay_in, xdt)

    decay_to_end = jnp.exp(a_cs[..., -1:] - a_cs)
    states = jnp.einsum('bcsgn,bgjcs,bcsgjp->bcgjpn', bc, decay_to_end, xdt)
    chunk_decay = jnp.exp(a_cs[..., -1])

    def carry_state(h, inp):
        dec, st = inp
        return h * dec[..., None, None] + st, h

    init = jnp.zeros((bsz, SSM_GROUPS, hpg, SSM_HEAD_DIM, SSM_STATE), jnp.float32)
    _, prev = lax.scan(carry_state, init, (jnp.moveaxis(chunk_decay, -1, 0), jnp.moveaxis(states, 1, 0)))
    prev = jnp.moveaxis(prev, 0, 1)
    y_off = jnp.einsum('bclgn,bcgjpn,bgjcl->bclgjp', cc, prev, jnp.exp(a_cs))

    y = (y_diag + y_off).reshape(bsz, seq, SSM_HEADS, SSM_HEAD_DIM) + d_skip.astype(jnp.float32)[:, None] * x_heads
    y = y.reshape(bsz, seq, SSM_WIDTH) * jax.nn.silu(z.astype(jnp.float32))
    yg = y.reshape(bsz, seq, SSM_GROUPS, SSM_WIDTH // SSM_GROUPS)
    yg = yg * lax.rsqrt(jnp.mean(yg * yg, axis=-1, keepdims=True) + EPS)
    y = yg.reshape(bsz, seq, SSM_WIDTH) * norm_g.astype(jnp.float32)
    return y.astype(z.dtype)


def hybrid_layer(x, w_in, g_pre, g_post, rel_bias, sinks, sg_ln_g, sg_ln_b, sg_w, sg_b,
                 conv_w, conv_b, dt_bias, a_log, d_skip, ssm_norm_g,
                 w_br_att, w_br_sg, w_br_ssm, w_out):
    bsz, seq = x.shape[:2]
    h = rms_norm(x, g_pre)
    proj = jnp.einsum('bsd,dc->bsc', h, w_in)
    (q, k, v, z_a, u, v_s, z_s, z_m, xbc, dt_raw, gate_logits) = jnp.split(proj, _split_points(), axis=-1)

    q = q.reshape(bsz, seq, ATT_HEADS, ATT_HEAD_DIM)
    k = k.reshape(bsz, seq, ATT_KV_HEADS, ATT_HEAD_DIM)
    v = v.reshape(bsz, seq, ATT_KV_HEADS, ATT_HEAD_DIM)
    y_att = sliding_window_attention(q, k, v, sinks, rel_bias) * jax.nn.silu(z_a)
    y_sg = chunked_spatial_gate(u, v_s, sg_ln_g, sg_ln_b, sg_w, sg_b) * jax.nn.silu(z_s)
    y_ssm = ssd_mixer(z_m, xbc, dt_raw, conv_w, conv_b, dt_bias, a_log, d_skip, ssm_norm_g)

    gates = jax.nn.sigmoid(gate_logits.reshape(bsz, seq, N_BRANCHES, D_MODEL))
    merged = (gates[:, :, 0] * (y_att @ w_br_att)
              + gates[:, :, 1] * (y_sg @ w_br_sg)
              + gates[:, :, 2] * (y_ssm @ w_br_ssm))
    out = merged @ w_out
    return x + rms_norm(out, g_post)


def setup_inputs(seed: int = 0) -> dict:
    key = jax.random.key(seed)
    ks = jax.random.split(key, 24)
    f32 = jnp.float32
    nrm = lambda k, shape, s: (jax.random.normal(k, shape, f32) * s)
    dt0 = jnp.exp(jax.random.uniform(ks[12], (DEPTH, SSM_HEADS), f32, math.log(1e-3), math.log(1e-1)))
    return {
        "x": nrm(ks[0], (BATCH, SEQ, D_MODEL), 1.0),
        "w_in": nrm(ks[1], (DEPTH, D_MODEL, IN_COLS), D_MODEL ** -0.5),
        "norm_pre": 1.0 + nrm(ks[2], (DEPTH, D_MODEL), 0.02),
        "norm_post": 1.0 + nrm(ks[3], (DEPTH, D_MODEL), 0.02),
        "rel_bias": nrm(ks[4], (REL_BUCKETS, ATT_HEADS), 0.5),
        "att_sinks": nrm(ks[5], (DEPTH, ATT_HEADS), 0.5),
        "sg_ln_g": 1.0 + nrm(ks[6], (DEPTH, SG_WIDTH), 0.02),
        "sg_ln_b": nrm(ks[7], (DEPTH, SG_WIDTH), 0.02),
        "sg_w": nrm(ks[8], (DEPTH, SG_GROUPS, SG_CHUNK, SG_CHUNK), SG_CHUNK ** -0.5),
        "sg_b": 1.0 + nrm(ks[9], (DEPTH, SG_GROUPS, SG_CHUNK), 0.02),
        "ssm_conv_w": nrm(ks[10], (DEPTH, SSM_CONV, SSM_CONV_DIM), SSM_CONV ** -0.5),
        "ssm_conv_b": nrm(ks[11], (DEPTH, SSM_CONV_DIM), 0.02),
        "ssm_dt_bias": dt0 + jnp.log(-jnp.expm1(-dt0)),
        "ssm_a_log": jnp.log(jax.random.uniform(ks[13], (DEPTH, SSM_HEADS), f32, 1.0, 16.0)),
        "ssm_d": 1.0 + nrm(ks[14], (DEPTH, SSM_HEADS), 0.02),
        "ssm_norm_g": 1.0 + nrm(ks[15], (DEPTH, SSM_WIDTH), 0.02),
        "w_br_att": nrm(ks[16], (DEPTH, ATT_WIDTH, D_MODEL), ATT_WIDTH ** -0.5),
        "w_br_sg": nrm(ks[17], (DEPTH, SG_WIDTH, D_MODEL), SG_WIDTH ** -0.5),
        "w_br_ssm": nrm(ks[18], (DEPTH, SSM_WIDTH, D_MODEL), SSM_WIDTH ** -0.5),
        "w_out": nrm(ks[19], (DEPTH, D_MODEL, D_MODEL), D_MODEL ** -0.5),
    }


def reference(x, w_in, norm_pre, norm_post, rel_bias, att_sinks, sg_ln_g, sg_ln_b, sg_w, sg_b,
              ssm_conv_w, ssm_conv_b, ssm_dt_bias, ssm_a_log, ssm_d, ssm_norm_g,
              w_br_att, w_br_sg, w_br_ssm, w_out):
    for layer in range(DEPTH):
        x = hybrid_layer(x, w_in[layer], norm_pre[layer], norm_post[layer], rel_bias,
                         att_sinks[layer], sg_ln_g[layer], sg_ln_b[layer], sg_w[layer], sg_b[layer],
                         ssm_conv_w[layer], ssm_conv_b[layer], ssm_dt_bias[layer], ssm_a_log[layer],
                         ssm_d[layer], ssm_norm_g[layer],
                         w_br_att[layer], w_br_sg[layer], w_br_ssm[layer], w_out[layer])
    return x
```

```python
import functools
import math

import jax
import jax.numpy as jnp
from jax import lax
from jax.experimental import pallas as pl
from jax.experimental.pallas import tpu as pltpu

F32 = jnp.float32
BF16 = jnp.bfloat16

D_MODEL = 1024
ATT_HEADS = 16
ATT_KV_HEADS = 2
ATT_HEAD_DIM = 64
ATT_GROUP = ATT_HEADS // ATT_KV_HEADS
CHUNK = 128
REL_BUCKETS = 32
REL_MAX_DIST = 128
SG_GROUPS = 8
SG_WIDTH = 1024
SSM_WIDTH = 2048
SSM_HEAD_DIM = 64
SSM_HEADS = 32
SSM_GROUPS = 4
SSM_STATE = 128
SSM_HPG = SSM_HEADS // SSM_GROUPS
SSM_CONV = 4
SSM_BC = 2 * SSM_GROUPS * SSM_STATE
N_BRANCHES = 3
EPS = 1e-6
NEG = -1e30

LANE = 128
OFF_Q, OFF_ZA = 0, 1024
OFF_U, OFF_VS, OFF_ZS = 2048, 3072, 4096
OFF_G0 = 5120
OFF_ZM = 8192
OFF_XS = 10240
OFF_BC = 12288
OFF_K, OFF_V, OFF_DT = 13312, 13440, 13568
PROJ_COLS = 13824
PROJ_TN = 1152
PROJ_TM = 1024
MERGE_TM = 256
VMEM_LIMIT = 56 * 1024 * 1024


def _cparams(n_axes):
    return pltpu.CompilerParams(dimension_semantics=("arbitrary",) * n_axes,
                                vmem_limit_bytes=VMEM_LIMIT)


def _sigmoid(x):
    return 1.0 / (1.0 + jnp.exp(-x))


def _silu(x):
    return x * _sigmoid(x)


def _dot(a, b):
    return jnp.dot(a, b, preferred_element_type=F32)


def _dot_nt(a, b):
    return lax.dot_general(a, b, (((1,), (1,)), ((), ())), preferred_element_type=F32)


def _proj_kernel(x_ref, g_ref, w_ref, o_ref, h_ref):
    @pl.when(pl.program_id(1) == 0)
    def _():
        x = x_ref[...]
        ms = jnp.mean(x * x, axis=-1, keepdims=True)
        h_ref[...] = (x * lax.rsqrt(ms + EPS) * g_ref[...]).astype(BF16)

    o_ref[...] = _dot(h_ref[...], w_ref[...])


def _proj(x2, g_pre, w_bf):
    t = x2.shape[0]
    tm = min(PROJ_TM, t)
    return pl.pallas_call(
        _proj_kernel,
        grid=(t // tm, PROJ_COLS // PROJ_TN),
        in_specs=[pl.BlockSpec((tm, D_MODEL), lambda i, j: (i, 0)),
                  pl.BlockSpec((1, D_MODEL), lambda i, j: (0, 0)),
                  pl.BlockSpec((D_MODEL, PROJ_TN), lambda i, j: (0, j))],
        out_specs=pl.BlockSpec((tm, PROJ_TN), lambda i, j: (i, j)),
        out_shape=jax.ShapeDtypeStruct((t, PROJ_COLS), F32),
        scratch_shapes=[pltpu.VMEM((tm, D_MODEL), BF16)],
        compiler_params=_cparams(2),
        name="proj",
    )(x2, g_pre.reshape(1, D_MODEL), w_bf)


def _att_kernel(rb_ref, sink_ref, bkt_ref, q_ref, kc_ref, vc_ref, kp_ref, vp_ref, za_ref,
                o_ref, bias_ref):
    n = pl.program_id(1)

    @pl.when((pl.program_id(0) == 0) & (n == 0))
    def _():
        bkt = bkt_ref[...]
        qi = lax.broadcasted_iota(jnp.int32, (CHUNK, 2 * CHUNK), 0)
        kj = lax.broadcasted_iota(jnp.int32, (CHUNK, 2 * CHUNK), 1)
        dist = qi + CHUNK - kj
        in_window = (dist >= 0) & (dist < CHUNK)
        for h in range(ATT_HEADS):
            acc = jnp.zeros((CHUNK, 2 * CHUNK), F32)
            for b in range(REL_BUCKETS):
                acc = jnp.where(bkt == b, rb_ref[b, h], acc)
            bias_ref[h] = jnp.where(in_window, acc, NEG)

    has_prev = n > 0
    scale = ATT_HEAD_DIM ** -0.5
    q = q_ref[...] * scale
    outs = []
    for h in range(ATT_HEADS):
        kvh = h // ATT_GROUP
        sl = slice(kvh * ATT_HEAD_DIM, (kvh + 1) * ATT_HEAD_DIM)
        qh = q[:, h * ATT_HEAD_DIM:(h + 1) * ATT_HEAD_DIM].astype(BF16)
        kp = kp_ref[:, sl].astype(BF16)
        kc = kc_ref[:, sl].astype(BF16)
        vp = vp_ref[:, sl].astype(BF16)
        vc = vc_ref[:, sl].astype(BF16)
        sp = _dot_nt(qh, kp) + bias_ref[h, :, 0:CHUNK]
        sp = jnp.where(has_prev, sp, NEG)
        sc = _dot_nt(qh, kc) + bias_ref[h, :, CHUNK:2 * CHUNK]
        sink = sink_ref[h]
        m = jnp.maximum(jnp.maximum(jnp.max(sp, axis=-1, keepdims=True),
                                    jnp.max(sc, axis=-1, keepdims=True)), sink)
        pp = jnp.exp(sp - m)
        pc = jnp.exp(sc - m)
        den = (jnp.sum(pp, axis=-1, keepdims=True) + jnp.sum(pc, axis=-1, keepdims=True)
               + jnp.exp(sink - m))
        o = _dot(pp.astype(BF16), vp) + _dot(pc.astype(BF16), vc)
        outs.append(o / den)
    y = jnp.concatenate(outs, axis=-1) * _silu(za_ref[...])
    o_ref[...] = y.astype(o_ref.dtype)


def _attention(proj, rel_bias, sinks, bucket, bsz, nb):
    t = proj.shape[0]
    cb = lambda off, w: off // w
    row = lambda b, n: b * nb + n
    prev = lambda b, n: jnp.maximum(b * nb + n - 1, 0)
    smem = pl.BlockSpec(memory_space=pltpu.SMEM)
    return pl.pallas_call(
        _att_kernel,
        grid=(bsz, nb),
        in_specs=[smem, smem,
                  pl.BlockSpec((CHUNK, 2 * CHUNK), lambda b, n: (0, 0)),
                  pl.BlockSpec((CHUNK, 1024), lambda b, n: (row(b, n), cb(OFF_Q, 1024))),
                  pl.BlockSpec((CHUNK, LANE), lambda b, n: (row(b, n), cb(OFF_K, LANE))),
                  pl.BlockSpec((CHUNK, LANE), lambda b, n: (row(b, n), cb(OFF_V, LANE))),
                  pl.BlockSpec((CHUNK, LANE), lambda b, n: (prev(b, n), cb(OFF_K, LANE))),
                  pl.BlockSpec((CHUNK, LANE), lambda b, n: (prev(b, n), cb(OFF_V, LANE))),
                  pl.BlockSpec((CHUNK, 1024), lambda b, n: (row(b, n), cb(OFF_ZA, 1024)))],
        out_specs=pl.BlockSpec((CHUNK, 1024), lambda b, n: (row(b, n), 0)),
        out_shape=jax.ShapeDtypeStruct((t, 1024), BF16),
        scratch_shapes=[pltpu.VMEM((ATT_HEADS, CHUNK, 2 * CHUNK), F32)],
        compiler_params=_cparams(2),
        name="att",
    )(rel_bias, sinks, bucket, proj, proj, proj, proj, proj, proj)


def _sg_kernel(u_ref, vs_ref, zs_ref, lng_ref, lnb_ref, w_ref, bfull_ref, o_ref, wt_ref):
    @pl.when(pl.program_id(0) == 0)
    def _():
        ti = lax.broadcasted_iota(jnp.int32, (CHUNK, CHUNK), 0)
        si = lax.broadcasted_iota(jnp.int32, (CHUNK, CHUNK), 1)
        for g in range(SG_GROUPS):
            wt_ref[g] = jnp.where(si <= ti, w_ref[g], 0.0).astype(BF16)

    v = vs_ref[...]
    mu = jnp.mean(v, axis=-1, keepdims=True)
    vc = v - mu
    var = jnp.mean(vc * vc, axis=-1, keepdims=True)
    vn = (vc * lax.rsqrt(var + EPS) * lng_ref[...] + lnb_ref[...]).astype(BF16)
    mixed = [_dot(wt_ref[g], vn[:, g * LANE:(g + 1) * LANE]) for g in range(SG_GROUPS)]
    mixed = jnp.concatenate(mixed, axis=-1) + bfull_ref[...]
    y = u_ref[...] * mixed * _silu(zs_ref[...])
    o_ref[...] = y.astype(o_ref.dtype)


def _spatial_gate(proj, ln_g, ln_b, w_s, b_full):
    t = proj.shape[0]
    wide = lambda off: pl.BlockSpec((CHUNK, 1024), lambda i: (i, off // 1024))
    const2 = lambda shape: pl.BlockSpec(shape, lambda i: (0, 0))
    return pl.pallas_call(
        _sg_kernel,
        grid=(t // CHUNK,),
        in_specs=[wide(OFF_U), wide(OFF_VS), wide(OFF_ZS),
                  const2((1, SG_WIDTH)), const2((1, SG_WIDTH)),
                  pl.BlockSpec((SG_GROUPS, CHUNK, CHUNK), lambda i: (0, 0, 0)),
                  const2((CHUNK, SG_WIDTH))],
        out_specs=pl.BlockSpec((CHUNK, SG_WIDTH), lambda i: (i, 0)),
        out_shape=jax.ShapeDtypeStruct((t, SG_WIDTH), BF16),
        scratch_shapes=[pltpu.VMEM((SG_GROUPS, CHUNK, CHUNK), BF16)],
        compiler_params=_cparams(1),
        name="sg",
    )(proj, proj, proj, ln_g.reshape(1, -1), ln_b.reshape(1, -1), w_s, b_full)


def _split3(a):
    hi = a.astype(BF16)
    r1 = a - hi.astype(F32)
    mid = r1.astype(BF16)
    lo = (r1 - mid.astype(F32)).astype(BF16)
    return hi, mid, lo


def _ssd_kernel(zm_ref, xs_ref, bc_ref, dt_ref, xsp_ref, bcp_ref,
                cwx_ref, cbx_ref, cwb_ref, cbb_ref, dtb_ref, alog_ref, drow_ref, ng_ref,
                o_ref, ext_x, ext_b, h_ref, y_ref):
    c = pl.program_id(1)

    @pl.when(c == 0)
    def _():
        h_ref[...] = jnp.zeros_like(h_ref)
        ext_x[0:8, :] = jnp.zeros((8, SSM_WIDTH), F32)
        ext_b[0:8, :] = jnp.zeros((8, SSM_BC), F32)

    @pl.when(c > 0)
    def _():
        ext_x[0:8, :] = xsp_ref[...]
        ext_b[0:8, :] = bcp_ref[...]

    ext_x[8:8 + CHUNK, :] = xs_ref[...]
    ext_b[8:8 + CHUNK, :] = bc_ref[...]

    def conv(ext, w_ref, b_ref):
        acc = b_ref[...]
        for k in range(SSM_CONV):
            acc = acc + w_ref[k:k + 1, :] * ext[pl.ds(8 - (SSM_CONV - 1) + k, CHUNK), :]
        return _silu(acc)

    xs = conv(ext_x, cwx_ref, cbx_ref)
    bcv = conv(ext_b, cwb_ref, cbb_ref)

    x_dt = dt_ref[...] + dtb_ref[...]
    dt = jnp.maximum(x_dt, 0.0) + jnp.log1p(jnp.exp(-jnp.abs(x_dt)))
    a_dt = dt * (-jnp.exp(alog_ref[...]))
    li = lax.broadcasted_iota(jnp.int32, (CHUNK, CHUNK), 0)
    si = lax.broadcasted_iota(jnp.int32, (CHUNK, CHUNK), 1)
    causal = si <= li
    tri = jnp.where(causal, 1.0, 0.0).astype(BF16)
    a_cs = sum(_dot(tri, part) for part in _split3(a_dt))
    a_cs_t = a_cs.T
    dt_t = dt.T
    a_last = a_cs_t[:, CHUNK - 1:CHUNK]
    w_t = dt_t * jnp.exp(a_last - a_cs_t)
    cd_t = jnp.exp(a_last)

    for g in range(SSM_GROUPS):
        b_g = bcv[:, g * SSM_STATE:(g + 1) * SSM_STATE]
        c_g = bcv[:, (SSM_GROUPS + g) * SSM_STATE:(SSM_GROUPS + g + 1) * SSM_STATE]
        cb = _dot_nt(c_g.astype(BF16), b_g.astype(BF16))
        b_gt = b_g.T
        for jj in range(SSM_HPG):
            j = g * SSM_HPG + jj
            hs = slice(j * SSM_HEAD_DIM, (j + 1) * SSM_HEAD_DIM)
            a_col = jnp.broadcast_to(a_cs[:, j:j + 1], (CHUNK, CHUNK))
            a_row = a_cs_t[j:j + 1, :]
            decay = jnp.exp(jnp.where(causal, a_col - a_row, NEG))
            m_j = cb * decay * dt_t[j:j + 1, :]
            lhs = jnp.concatenate([m_j, c_g * jnp.exp(a_col)], axis=1).astype(BF16)
            x_j = xs[:, hs].astype(BF16)
            h_j = h_ref[:, hs]
            rhs = jnp.concatenate([x_j, h_j.astype(BF16)], axis=0)
            y_ref[:, hs] = _dot(lhs, rhs)
            bw = (b_gt * w_t[j:j + 1, :]).astype(BF16)
            h_ref[:, hs] = h_j * cd_t[j:j + 1, :] + _dot(bw, x_j)

    y = y_ref[...] + drow_ref[...] * xs
    y = y * _silu(zm_ref[...])
    gw = SSM_WIDTH // SSM_GROUPS
    parts = []
    for g in range(SSM_GROUPS):
        yg = y[:, g * gw:(g + 1) * gw]
        parts.append(yg * lax.rsqrt(jnp.mean(yg * yg, axis=-1, keepdims=True) + EPS))
    o_ref[...] = (jnp.concatenate(parts, axis=-1) * ng_ref[...]).astype(o_ref.dtype)


def _ssd(proj, cwx, cbx, cwb, cbb, dtb, alog, drow, ng, bsz, nc):
    t = proj.shape[0]
    row = lambda b, c: b * nc + c
    prev8 = lambda b, c: jnp.maximum((b * nc + c) * (CHUNK // 8) - 1, 0)
    const2 = lambda shape: pl.BlockSpec(shape, lambda b, c: (0, 0))
    return pl.pallas_call(
        _ssd_kernel,
        grid=(bsz, nc),
        in_specs=[pl.BlockSpec((CHUNK, SSM_WIDTH), lambda b, c: (row(b, c), OFF_ZM // SSM_WIDTH)),
                  pl.BlockSpec((CHUNK, SSM_WIDTH), lambda b, c: (row(b, c), OFF_XS // SSM_WIDTH)),
                  pl.BlockSpec((CHUNK, SSM_BC), lambda b, c: (row(b, c), OFF_BC // SSM_BC)),
                  pl.BlockSpec((CHUNK, LANE), lambda b, c: (row(b, c), OFF_DT // LANE)),
                  pl.BlockSpec((8, SSM_WIDTH), lambda b, c: (prev8(b, c), OFF_XS // SSM_WIDTH)),
                  pl.BlockSpec((8, SSM_BC), lambda b, c: (prev8(b, c), OFF_BC // SSM_BC)),
                  const2((SSM_CONV, SSM_WIDTH)), const2((1, SSM_WIDTH)),
                  const2((SSM_CONV, SSM_BC)), const2((1, SSM_BC)),
                  const2((1, LANE)), const2((1, LANE)),
                  const2((1, SSM_WIDTH)), const2((1, SSM_WIDTH))],
        out_specs=pl.BlockSpec((CHUNK, SSM_WIDTH), lambda b, c: (row(b, c), 0)),
        out_shape=jax.ShapeDtypeStruct((t, SSM_WIDTH), BF16),
        scratch_shapes=[pltpu.VMEM((8 + CHUNK, SSM_WIDTH), F32),
                        pltpu.VMEM((8 + CHUNK, SSM_BC), F32),
                        pltpu.VMEM((SSM_STATE, SSM_WIDTH), F32),
                        pltpu.VMEM((CHUNK, SSM_WIDTH), F32)],
        compiler_params=_cparams(2),
        name="ssd",
    )(proj, proj, proj, proj, proj, proj, cwx, cbx, cwb, cbb, dtb, alog, drow, ng)


def _merge_kernel(x_ref, g0_ref, g1_ref, g2_ref, ya_ref, ys_ref, ym_ref,
                  wa_ref, ws_ref, wm_ref, wo_ref, gp_ref, o_ref):
    merged = (_sigmoid(g0_ref[...]) * _dot(ya_ref[...], wa_ref[...])
              + _sigmoid(g1_ref[...]) * _dot(ys_ref[...], ws_ref[...])
              + _sigmoid(g2_ref[...]) * _dot(ym_ref[...], wm_ref[...]))
    out = _dot(merged.astype(BF16), wo_ref[...])
    ms = jnp.mean(out * out, axis=-1, keepdims=True)
    o_ref[...] = x_ref[...] + out * lax.rsqrt(ms + EPS) * gp_ref[...]


def _merge(x2, proj, y_att, y_sg, y_ssm, wa, ws, wm, wo, g_post):
    t = x2.shape[0]
    tm = min(MERGE_TM, t)
    rows = lambda w, cblk: pl.BlockSpec((tm, w), lambda i: (i, cblk))
    const2 = lambda shape: pl.BlockSpec(shape, lambda i: (0, 0))
    g_blk = OFF_G0 // 1024
    return pl.pallas_call(
        _merge_kernel,
        grid=(t // tm,),
        in_specs=[rows(D_MODEL, 0),
                  rows(1024, g_blk), rows(1024, g_blk + 1), rows(1024, g_blk + 2),
                  rows(1024, 0), rows(SG_WIDTH, 0), rows(SSM_WIDTH, 0),
                  const2((1024, D_MODEL)), const2((SG_WIDTH, D_MODEL)),
                  const2((SSM_WIDTH, D_MODEL)), const2((D_MODEL, D_MODEL)),
                  const2((1, D_MODEL))],
        out_specs=rows(D_MODEL, 0),
        out_shape=jax.ShapeDtypeStruct((t, D_MODEL), F32),
        compiler_params=_cparams(1),
        name="merge",
    )(x2, proj, proj, proj, y_att, y_sg, y_ssm, wa, ws, wm, wo, g_post.reshape(1, -1))


def _regroup_w_in(w):
    q, k, v, za = w[:, 0:1024], w[:, 1024:1152], w[:, 1152:1280], w[:, 1280:2304]
    u, vs, zs = w[:, 2304:3328], w[:, 3328:4352], w[:, 4352:5376]
    zm, xbc, dt, gates = w[:, 5376:7424], w[:, 7424:10496], w[:, 10496:10528], w[:, 10528:13600]
    pad = lambda n: jnp.zeros((w.shape[0], n), w.dtype)
    out = jnp.concatenate([q, za, u, vs, zs, gates, zm, xbc, k, v, dt, pad(LANE - SSM_HEADS), pad(LANE)],
                          axis=1)
    assert out.shape[1] == PROJ_COLS
    return out.astype(BF16)


def _rel_bucket_table():
    qi = jnp.arange(CHUNK, dtype=jnp.int32)[:, None]
    kj = jnp.arange(2 * CHUNK, dtype=jnp.int32)[None, :]
    dist = jnp.maximum(qi + CHUNK - kj, 0)
    max_exact = REL_BUCKETS // 2
    dist_f = jnp.maximum(dist, 1).astype(F32)
    large = max_exact + (jnp.log(dist_f / max_exact) / math.log(REL_MAX_DIST / max_exact)
                         * (REL_BUCKETS - max_exact)).astype(jnp.int32)
    large = jnp.minimum(large, REL_BUCKETS - 1)
    return jnp.where(dist < max_exact, dist, large)


def _pad_lanes(v):
    return jnp.pad(v, (0, LANE - v.shape[0])).reshape(1, LANE)


def kernel(x, w_in, norm_pre, norm_post, rel_bias, att_sinks, sg_ln_g, sg_ln_b, sg_w, sg_b, ssm_conv_w, ssm_conv_b, ssm_dt_bias, ssm_a_log, ssm_d, ssm_norm_g, w_br_att, w_br_sg, w_br_ssm, w_out):
    bsz, seq, d = x.shape
    depth = w_in.shape[0]
    assert d == D_MODEL and seq % CHUNK == 0
    nb = seq // CHUNK
    x2 = x.reshape(bsz * seq, d)
    bucket = _rel_bucket_table()
    for l in range(depth):
        proj = _proj(x2, norm_pre[l], _regroup_w_in(w_in[l]))
        y_att = _attention(proj, rel_bias, att_sinks[l], bucket, bsz, nb)
        b_full = jnp.repeat(jnp.transpose(sg_b[l]), CHUNK, axis=1)
        y_sg = _spatial_gate(proj, sg_ln_g[l], sg_ln_b[l], sg_w[l], b_full)
        cw, cb = ssm_conv_w[l], ssm_conv_b[l]
        y_ssm = _ssd(proj,
                     cw[:, :SSM_WIDTH], cb[:SSM_WIDTH].reshape(1, -1),
                     cw[:, SSM_WIDTH:], cb[SSM_WIDTH:].reshape(1, -1),
                     _pad_lanes(ssm_dt_bias[l]), _pad_lanes(ssm_a_log[l]),
                     jnp.repeat(ssm_d[l], SSM_HEAD_DIM).reshape(1, -1),
                     ssm_norm_g[l].reshape(1, -1), bsz, nb)
        x2 = _merge(x2, proj, y_att, y_sg, y_ssm,
                    w_br_att[l].astype(BF16), w_br_sg[l].astype(BF16),
                    w_br_ssm[l].astype(BF16), w_out[l].astype(BF16), norm_post[l])
    return x2.reshape(bsz, seq, d)
```

```python
import functools
import math

import jax
import jax.numpy as jnp
from jax import lax
from jax.experimental import pallas as pl
from jax.experimental.pallas import tpu as pltpu

F32 = jnp.float32
BF16 = jnp.bfloat16

D_MODEL = 1024
ATT_HEADS = 16
ATT_KV_HEADS = 2
ATT_HEAD_DIM = 64
ATT_GROUP = ATT_HEADS // ATT_KV_HEADS
CHUNK = 128
REL_BUCKETS = 32
REL_MAX_DIST = 128
SG_GROUPS = 8
SG_WIDTH = 1024
SSM_WIDTH = 2048
SSM_HEAD_DIM = 64
SSM_HEADS = 32
SSM_GROUPS = 4
SSM_STATE = 128
SSM_HPG = SSM_HEADS // SSM_GROUPS
SSM_CONV = 4
SSM_BC = 2 * SSM_GROUPS * SSM_STATE
N_BRANCHES = 3
EPS = 1e-6
NEG = -1e30
LOG2E = 1.4426950408889634

LANE = 128
OFF_Q, OFF_ZA = 0, 1024
OFF_U, OFF_VS, OFF_ZS = 2048, 3072, 4096
OFF_G0 = 5120
OFF_ZM = 8192
OFF_XS = 10240
OFF_BC = 12288
OFF_K, OFF_V, OFF_DT = 13312, 13440, 13568
PROJ_COLS = 13824
PROJ_TN = 1152
PROJ_TM = 1024
MERGE_TM = 256
VMEM_LIMIT = 56 * 1024 * 1024


def _cparams(n_axes):
    return pltpu.CompilerParams(dimension_semantics=("arbitrary",) * n_axes,
                                vmem_limit_bytes=VMEM_LIMIT)


def _sigmoid(x):
    return 1.0 / (1.0 + jnp.exp(-x))


def _silu(x):
    return x * _sigmoid(x)


def _dot(a, b):
    return jnp.dot(a, b, preferred_element_type=F32)


def _dot_nt(a, b):
    return lax.dot_general(a, b, (((1,), (1,)), ((), ())), preferred_element_type=F32)


def _proj_kernel(x_ref, g_ref, w_ref, o_ref, h_ref):
    @pl.when(pl.program_id(1) == 0)
    def _():
        x = x_ref[...]
        ms = jnp.mean(x * x, axis=-1, keepdims=True)
        h_ref[...] = (x * lax.rsqrt(ms + EPS) * g_ref[...]).astype(BF16)

    o_ref[...] = _dot(h_ref[...], w_ref[...])


def _proj(x2, g_pre, w_bf):
    t = x2.shape[0]
    tm = min(PROJ_TM, t)
    return pl.pallas_call(
        _proj_kernel,
        grid=(t // tm, PROJ_COLS // PROJ_TN),
        in_specs=[pl.BlockSpec((tm, D_MODEL), lambda i, j: (i, 0)),
                  pl.BlockSpec((1, D_MODEL), lambda i, j: (0, 0)),
                  pl.BlockSpec((D_MODEL, PROJ_TN), lambda i, j: (0, j))],
        out_specs=pl.BlockSpec((tm, PROJ_TN), lambda i, j: (i, j)),
        out_shape=jax.ShapeDtypeStruct((t, PROJ_COLS), F32),
        scratch_shapes=[pltpu.VMEM((tm, D_MODEL), BF16)],
        compiler_params=_cparams(2),
        name="proj",
    )(x2, g_pre.reshape(1, D_MODEL), w_bf)


def _att_kernel(rb_ref, sink_ref, bkt_ref, q_ref, kc_ref, vc_ref, kp_ref, vp_ref, za_ref,
                o_ref, bias_ref):
    n = pl.program_id(1)
    qi = lax.broadcasted_iota(jnp.int32, (CHUNK, CHUNK), 0)
    ci = lax.broadcasted_iota(jnp.int32, (CHUNK, CHUNK), 1)
    own = ci <= qi

    @pl.when((pl.program_id(0) == 0) & (n == 0))
    def _():
        bkt = jnp.where(own, bkt_ref[:, CHUNK:2 * CHUNK], bkt_ref[:, 0:CHUNK])
        for h in range(ATT_HEADS):
            acc = jnp.zeros((CHUNK, CHUNK), F32)
            for b in range(REL_BUCKETS):
                acc = jnp.where(bkt == b, rb_ref[b, h], acc)
            bias_ref[1, h] = acc
            bias_ref[0, h] = jnp.where(own, acc, NEG)

    variant = (n > 0).astype(jnp.int32)
    low = lax.broadcasted_iota(jnp.int32, (2 * CHUNK, LANE), 1) < ATT_HEAD_DIM
    low_q = lax.broadcasted_iota(jnp.int32, (CHUNK, LANE), 1) < ATT_HEAD_DIM
    scale = ATT_HEAD_DIM ** -0.5
    pairs = []
    for kvh in range(ATT_KV_HEADS):
        kcat = jnp.concatenate([kc_ref[...], kp_ref[...]], axis=0)
        vcat = jnp.concatenate([vc_ref[...], vp_ref[...]], axis=0)
        k_roll = pltpu.roll(kcat, ATT_HEAD_DIM, 1)
        v_roll = pltpu.roll(vcat, ATT_HEAD_DIM, 1)
        k_lo, k_hi = (kcat, k_roll) if kvh == 0 else (k_roll, kcat)
        v_lo, v_hi = (vcat, v_roll) if kvh == 0 else (v_roll, vcat)
        kz = (jnp.where(low, k_lo, 0.0).astype(BF16), jnp.where(low, 0.0, k_hi).astype(BF16))
        vz = (jnp.where(low, v_lo, 0.0).astype(BF16), jnp.where(low, 0.0, v_hi).astype(BF16))
        for pr in range(kvh * ATT_GROUP // 2, (kvh + 1) * ATT_GROUP // 2):
            q_pair = (q_ref[:, pr * LANE:(pr + 1) * LANE] * scale).astype(BF16)
            o_pair = None
            recips = []
            for par in range(2):
                h = 2 * pr + par
                s2 = _dot_nt(q_pair, kz[par])
                s = jnp.where(own, s2[:, 0:CHUNK], s2[:, CHUNK:2 * CHUNK]) + bias_ref[variant, h]
                sink = sink_ref[h]
                m = jnp.maximum(jnp.max(s, axis=-1, keepdims=True), sink)
                p = jnp.exp(s - m)
                den = jnp.sum(p, axis=-1, keepdims=True) + jnp.exp(sink - m)
                recips.append(1.0 / den)
                lhs = jnp.concatenate([jnp.where(own, p, 0.0), jnp.where(own, 0.0, p)],
                                      axis=1).astype(BF16)
                o = _dot(lhs, vz[par])
                o_pair = o if o_pair is None else o_pair + o
            r_pair = jnp.where(low_q, jnp.broadcast_to(recips[0], (CHUNK, LANE)),
                               jnp.broadcast_to(recips[1], (CHUNK, LANE)))
            pairs.append(o_pair * r_pair)
    y = jnp.concatenate(pairs, axis=-1) * _silu(za_ref[...])
    o_ref[...] = y.astype(o_ref.dtype)


def _attention(proj, rel_bias, sinks, bucket, bsz, nb):
    t = proj.shape[0]
    cb = lambda off, w: off // w
    row = lambda b, n: b * nb + n
    prev = lambda b, n: jnp.maximum(b * nb + n - 1, 0)
    smem = pl.BlockSpec(memory_space=pltpu.SMEM)
    return pl.pallas_call(
        _att_kernel,
        grid=(bsz, nb),
        in_specs=[smem, smem,
                  pl.BlockSpec((CHUNK, 2 * CHUNK), lambda b, n: (0, 0)),
                  pl.BlockSpec((CHUNK, 1024), lambda b, n: (row(b, n), cb(OFF_Q, 1024))),
                  pl.BlockSpec((CHUNK, LANE), lambda b, n: (row(b, n), cb(OFF_K, LANE))),
                  pl.BlockSpec((CHUNK, LANE), lambda b, n: (row(b, n), cb(OFF_V, LANE))),
                  pl.BlockSpec((CHUNK, LANE), lambda b, n: (prev(b, n), cb(OFF_K, LANE))),
                  pl.BlockSpec((CHUNK, LANE), lambda b, n: (prev(b, n), cb(OFF_V, LANE))),
                  pl.BlockSpec((CHUNK, 1024), lambda b, n: (row(b, n), cb(OFF_ZA, 1024)))],
        out_specs=pl.BlockSpec((CHUNK, 1024), lambda b, n: (row(b, n), 0)),
        out_shape=jax.ShapeDtypeStruct((t, 1024), BF16),
        scratch_shapes=[pltpu.VMEM((2, ATT_HEADS, CHUNK, CHUNK), F32)],
        compiler_params=_cparams(2),
        name="att",
    )(rel_bias, sinks, bucket, proj, proj, proj, proj, proj, proj)


def _sg_kernel(u_ref, vs_ref, zs_ref, lng_ref, lnb_ref, w_ref, bfull_ref, o_ref, wt_ref):
    @pl.when(pl.program_id(0) == 0)
    def _():
        ti = lax.broadcasted_iota(jnp.int32, (CHUNK, CHUNK), 0)
        si = lax.broadcasted_iota(jnp.int32, (CHUNK, CHUNK), 1)
        for g in range(SG_GROUPS):
            wt_ref[g] = jnp.where(si <= ti, w_ref[g], 0.0).astype(BF16)

    v = vs_ref[...]
    mu = jnp.mean(v, axis=-1, keepdims=True)
    vc = v - mu
    var = jnp.mean(vc * vc, axis=-1, keepdims=True)
    vn = (vc * lax.rsqrt(var + EPS) * lng_ref[...] + lnb_ref[...]).astype(BF16)
    mixed = [_dot(wt_ref[g], vn[:, g * LANE:(g + 1) * LANE]) for g in range(SG_GROUPS)]
    mixed = jnp.concatenate(mixed, axis=-1) + bfull_ref[...]
    y = u_ref[...] * mixed * _silu(zs_ref[...])
    o_ref[...] = y.astype(o_ref.dtype)


def _spatial_gate(proj, ln_g, ln_b, w_s, b_full):
    t = proj.shape[0]
    wide = lambda off: pl.BlockSpec((CHUNK, 1024), lambda i: (i, off // 1024))
    const2 = lambda shape: pl.BlockSpec(shape, lambda i: (0, 0))
    return pl.pallas_call(
        _sg_kernel,
        grid=(t // CHUNK,),
        in_specs=[wide(OFF_U), wide(OFF_VS), wide(OFF_ZS),
                  const2((1, SG_WIDTH)), const2((1, SG_WIDTH)),
                  pl.BlockSpec((SG_GROUPS, CHUNK, CHUNK), lambda i: (0, 0, 0)),
                  const2((CHUNK, SG_WIDTH))],
        out_specs=pl.BlockSpec((CHUNK, SG_WIDTH), lambda i: (i, 0)),
        out_shape=jax.ShapeDtypeStruct((t, SG_WIDTH), BF16),
        scratch_shapes=[pltpu.VMEM((SG_GROUPS, CHUNK, CHUNK), BF16)],
        compiler_params=_cparams(1),
        name="sg",
    )(proj, proj, proj, ln_g.reshape(1, -1), ln_b.reshape(1, -1), w_s, b_full)


def _split3(a):
    hi = a.astype(BF16)
    r1 = a - hi.astype(F32)
    mid = r1.astype(BF16)
    lo = (r1 - mid.astype(F32)).astype(BF16)
    return hi, mid, lo


def _softplus(x):
    return jnp.maximum(x, 0.0) + jnp.log1p(jnp.exp(-jnp.abs(x)))


def _ssd_kernel(zm_ref, xs_ref, bc_ref, dt_ref, xsp_ref, bcp_ref,
                cwx_ref, cbx_ref, cwb_ref, cbb_ref, dtb_ref, alog_ref, drow_ref, ng_ref,
                o_ref, ext_x, ext_b, h_ref, xc_ref, y_ref):
    c = pl.program_id(1)
    n_xt, n_bt = SSM_WIDTH // LANE, SSM_BC // LANE

    @pl.when(c == 0)
    def _():
        h_ref[...] = jnp.zeros_like(h_ref)

    def stage(ext, cur_ref, prev_ref, n_tiles):
        for t in range(n_tiles):
            sl = slice(t * LANE, (t + 1) * LANE)
            ext[t, 0:8, :] = jnp.where(c > 0, prev_ref[:, sl], 0.0)
            ext[t, 8:8 + CHUNK, :] = cur_ref[:, sl]

    def conv_tile(ext, w_ref, b_ref, t):
        sl = slice(t * LANE, (t + 1) * LANE)
        acc = b_ref[:, sl]
        for k in range(SSM_CONV):
            acc = acc + w_ref[k, :, sl] * ext[t, pl.ds(8 - (SSM_CONV - 1) + k, CHUNK), :]
        return _silu(acc)

    stage(ext_x, xs_ref, xsp_ref, n_xt)
    stage(ext_b, bc_ref, bcp_ref, n_bt)
    for t in range(n_xt):
        xc_ref[:, t * LANE:(t + 1) * LANE] = conv_tile(ext_x, cwx_ref, cbx_ref, t)
    bcv = [conv_tile(ext_b, cwb_ref, cbb_ref, t) for t in range(n_bt)]

    li = lax.broadcasted_iota(jnp.int32, (CHUNK, CHUNK), 0)
    si = lax.broadcasted_iota(jnp.int32, (CHUNK, CHUNK), 1)
    causal = si <= li
    x_dt_t = (dt_ref[...] + dtb_ref[...]).T[0:SSM_HEADS, :]
    dt_t = _softplus(x_dt_t)
    a_dt_t = dt_t * (-jnp.exp(alog_ref[0:SSM_HEADS, :]))
    upper = jnp.where(li <= si, 1.0, 0.0).astype(BF16)
    parts = _dot(jnp.concatenate(_split3(a_dt_t), axis=0), upper)
    a_cs_t = (parts[0:SSM_HEADS] + parts[SSM_HEADS:2 * SSM_HEADS]
              + parts[2 * SSM_HEADS:3 * SSM_HEADS])
    a2_t = a_cs_t * LOG2E
    a2_last = a2_t[:, CHUNK - 1:CHUNK]
    w_t = dt_t * jnp.exp2(a2_last - a2_t)
    cd_t = jnp.exp2(a2_last)
    a2 = jnp.concatenate([a2_t, jnp.zeros((CHUNK - SSM_HEADS, CHUNK), F32)], axis=0).T
    e2 = jnp.exp2(a2)
    low = lax.broadcasted_iota(jnp.int32, (CHUNK, LANE), 1) < SSM_HEAD_DIM
    low_row = low[0:1]

    for g in range(SSM_GROUPS):
        b_g, c_g = bcv[g], bcv[SSM_GROUPS + g]
        cb = jnp.where(causal, _dot_nt(c_g.astype(BF16), b_g.astype(BF16)), 0.0)
        b_gt = b_g.T
        for pr in range(g * SSM_HPG // 2, (g + 1) * SSM_HPG // 2):
            sl = slice(pr * LANE, (pr + 1) * LANE)
            x_pair = xc_ref[:, sl].astype(BF16)
            h_pair = h_ref[:, sl]
            rhs = jnp.concatenate([x_pair, h_pair.astype(BF16)], axis=0)
            ys, ups = [], []
            for j in (2 * pr, 2 * pr + 1):
                a_col = jnp.broadcast_to(a2[:, j:j + 1], (CHUNK, CHUNK))
                decay = jnp.exp2(jnp.minimum(a_col - a2_t[j:j + 1, :], 0.0))
                m_j = cb * decay * dt_t[j:j + 1, :]
                e_col = jnp.broadcast_to(e2[:, j:j + 1], (CHUNK, CHUNK))
                lhs = jnp.concatenate([m_j, c_g * e_col], axis=1).astype(BF16)
                ys.append(_dot(lhs, rhs))
                bw = (b_gt * w_t[j:j + 1, :]).astype(BF16)
                ups.append(_dot(bw, x_pair))
            y_ref[:, sl] = jnp.where(low, ys[0], ys[1])
            cd_pair = jnp.where(low_row, cd_t[2 * pr:2 * pr + 1, :], cd_t[2 * pr + 1:2 * pr + 2, :])
            h_ref[:, sl] = h_pair * cd_pair + jnp.where(low, ups[0], ups[1])

    gw = SSM_WIDTH // SSM_GROUPS
    for g in range(SSM_GROUPS):
        sl = slice(g * gw, (g + 1) * gw)
        yg = (y_ref[:, sl] + drow_ref[:, sl] * xc_ref[:, sl]) * _silu(zm_ref[:, sl])
        yg = yg * lax.rsqrt(jnp.mean(yg * yg, axis=-1, keepdims=True) + EPS)
        o_ref[:, sl] = (yg * ng_ref[:, sl]).astype(o_ref.dtype)


def _ssd(proj, cwx, cbx, cwb, cbb, dtb, alog, drow, ng, bsz, nc):
    t = proj.shape[0]
    row = lambda b, c: b * nc + c
    prev8 = lambda b, c: jnp.maximum((b * nc + c) * (CHUNK // 8) - 1, 0)
    const2 = lambda shape: pl.BlockSpec(shape, lambda b, c: (0, 0))
    const3 = lambda shape: pl.BlockSpec(shape, lambda b, c: (0, 0, 0))
    return pl.pallas_call(
        _ssd_kernel,
        grid=(bsz, nc),
        in_specs=[pl.BlockSpec((CHUNK, SSM_WIDTH), lambda b, c: (row(b, c), OFF_ZM // SSM_WIDTH)),
                  pl.BlockSpec((CHUNK, SSM_WIDTH), lambda b, c: (row(b, c), OFF_XS // SSM_WIDTH)),
                  pl.BlockSpec((CHUNK, SSM_BC), lambda b, c: (row(b, c), OFF_BC // SSM_BC)),
                  pl.BlockSpec((CHUNK, LANE), lambda b, c: (row(b, c), OFF_DT // LANE)),
                  pl.BlockSpec((8, SSM_WIDTH), lambda b, c: (prev8(b, c), OFF_XS // SSM_WIDTH)),
                  pl.BlockSpec((8, SSM_BC), lambda b, c: (prev8(b, c), OFF_BC // SSM_BC)),
                  const3((SSM_CONV, 1, SSM_WIDTH)), const2((1, SSM_WIDTH)),
                  const3((SSM_CONV, 1, SSM_BC)), const2((1, SSM_BC)),
                  const2((1, LANE)), const2((LANE, 1)),
                  const2((1, SSM_WIDTH)), const2((1, SSM_WIDTH))],
        out_specs=pl.BlockSpec((CHUNK, SSM_WIDTH), lambda b, c: (row(b, c), 0)),
        out_shape=jax.ShapeDtypeStruct((t, SSM_WIDTH), BF16),
        scratch_shapes=[pltpu.VMEM((SSM_WIDTH // LANE, 8 + CHUNK, LANE), F32),
                        pltpu.VMEM((SSM_BC // LANE, 8 + CHUNK, LANE), F32),
                        pltpu.VMEM((SSM_STATE, SSM_WIDTH), F32),
                        pltpu.VMEM((CHUNK, SSM_WIDTH), F32),
                        pltpu.VMEM((CHUNK, SSM_WIDTH), F32)],
        compiler_params=_cparams(2),
        name="ssd",
    )(proj, proj, proj, proj, proj, proj, cwx, cbx, cwb, cbb, dtb, alog, drow, ng)


def _merge_kernel(x_ref, g0_ref, g1_ref, g2_ref, ya_ref, ys_ref, ym_ref,
                  wa_ref, ws_ref, wm_ref, wo_ref, gp_ref, o_ref):
    merged = (_sigmoid(g0_ref[...]) * _dot(ya_ref[...], wa_ref[...])
              + _sigmoid(g1_ref[...]) * _dot(ys_ref[...], ws_ref[...])
              + _sigmoid(g2_ref[...]) * _dot(ym_ref[...], wm_ref[...]))
    out = _dot(merged.astype(BF16), wo_ref[...])
    ms = jnp.mean(out * out, axis=-1, keepdims=True)
    o_ref[...] = x_ref[...] + out * lax.rsqrt(ms + EPS) * gp_ref[...]


def _merge(x2, proj, y_att, y_sg, y_ssm, wa, ws, wm, wo, g_post):
    t = x2.shape[0]
    tm = min(MERGE_TM, t)
    rows = lambda w, cblk: pl.BlockSpec((tm, w), lambda i: (i, cblk))
    const2 = lambda shape: pl.BlockSpec(shape, lambda i: (0, 0))
    g_blk = OFF_G0 // 1024
    return pl.pallas_call(
        _merge_kernel,
        grid=(t // tm,),
        in_specs=[rows(D_MODEL, 0),
                  rows(1024, g_blk), rows(1024, g_blk + 1), rows(1024, g_blk + 2),
                  rows(1024, 0), rows(SG_WIDTH, 0), rows(SSM_WIDTH, 0),
                  const2((1024, D_MODEL)), const2((SG_WIDTH, D_MODEL)),
                  const2((SSM_WIDTH, D_MODEL)), const2((D_MODEL, D_MODEL)),
                  const2((1, D_MODEL))],
        out_specs=rows(D_MODEL, 0),
        out_shape=jax.ShapeDtypeStruct((t, D_MODEL), F32),
        compiler_params=_cparams(1),
        name="merge",
    )(x2, proj, proj, proj, y_att, y_sg, y_ssm, wa, ws, wm, wo, g_post.reshape(1, -1))


def _regroup_w_in(w):
    q, k, v, za = w[:, 0:1024], w[:, 1024:1152], w[:, 1152:1280], w[:, 1280:2304]
    u, vs, zs = w[:, 2304:3328], w[:, 3328:4352], w[:, 4352:5376]
    zm, xbc, dt, gates = w[:, 5376:7424], w[:, 7424:10496], w[:, 10496:10528], w[:, 10528:13600]
    pad = lambda n: jnp.zeros((w.shape[0], n), w.dtype)
    out = jnp.concatenate([q, za, u, vs, zs, gates, zm, xbc, k, v, dt, pad(LANE - SSM_HEADS), pad(LANE)],
                          axis=1)
    assert out.shape[1] == PROJ_COLS
    return out.astype(BF16)


def _rel_bucket_table():
    qi = jnp.arange(CHUNK, dtype=jnp.int32)[:, None]
    kj = jnp.arange(2 * CHUNK, dtype=jnp.int32)[None, :]
    dist = jnp.maximum(qi + CHUNK - kj, 0)
    max_exact = REL_BUCKETS // 2
    dist_f = jnp.maximum(dist, 1).astype(F32)
    large = max_exact + (jnp.log(dist_f / max_exact) / math.log(REL_MAX_DIST / max_exact)
                         * (REL_BUCKETS - max_exact)).astype(jnp.int32)
    large = jnp.minimum(large, REL_BUCKETS - 1)
    return jnp.where(dist < max_exact, dist, large)


def _pad_lanes(v):
    return jnp.pad(v, (0, LANE - v.shape[0])).reshape(1, LANE)


def kernel(x, w_in, norm_pre, norm_post, rel_bias, att_sinks, sg_ln_g, sg_ln_b, sg_w, sg_b, ssm_conv_w, ssm_conv_b, ssm_dt_bias, ssm_a_log, ssm_d, ssm_norm_g, w_br_att, w_br_sg, w_br_ssm, w_out):
    bsz, seq, d = x.shape
    depth = w_in.shape[0]
    assert d == D_MODEL and seq % CHUNK == 0
    nb = seq // CHUNK
    x2 = x.reshape(bsz * seq, d)
    bucket = _rel_bucket_table()
    for l in range(depth):
        proj = _proj(x2, norm_pre[l], _regroup_w_in(w_in[l]))
        y_att = _attention(proj, rel_bias, att_sinks[l], bucket, bsz, nb)
        b_full = jnp.repeat(jnp.transpose(sg_b[l]), CHUNK, axis=1)
        y_sg = _spatial_gate(proj, sg_ln_g[l], sg_ln_b[l], sg_w[l], b_full)
        cw, cb = ssm_conv_w[l], ssm_conv_b[l]
        y_ssm = _ssd(proj,
                     cw[:, None, :SSM_WIDTH], cb[:SSM_WIDTH].reshape(1, -1),
                     cw[:, None, SSM_WIDTH:], cb[SSM_WIDTH:].reshape(1, -1),
                     _pad_lanes(ssm_dt_bias[l]), _pad_lanes(ssm_a_log[l]).reshape(LANE, 1),
                     jnp.repeat(ssm_d[l], SSM_HEAD_DIM).reshape(1, -1),
                     ssm_norm_g[l].reshape(1, -1), bsz, nb)
        x2 = _merge(x2, proj, y_att, y_sg, y_ssm,
                    w_br_att[l].astype(BF16), w_br_sg[l].astype(BF16),
                    w_br_ssm[l].astype(BF16), w_out[l].astype(BF16), norm_post[l])
    return x2.reshape(bsz, seq, d)
```

```python
import functools
import math

import jax
import jax.numpy as jnp
from jax import lax
from jax.experimental import pallas as pl
from jax.experimental.pallas import tpu as pltpu

F32 = jnp.float32
BF16 = jnp.bfloat16

D_MODEL = 1024
ATT_HEADS = 16
ATT_KV_HEADS = 2
ATT_HEAD_DIM = 64
ATT_GROUP = ATT_HEADS // ATT_KV_HEADS
CHUNK = 128
REL_BUCKETS = 32
REL_MAX_DIST = 128
SG_GROUPS = 8
SG_WIDTH = 1024
SSM_WIDTH = 2048
SSM_HEAD_DIM = 64
SSM_HEADS = 32
SSM_GROUPS = 4
SSM_STATE = 128
SSM_HPG = SSM_HEADS // SSM_GROUPS
SSM_CONV = 4
SSM_BC = 2 * SSM_GROUPS * SSM_STATE
EPS = 1e-6
NEG = -1e30
LOG2E = 1.4426950408889634
LANE = 128

A_Q, A_ZA, A_G, A_K, A_V, A_COLS = 0, 1024, 2048, 3072, 3200, 3328
S_U, S_VS, S_ZS, S_G, S_COLS = 0, 1024, 2048, 3072, 4096
M_ZM, M_XS, M_BC, M_DT, M_G, M_COLS = 0, 2048, 4096, 5120, 5248, 6272
VMEM_LIMIT = 56 * 1024 * 1024


def _cparams():
    return pltpu.CompilerParams(dimension_semantics=("arbitrary",), vmem_limit_bytes=VMEM_LIMIT)


def _sigmoid(x):
    return 1.0 / (1.0 + jnp.exp(-x))


def _silu(x):
    return x * _sigmoid(x)


def _softplus(x):
    return jnp.maximum(x, 0.0) + jnp.log1p(jnp.exp(-jnp.abs(x)))


def _dot(a, b):
    return jnp.dot(a, b, preferred_element_type=F32)


def _dot_nt(a, b):
    return lax.dot_general(a, b, (((1,), (1,)), ((), ())), preferred_element_type=F32)


def _rms_proj(x_ref, g_ref, w_ref, dst_ref):
    x = x_ref[...]
    ms = jnp.mean(x * x, axis=-1, keepdims=True)
    h = (x * lax.rsqrt(ms + EPS) * g_ref[...]).astype(BF16)
    dst_ref[...] = _dot(h, w_ref[...])


def _split3(a):
    hi = a.astype(BF16)
    r1 = a - hi.astype(F32)
    mid = r1.astype(BF16)
    lo = (r1 - mid.astype(F32)).astype(BF16)
    return hi, mid, lo


def _att_block(src, kv_prev, bias_ref, sink_ref, variant):
    qi = lax.broadcasted_iota(jnp.int32, (CHUNK, CHUNK), 0)
    ci = lax.broadcasted_iota(jnp.int32, (CHUNK, CHUNK), 1)
    own = ci <= qi
    low = lax.broadcasted_iota(jnp.int32, (2 * CHUNK, LANE), 1) < ATT_HEAD_DIM
    low_q = lax.broadcasted_iota(jnp.int32, (CHUNK, LANE), 1) < ATT_HEAD_DIM
    scale = ATT_HEAD_DIM ** -0.5
    kcat = jnp.concatenate([src[:, A_K:A_K + LANE], kv_prev[:, 0:LANE]], axis=0)
    vcat = jnp.concatenate([src[:, A_V:A_V + LANE], kv_prev[:, LANE:2 * LANE]], axis=0)
    k_roll = pltpu.roll(kcat, ATT_HEAD_DIM, 1)
    v_roll = pltpu.roll(vcat, ATT_HEAD_DIM, 1)
    pairs = []
    for kvh in range(ATT_KV_HEADS):
        k_lo, k_hi = (kcat, k_roll) if kvh == 0 else (k_roll, kcat)
        v_lo, v_hi = (vcat, v_roll) if kvh == 0 else (v_roll, vcat)
        kz = (jnp.where(low, k_lo, 0.0).astype(BF16), jnp.where(low, 0.0, k_hi).astype(BF16))
        vz = (jnp.where(low, v_lo, 0.0).astype(BF16), jnp.where(low, 0.0, v_hi).astype(BF16))
        for pr in range(kvh * ATT_GROUP // 2, (kvh + 1) * ATT_GROUP // 2):
            q_pair = (src[:, A_Q + pr * LANE:A_Q + (pr + 1) * LANE] * scale).astype(BF16)
            o_pair = None
            recips = []
            for par in range(2):
                h = 2 * pr + par
                s2 = _dot_nt(q_pair, kz[par])
                s = jnp.where(own, s2[:, 0:CHUNK], s2[:, CHUNK:2 * CHUNK]) + bias_ref[variant, h]
                sink = sink_ref[h]
                m = jnp.maximum(jnp.max(s, axis=-1, keepdims=True), sink)
                p = jnp.exp(s - m)
                den = jnp.sum(p, axis=-1, keepdims=True) + jnp.exp(sink - m)
                recips.append(1.0 / den)
                lhs = jnp.concatenate([jnp.where(own, p, 0.0), jnp.where(own, 0.0, p)],
                                      axis=1).astype(BF16)
                o = _dot(lhs, vz[par])
                o_pair = o if o_pair is None else o_pair + o
            r_pair = jnp.where(low_q, jnp.broadcast_to(recips[0], (CHUNK, LANE)),
                               jnp.broadcast_to(recips[1], (CHUNK, LANE)))
            pairs.append(o_pair * r_pair)
    return jnp.concatenate(pairs, axis=-1) * _silu(src[:, A_ZA:A_ZA + 1024])


def _att_kernel(rb_ref, sink_ref, bkt_ref, x0_ref, xa_ref, xb_ref, gpre_ref, w_ref, wbr_ref,
                o_ref, slot_a, slot_b, kv_prev, bias_ref, *, nb):
    i = pl.program_id(0)

    @pl.when(i == 0)
    def _():
        qi = lax.broadcasted_iota(jnp.int32, (CHUNK, CHUNK), 0)
        ci = lax.broadcasted_iota(jnp.int32, (CHUNK, CHUNK), 1)
        own = ci <= qi
        bkt = jnp.where(own, bkt_ref[:, CHUNK:2 * CHUNK], bkt_ref[:, 0:CHUNK])
        for h in range(ATT_HEADS):
            acc = jnp.zeros((CHUNK, CHUNK), F32)
            for b in range(REL_BUCKETS):
                acc = jnp.where(bkt == b, rb_ref[b, h], acc)
            bias_ref[1, h] = acc
            bias_ref[0, h] = jnp.where(own, acc, NEG)
        kv_prev[...] = jnp.zeros_like(kv_prev)
        _rms_proj(x0_ref, gpre_ref, w_ref, slot_a)

    def mix(src, variant, rows):
        y = _att_block(src, kv_prev, bias_ref, sink_ref, variant)
        kv_prev[...] = src[:, A_K:A_K + 2 * LANE]
        o_ref[rows, :] = _sigmoid(src[:, A_G:A_G + D_MODEL]) * _dot(y.astype(BF16), wbr_ref[...])

    first = ((2 * i) % nb) == 0
    _rms_proj(xa_ref, gpre_ref, w_ref, slot_b)
    mix(slot_a, jnp.where(first, 0, 1), slice(0, CHUNK))
    _rms_proj(xb_ref, gpre_ref, w_ref, slot_a)
    mix(slot_b, 1, slice(CHUNK, 2 * CHUNK))


def _sg_block(src, lng_ref, lnb_ref, wt_ref, bfull_ref):
    v = src[:, S_VS:S_VS + SG_WIDTH]
    mu = jnp.mean(v, axis=-1, keepdims=True)
    vc = v - mu
    var = jnp.mean(vc * vc, axis=-1, keepdims=True)
    vn = (vc * lax.rsqrt(var + EPS) * lng_ref[...] + lnb_ref[...]).astype(BF16)
    mixed = [_dot(wt_ref[g], vn[:, g * LANE:(g + 1) * LANE]) for g in range(SG_GROUPS)]
    mixed = jnp.concatenate(mixed, axis=-1) + bfull_ref[...]
    return src[:, S_U:S_U + SG_WIDTH] * mixed * _silu(src[:, S_ZS:S_ZS + SG_WIDTH])


def _sg_kernel(x0_ref, xa_ref, xb_ref, acc_ref, gpre_ref, w_ref, wbr_ref,
               lng_ref, lnb_ref, ws_ref, bfull_ref, o_ref, slot_a, slot_b, wt_ref):
    i = pl.program_id(0)

    @pl.when(i == 0)
    def _():
        ti = lax.broadcasted_iota(jnp.int32, (CHUNK, CHUNK), 0)
        si = lax.broadcasted_iota(jnp.int32, (CHUNK, CHUNK), 1)
        for g in range(SG_GROUPS):
            wt_ref[g] = jnp.where(si <= ti, ws_ref[g], 0.0).astype(BF16)
        _rms_proj(x0_ref, gpre_ref, w_ref, slot_a)

    def mix(src, rows):
        y = _sg_block(src, lng_ref, lnb_ref, wt_ref, bfull_ref)
        o_ref[rows, :] = (acc_ref[rows, :]
                          + _sigmoid(src[:, S_G:S_G + D_MODEL]) * _dot(y.astype(BF16), wbr_ref[...]))

    _rms_proj(xa_ref, gpre_ref, w_ref, slot_b)
    mix(slot_a, slice(0, CHUNK))
    _rms_proj(xb_ref, gpre_ref, w_ref, slot_a)
    mix(slot_b, slice(CHUNK, 2 * CHUNK))


def _ssd_block(src, first, cwx_ref, cbx_ref, cwb_ref, cbb_ref, dtb_ref, alog_ref, drow_ref, ng_ref,
               ext_x, ext_b, h_ref, xc_ref, y_ref):
    n_xt, n_bt = SSM_WIDTH // LANE, SSM_BC // LANE
    tail = slice(CHUNK, CHUNK + 8)

    if first is not None:
        @pl.when(first)
        def _():
            h_ref[...] = jnp.zeros_like(h_ref)

    def stage(ext, col0, n_tiles):
        for t in range(n_tiles):
            prev = ext[t, tail, :]
            ext[t, 0:8, :] = prev if first is None else jnp.where(first, 0.0, prev)
            ext[t, 8:8 + CHUNK, :] = src[:, col0 + t * LANE:col0 + (t + 1) * LANE]

    def conv_tile(ext, w_ref, b_ref, t):
        sl = slice(t * LANE, (t + 1) * LANE)
        acc = b_ref[:, sl]
        for k in range(SSM_CONV):
            acc = acc + w_ref[k, :, sl] * ext[t, pl.ds(8 - (SSM_CONV - 1) + k, CHUNK), :]
        return _silu(acc)

    stage(ext_x, M_XS, n_xt)
    stage(ext_b, M_BC, n_bt)
    for t in range(n_xt):
        xc_ref[:, t * LANE:(t + 1) * LANE] = conv_tile(ext_x, cwx_ref, cbx_ref, t)
    bcv = [conv_tile(ext_b, cwb_ref, cbb_ref, t) for t in range(n_bt)]

    li = lax.broadcasted_iota(jnp.int32, (CHUNK, CHUNK), 0)
    si = lax.broadcasted_iota(jnp.int32, (CHUNK, CHUNK), 1)
    causal = si <= li
    x_dt_t = (src[:, M_DT:M_DT + LANE] + dtb_ref[...]).T[0:SSM_HEADS, :]
    dt_t = _softplus(x_dt_t)
    a_dt_t = dt_t * (-jnp.exp(alog_ref[0:SSM_HEADS, :]))
    upper = jnp.where(li <= si, 1.0, 0.0).astype(BF16)
    parts = _dot(jnp.concatenate(_split3(a_dt_t), axis=0), upper)
    a_cs_t = (parts[0:SSM_HEADS] + parts[SSM_HEADS:2 * SSM_HEADS]
              + parts[2 * SSM_HEADS:3 * SSM_HEADS])
    a2_t = a_cs_t * LOG2E
    a2_last = a2_t[:, CHUNK - 1:CHUNK]
    w_t = dt_t * jnp.exp2(a2_last - a2_t)
    cd_t = jnp.exp2(a2_last)
    a2 = jnp.concatenate([a2_t, jnp.zeros((CHUNK - SSM_HEADS, CHUNK), F32)], axis=0).T
    e2 = jnp.exp2(a2)
    low = lax.broadcasted_iota(jnp.int32, (CHUNK, LANE), 1) < SSM_HEAD_DIM
    low_row = low[0:1]

    for g in range(SSM_GROUPS):
        b_g, c_g = bcv[g], bcv[SSM_GROUPS + g]
        cb = jnp.where(causal, _dot_nt(c_g.astype(BF16), b_g.astype(BF16)), 0.0)
        b_gt = b_g.T
        for pr in range(g * SSM_HPG // 2, (g + 1) * SSM_HPG // 2):
            sl = slice(pr * LANE, (pr + 1) * LANE)
            x_pair = xc_ref[:, sl].astype(BF16)
            h_pair = h_ref[:, sl]
            rhs = jnp.concatenate([x_pair, h_pair.astype(BF16)], axis=0)
            ys, ups = [], []
            for j in (2 * pr, 2 * pr + 1):
                a_col = jnp.broadcast_to(a2[:, j:j + 1], (CHUNK, CHUNK))
                decay = jnp.exp2(jnp.minimum(a_col - a2_t[j:j + 1, :], 0.0))
                m_j = cb * decay * dt_t[j:j + 1, :]
                e_col = jnp.broadcast_to(e2[:, j:j + 1], (CHUNK, CHUNK))
                lhs = jnp.concatenate([m_j, c_g * e_col], axis=1).astype(BF16)
                ys.append(_dot(lhs, rhs))
                bw = (b_gt * w_t[j:j + 1, :]).astype(BF16)
                ups.append(_dot(bw, x_pair))
            y_ref[:, sl] = jnp.where(low, ys[0], ys[1])
            cd_pair = jnp.where(low_row, cd_t[2 * pr:2 * pr + 1, :], cd_t[2 * pr + 1:2 * pr + 2, :])
            h_ref[:, sl] = h_pair * cd_pair + jnp.where(low, ups[0], ups[1])

    gw = SSM_WIDTH // SSM_GROUPS
    out = []
    for g in range(SSM_GROUPS):
        sl = slice(g * gw, (g + 1) * gw)
        yg = (y_ref[:, sl] + drow_ref[:, sl] * xc_ref[:, sl]) * _silu(src[:, M_ZM + g * gw:M_ZM + (g + 1) * gw])
        yg = yg * lax.rsqrt(jnp.mean(yg * yg, axis=-1, keepdims=True) + EPS)
        out.append(yg * ng_ref[:, sl])
    return jnp.concatenate(out, axis=-1)


def _ssd_kernel(x0_ref, xa_ref, xb_ref, xres_ref, acc_ref, gpre_ref, w_ref, wbr_ref, wout_ref, gpost_ref,
                cwx_ref, cbx_ref, cwb_ref, cbb_ref, dtb_ref, alog_ref, drow_ref, ng_ref,
                o_ref, slot_a, slot_b, ext_x, ext_b, h_ref, xc_ref, y_ref, *, nb):
    i = pl.program_id(0)

    @pl.when(i == 0)
    def _():
        ext_x[...] = jnp.zeros_like(ext_x)
        ext_b[...] = jnp.zeros_like(ext_b)
        _rms_proj(x0_ref, gpre_ref, w_ref, slot_a)

    def mix(src, first, rows):
        y = _ssd_block(src, first, cwx_ref, cbx_ref, cwb_ref, cbb_ref, dtb_ref, alog_ref, drow_ref, ng_ref,
                       ext_x, ext_b, h_ref, xc_ref, y_ref)
        merged = (acc_ref[rows, :]
                  + _sigmoid(src[:, M_G:M_G + D_MODEL]) * _dot(y.astype(BF16), wbr_ref[...]))
        out = _dot(merged.astype(BF16), wout_ref[...])
        ms = jnp.mean(out * out, axis=-1, keepdims=True)
        o_ref[rows, :] = xres_ref[rows, :] + out * lax.rsqrt(ms + EPS) * gpost_ref[...]

    first = ((2 * i) % nb) == 0
    _rms_proj(xa_ref, gpre_ref, w_ref, slot_b)
    mix(slot_a, first, slice(0, CHUNK))
    _rms_proj(xb_ref, gpre_ref, w_ref, slot_a)
    mix(slot_b, None, slice(CHUNK, 2 * CHUNK))


def _x_specs(n_blk):
    return [pl.BlockSpec((CHUNK, D_MODEL), lambda i: (0, 0)),
            pl.BlockSpec((CHUNK, D_MODEL), lambda i: (2 * i + 1, 0)),
            pl.BlockSpec((CHUNK, D_MODEL), lambda i: (jnp.minimum(2 * i + 2, n_blk - 1), 0))]


def _const(shape):
    return pl.BlockSpec(shape, lambda i: (0,) * len(shape))


def _rows2():
    return pl.BlockSpec((2 * CHUNK, D_MODEL), lambda i: (i, 0))


def _att_branch(x2, g_pre, w_bf, wbr_bf, rel_bias, sinks, bucket, nb):
    t = x2.shape[0]
    n_blk = t // CHUNK
    smem = pl.BlockSpec(memory_space=pltpu.SMEM)
    return pl.pallas_call(
        functools.partial(_att_kernel, nb=nb),
        grid=(n_blk // 2,),
        in_specs=[smem, smem, _const((CHUNK, 2 * CHUNK))] + _x_specs(n_blk)
                 + [_const((1, D_MODEL)), _const((D_MODEL, A_COLS)), _const((1024, D_MODEL))],
        out_specs=_rows2(),
        out_shape=jax.ShapeDtypeStruct((t, D_MODEL), F32),
        scratch_shapes=[pltpu.VMEM((CHUNK, A_COLS), F32), pltpu.VMEM((CHUNK, A_COLS), F32),
                        pltpu.VMEM((CHUNK, 2 * LANE), F32),
                        pltpu.VMEM((2, ATT_HEADS, CHUNK, CHUNK), F32)],
        compiler_params=_cparams(),
        name="att",
    )(rel_bias, sinks, bucket, x2, x2, x2, g_pre.reshape(1, -1), w_bf, wbr_bf)


def _sg_branch(x2, acc, g_pre, w_bf, wbr_bf, ln_g, ln_b, w_s, b_full):
    t = x2.shape[0]
    n_blk = t // CHUNK
    return pl.pallas_call(
        _sg_kernel,
        grid=(n_blk // 2,),
        in_specs=_x_specs(n_blk) + [_rows2(), _const((1, D_MODEL)), _const((D_MODEL, S_COLS)),
                                    _const((SG_WIDTH, D_MODEL)), _const((1, SG_WIDTH)), _const((1, SG_WIDTH)),
                                    _const((SG_GROUPS, CHUNK, CHUNK)), _const((CHUNK, SG_WIDTH))],
        out_specs=_rows2(),
        out_shape=jax.ShapeDtypeStruct((t, D_MODEL), F32),
        scratch_shapes=[pltpu.VMEM((CHUNK, S_COLS), F32), pltpu.VMEM((CHUNK, S_COLS), F32),
                        pltpu.VMEM((SG_GROUPS, CHUNK, CHUNK), BF16)],
        compiler_params=_cparams(),
        name="sg",
    )(x2, x2, x2, acc, g_pre.reshape(1, -1), w_bf, wbr_bf, ln_g.reshape(1, -1), ln_b.reshape(1, -1),
      w_s, b_full)


def _ssd_branch(x2, acc, g_pre, w_bf, wbr_bf, wout_bf, g_post, cwx, cbx, cwb, cbb, dtb, alog, drow, ng, nb):
    t = x2.shape[0]
    n_blk = t // CHUNK
    return pl.pallas_call(
        functools.partial(_ssd_kernel, nb=nb),
        grid=(n_blk // 2,),
        in_specs=_x_specs(n_blk) + [_rows2(), _rows2(), _const((1, D_MODEL)), _const((D_MODEL, M_COLS)),
                                    _const((SSM_WIDTH, D_MODEL)), _const((D_MODEL, D_MODEL)),
                                    _const((1, D_MODEL)),
                                    _const((SSM_CONV, 1, SSM_WIDTH)), _const((1, SSM_WIDTH)),
                                    _const((SSM_CONV, 1, SSM_BC)), _const((1, SSM_BC)),
                                    _const((1, LANE)), _const((LANE, 1)),
                                    _const((1, SSM_WIDTH)), _const((1, SSM_WIDTH))],
        out_specs=_rows2(),
        out_shape=jax.ShapeDtypeStruct((t, D_MODEL), F32),
        scratch_shapes=[pltpu.VMEM((CHUNK, M_COLS), F32), pltpu.VMEM((CHUNK, M_COLS), F32),
                        pltpu.VMEM((SSM_WIDTH // LANE, 8 + CHUNK, LANE), F32),
                        pltpu.VMEM((SSM_BC // LANE, 8 + CHUNK, LANE), F32),
                        pltpu.VMEM((SSM_STATE, SSM_WIDTH), F32),
                        pltpu.VMEM((CHUNK, SSM_WIDTH), F32),
                        pltpu.VMEM((CHUNK, SSM_WIDTH), F32)],
        compiler_params=_cparams(),
        name="ssd",
    )(x2, x2, x2, x2, acc, g_pre.reshape(1, -1), w_bf, wbr_bf, wout_bf, g_post.reshape(1, -1),
      cwx, cbx, cwb, cbb, dtb, alog, drow, ng)


def _branch_weights(w):
    q, k, v, za = w[:, 0:1024], w[:, 1024:1152], w[:, 1152:1280], w[:, 1280:2304]
    u, vs, zs = w[:, 2304:3328], w[:, 3328:4352], w[:, 4352:5376]
    zm, xbc, dt = w[:, 5376:7424], w[:, 7424:10496], w[:, 10496:10528]
    g0, g1, g2 = w[:, 10528:11552], w[:, 11552:12576], w[:, 12576:13600]
    dt_pad = jnp.zeros((w.shape[0], LANE - SSM_HEADS), w.dtype)
    w_att = jnp.concatenate([q, za, g0, k, v], axis=1).astype(BF16)
    w_sg = jnp.concatenate([u, vs, zs, g1], axis=1).astype(BF16)
    w_ssm = jnp.concatenate([zm, xbc, dt, dt_pad, g2], axis=1).astype(BF16)
    assert w_att.shape[1] == A_COLS and w_sg.shape[1] == S_COLS and w_ssm.shape[1] == M_COLS
    return w_att, w_sg, w_ssm


def _rel_bucket_table():
    qi = jnp.arange(CHUNK, dtype=jnp.int32)[:, None]
    kj = jnp.arange(2 * CHUNK, dtype=jnp.int32)[None, :]
    dist = jnp.maximum(qi + CHUNK - kj, 0)
    max_exact = REL_BUCKETS // 2
    dist_f = jnp.maximum(dist, 1).astype(F32)
    large = max_exact + (jnp.log(dist_f / max_exact) / math.log(REL_MAX_DIST / max_exact)
                         * (REL_BUCKETS - max_exact)).astype(jnp.int32)
    large = jnp.minimum(large, REL_BUCKETS - 1)
    return jnp.where(dist < max_exact, dist, large)


def _pad_lanes(v):
    return jnp.pad(v, (0, LANE - v.shape[0])).reshape(1, LANE)


def kernel(x, w_in, norm_pre, norm_post, rel_bias, att_sinks, sg_ln_g, sg_ln_b, sg_w, sg_b, ssm_conv_w, ssm_conv_b, ssm_dt_bias, ssm_a_log, ssm_d, ssm_norm_g, w_br_att, w_br_sg, w_br_ssm, w_out):
    bsz, seq, d = x.shape
    depth = w_in.shape[0]
    assert d == D_MODEL and seq % (2 * CHUNK) == 0
    nb = seq // CHUNK
    x2 = x.reshape(bsz * seq, d)
    bucket = _rel_bucket_table()
    for l in range(depth):
        w_att, w_sg, w_ssm = _branch_weights(w_in[l])
        acc = _att_branch(x2, norm_pre[l], w_att, w_br_att[l].astype(BF16), rel_bias, att_sinks[l], bucket, nb)
        b_full = jnp.repeat(jnp.transpose(sg_b[l]), CHUNK, axis=1)
        acc = _sg_branch(x2, acc, norm_pre[l], w_sg, w_br_sg[l].astype(BF16),
                         sg_ln_g[l], sg_ln_b[l], sg_w[l], b_full)
        cw, cb = ssm_conv_w[l], ssm_conv_b[l]
        x2 = _ssd_branch(x2, acc, norm_pre[l], w_ssm, w_br_ssm[l].astype(BF16), w_out[l].astype(BF16),
                         norm_post[l],
                         cw[:, None, :SSM_WIDTH], cb[:SSM_WIDTH].reshape(1, -1),
                         cw[:, None, SSM_WIDTH:], cb[SSM_WIDTH:].reshape(1, -1),
                         _pad_lanes(ssm_dt_bias[l]), _pad_lanes(ssm_a_log[l]).reshape(LANE, 1),
                         jnp.repeat(ssm_d[l], SSM_HEAD_DIM).reshape(1, -1),
                         ssm_norm_g[l].reshape(1, -1), nb)
    return x2.reshape(bsz, seq, d)
```

```python
import functools
import math

import jax
import jax.numpy as jnp
from jax import lax
from jax.experimental import pallas as pl
from jax.experimental.pallas import tpu as pltpu

F32 = jnp.float32
BF16 = jnp.bfloat16

D_MODEL = 1024
ATT_HEADS = 16
ATT_KV_HEADS = 2
ATT_HEAD_DIM = 64
ATT_GROUP = ATT_HEADS // ATT_KV_HEADS
CHUNK = 128
REL_BUCKETS = 32
REL_MAX_DIST = 128
SG_GROUPS = 8
SG_WIDTH = 1024
SSM_WIDTH = 2048
SSM_HEAD_DIM = 64
SSM_HEADS = 32
SSM_GROUPS = 4
SSM_STATE = 128
SSM_HPG = SSM_HEADS // SSM_GROUPS
SSM_CONV = 4
SSM_BC = 2 * SSM_GROUPS * SSM_STATE
EPS = 1e-6
NEG = -1e30
LOG2E = 1.4426950408889634
LANE = 128

A_Q, A_ZA, A_G, A_K, A_V, A_COLS = 0, 1024, 2048, 3072, 3200, 3328
S_U, S_VS, S_ZS, S_G, S_COLS = 0, 1024, 2048, 3072, 4096
M_ZM, M_XS, M_BC, M_DT, M_G, M_COLS = 0, 2048, 4096, 5120, 5248, 6272
VMEM_LIMIT = 56 * 1024 * 1024


def _cparams():
    return pltpu.CompilerParams(dimension_semantics=("arbitrary",), vmem_limit_bytes=VMEM_LIMIT)


def _sigmoid(x):
    return 1.0 / (1.0 + jnp.exp(-x))


def _silu(x):
    return x * _sigmoid(x)


def _softplus(x):
    return jnp.maximum(x, 0.0) + jnp.log1p(jnp.exp(-jnp.abs(x)))


def _dot(a, b):
    return jnp.dot(a, b, preferred_element_type=F32)


def _dot_nt(a, b):
    return lax.dot_general(a, b, (((1,), (1,)), ((), ())), preferred_element_type=F32)


def _rms_norm_to(x_ref, g_ref, h_ref):
    x = x_ref[...]
    ms = jnp.mean(x * x, axis=-1, keepdims=True)
    h_ref[...] = (x * lax.rsqrt(ms + EPS) * g_ref[...]).astype(BF16)


def _proj_cols(h_ref, w_ref, dst_ref, c0, c1):
    if c1 > c0:
        dst_ref[:, c0:c1] = _dot(h_ref[...], w_ref[:, c0:c1])


def _rms_proj(x_ref, g_ref, w_ref, dst_ref, h_ref):
    _rms_norm_to(x_ref, g_ref, h_ref)
    _proj_cols(h_ref, w_ref, dst_ref, 0, dst_ref.shape[1])


class _Spread:
    def __init__(self, thunks, n_ticks):
        self.thunks, self.n_ticks, self.ticks, self.done = thunks, n_ticks, 0, 0

    def tick(self):
        self.ticks += 1
        while self.done < len(self.thunks) and self.done * self.n_ticks < self.ticks * len(self.thunks):
            self.thunks[self.done]()
            self.done += 1

    def flush(self):
        self.ticks = self.n_ticks
        self.tick()


def _col_chunks(n_cols, n_chunks, width=256):
    tiles = -(-n_cols // width)
    per = [tiles // n_chunks + (1 if k < tiles % n_chunks else 0) for k in range(n_chunks)]
    out, c = [], 0
    for n_tiles in per:
        out.append((c, min(c + n_tiles * width, n_cols)))
        c = min(c + n_tiles * width, n_cols)
    return out


def _split3(a):
    hi = a.astype(BF16)
    r1 = a - hi.astype(F32)
    mid = r1.astype(BF16)
    lo = (r1 - mid.astype(F32)).astype(BF16)
    return hi, mid, lo


def _att_block(src, kv_prev, bias_ref, sink_ref, variant, between=lambda pr: None):
    qi = lax.broadcasted_iota(jnp.int32, (CHUNK, CHUNK), 0)
    ci = lax.broadcasted_iota(jnp.int32, (CHUNK, CHUNK), 1)
    own = ci <= qi
    low = lax.broadcasted_iota(jnp.int32, (2 * CHUNK, LANE), 1) < ATT_HEAD_DIM
    low_q = lax.broadcasted_iota(jnp.int32, (CHUNK, LANE), 1) < ATT_HEAD_DIM
    scale = ATT_HEAD_DIM ** -0.5
    kcat = jnp.concatenate([src[:, A_K:A_K + LANE], kv_prev[:, 0:LANE]], axis=0)
    vcat = jnp.concatenate([src[:, A_V:A_V + LANE], kv_prev[:, LANE:2 * LANE]], axis=0)
    k_roll = pltpu.roll(kcat, ATT_HEAD_DIM, 1)
    v_roll = pltpu.roll(vcat, ATT_HEAD_DIM, 1)
    n_pair = ATT_GROUP // 2
    pairs = []
    hook = 0
    for kvh in range(ATT_KV_HEADS):
        k_lo, k_hi = (kcat, k_roll) if kvh == 0 else (k_roll, kcat)
        v_lo, v_hi = (vcat, v_roll) if kvh == 0 else (v_roll, vcat)
        kz = (jnp.where(low, k_lo, 0.0).astype(BF16), jnp.where(low, 0.0, k_hi).astype(BF16))
        vz = (jnp.where(low, v_lo, 0.0).astype(BF16), jnp.where(low, 0.0, v_hi).astype(BF16))
        pr0 = kvh * n_pair
        q4 = jnp.concatenate([src[:, A_Q + pr * LANE:A_Q + (pr + 1) * LANE]
                              for pr in range(pr0, pr0 + n_pair)], axis=0)
        q4 = (q4 * scale).astype(BF16)
        lhs_cols = []
        recips = [[None, None] for _ in range(n_pair)]
        for par in range(2):
            between(hook)
            hook += 1
            s2 = _dot_nt(q4, kz[par])
            rows = []
            for k in range(n_pair):
                h = 2 * (pr0 + k) + par
                s2k = s2[k * CHUNK:(k + 1) * CHUNK]
                s = jnp.where(own, s2k[:, 0:CHUNK], s2k[:, CHUNK:2 * CHUNK]) + bias_ref[variant, h]
                sink = sink_ref[h]
                m = jnp.maximum(jnp.max(s, axis=-1, keepdims=True), sink)
                p = jnp.exp(s - m)
                den = jnp.sum(p, axis=-1, keepdims=True) + jnp.exp(sink - m)
                recips[k][par] = 1.0 / den
                rows.append(jnp.concatenate([jnp.where(own, p, 0.0), jnp.where(own, 0.0, p)],
                                            axis=1).astype(BF16))
            lhs_cols.append(jnp.concatenate(rows, axis=0))
        between(hook)
        hook += 1
        o4 = _dot(jnp.concatenate(lhs_cols, axis=1), jnp.concatenate(vz, axis=0))
        for k in range(n_pair):
            r_pair = jnp.where(low_q, jnp.broadcast_to(recips[k][0], (CHUNK, LANE)),
                               jnp.broadcast_to(recips[k][1], (CHUNK, LANE)))
            pairs.append(o4[k * CHUNK:(k + 1) * CHUNK] * r_pair)
    y = jnp.concatenate(pairs, axis=-1) * _silu(src[:, A_ZA:A_ZA + 1024])
    between(hook)
    between(hook + 1)
    return y


def _att_kernel(rb_ref, sink_ref, bkt_ref, x0_ref, xa_ref, xb_ref, gpre_ref, w_ref, wbr_ref,
                o_ref, slot_a, slot_b, h_ref, kv_prev, bias_ref, *, nb):
    i = pl.program_id(0)
    chunks = _col_chunks(A_COLS, ATT_HEADS // 2)

    @pl.when(i == 0)
    def _():
        qi = lax.broadcasted_iota(jnp.int32, (CHUNK, CHUNK), 0)
        ci = lax.broadcasted_iota(jnp.int32, (CHUNK, CHUNK), 1)
        own = ci <= qi
        bkt = jnp.where(own, bkt_ref[:, CHUNK:2 * CHUNK], bkt_ref[:, 0:CHUNK])
        for h in range(ATT_HEADS):
            acc = jnp.zeros((CHUNK, CHUNK), F32)
            for b in range(REL_BUCKETS):
                acc = jnp.where(bkt == b, rb_ref[b, h], acc)
            bias_ref[1, h] = acc
            bias_ref[0, h] = jnp.where(own, acc, NEG)
        kv_prev[...] = jnp.zeros_like(kv_prev)
        _rms_proj(x0_ref, gpre_ref, w_ref, slot_a, h_ref)

    def mix(src, variant, rows, x_next_ref, dst):
        _rms_norm_to(x_next_ref, gpre_ref, h_ref)
        y = _att_block(src, kv_prev, bias_ref, sink_ref, variant,
                       between=lambda pr: _proj_cols(h_ref, w_ref, dst, *chunks[pr]))
        kv_prev[...] = src[:, A_K:A_K + 2 * LANE]
        o_ref[rows, :] = _sigmoid(src[:, A_G:A_G + D_MODEL]) * _dot(y.astype(BF16), wbr_ref[...])

    first = ((2 * i) % nb) == 0
    mix(slot_a, jnp.where(first, 0, 1), slice(0, CHUNK), xa_ref, slot_b)
    mix(slot_b, 1, slice(CHUNK, 2 * CHUNK), xb_ref, slot_a)


def _sg_block(src, lng_ref, lnb_ref, wt_ref, bfull_ref, tick=lambda: None):
    v = src[:, S_VS:S_VS + SG_WIDTH]
    mu = jnp.mean(v, axis=-1, keepdims=True)
    vc = v - mu
    var = jnp.mean(vc * vc, axis=-1, keepdims=True)
    vn = (vc * lax.rsqrt(var + EPS) * lng_ref[...] + lnb_ref[...]).astype(BF16)
    out = []
    for g in range(SG_GROUPS):
        tick()
        sl = slice(g * LANE, (g + 1) * LANE)
        mixed = _dot(wt_ref[g], vn[:, sl]) + bfull_ref[:, sl]
        out.append(src[:, S_U + g * LANE:S_U + (g + 1) * LANE] * mixed
                   * _silu(src[:, S_ZS + g * LANE:S_ZS + (g + 1) * LANE]))
    return jnp.concatenate(out, axis=-1)


def _sg_kernel(x0_ref, xa_ref, xb_ref, acc_ref, gpre_ref, w_ref, wbr_ref,
               lng_ref, lnb_ref, ws_ref, bfull_ref, o_ref, slot_a, slot_b, h_ref, wt_ref):
    i = pl.program_id(0)

    @pl.when(i == 0)
    def _():
        ti = lax.broadcasted_iota(jnp.int32, (CHUNK, CHUNK), 0)
        si = lax.broadcasted_iota(jnp.int32, (CHUNK, CHUNK), 1)
        for g in range(SG_GROUPS):
            wt_ref[g] = jnp.where(si <= ti, ws_ref[g], 0.0).astype(BF16)
        _rms_proj(x0_ref, gpre_ref, w_ref, slot_a, h_ref)

    chunks = _col_chunks(S_COLS, SG_GROUPS)

    def mix(src, rows, x_next_ref, dst):
        _rms_norm_to(x_next_ref, gpre_ref, h_ref)
        spread = _Spread([functools.partial(_proj_cols, h_ref, w_ref, dst, c0, c1) for c0, c1 in chunks],
                         SG_GROUPS)
        y = _sg_block(src, lng_ref, lnb_ref, wt_ref, bfull_ref, tick=spread.tick)
        spread.flush()
        o_ref[rows, :] = (acc_ref[rows, :]
                          + _sigmoid(src[:, S_G:S_G + D_MODEL]) * _dot(y.astype(BF16), wbr_ref[...]))

    mix(slot_a, slice(0, CHUNK), xa_ref, slot_b)
    mix(slot_b, slice(CHUNK, 2 * CHUNK), xb_ref, slot_a)


def _ssd_block(src, first, cwx_ref, cbx_ref, cwb_ref, cbb_ref, dtb_ref, alog_ref, drow_ref, ng_ref,
               ext_x, ext_b, h_ref, xc_ref, y_ref, tick=lambda: None):
    n_xt, n_bt = SSM_WIDTH // LANE, SSM_BC // LANE
    tail = slice(CHUNK, CHUNK + 8)

    if first is not None:
        @pl.when(first)
        def _():
            h_ref[...] = jnp.zeros_like(h_ref)

    def stage(ext, col0, n_tiles):
        for t in range(n_tiles):
            prev = ext[t, tail, :]
            ext[t, 0:8, :] = prev if first is None else jnp.where(first, 0.0, prev)
            ext[t, 8:8 + CHUNK, :] = src[:, col0 + t * LANE:col0 + (t + 1) * LANE]

    def conv_tile(ext, w_ref, b_ref, t):
        sl = slice(t * LANE, (t + 1) * LANE)
        acc = b_ref[:, sl]
        for k in range(SSM_CONV):
            acc = acc + w_ref[k, :, sl] * ext[t, pl.ds(8 - (SSM_CONV - 1) + k, CHUNK), :]
        return _silu(acc)

    stage(ext_x, M_XS, n_xt)
    stage(ext_b, M_BC, n_bt)
    for t in range(n_xt):
        tick()
        xc_ref[:, t * LANE:(t + 1) * LANE] = conv_tile(ext_x, cwx_ref, cbx_ref, t)
    bcv = []
    for t in range(n_bt):
        tick()
        bcv.append(conv_tile(ext_b, cwb_ref, cbb_ref, t))

    li = lax.broadcasted_iota(jnp.int32, (CHUNK, CHUNK), 0)
    si = lax.broadcasted_iota(jnp.int32, (CHUNK, CHUNK), 1)
    causal = si <= li
    x_dt_t = (src[:, M_DT:M_DT + LANE] + dtb_ref[...]).T[0:SSM_HEADS, :]
    dt_t = _softplus(x_dt_t)
    a_dt_t = dt_t * (-jnp.exp(alog_ref[0:SSM_HEADS, :]))
    upper = jnp.where(li <= si, 1.0, 0.0).astype(BF16)
    parts = _dot(jnp.concatenate(_split3(a_dt_t), axis=0), upper)
    a_cs_t = (parts[0:SSM_HEADS] + parts[SSM_HEADS:2 * SSM_HEADS]
              + parts[2 * SSM_HEADS:3 * SSM_HEADS])
    a2_t = a_cs_t * LOG2E
    a2_last = a2_t[:, CHUNK - 1:CHUNK]
    w_t = dt_t * jnp.exp2(a2_last - a2_t)
    cd_t = jnp.exp2(a2_last)
    a2 = jnp.concatenate([a2_t, jnp.zeros((CHUNK - SSM_HEADS, CHUNK), F32)], axis=0).T
    e2 = jnp.exp2(a2)
    low = lax.broadcasted_iota(jnp.int32, (CHUNK, LANE), 1) < SSM_HEAD_DIM
    low_row = low[0:1]

    for g in range(SSM_GROUPS):
        b_g, c_g = bcv[g], bcv[SSM_GROUPS + g]
        cb = jnp.where(causal, _dot_nt(c_g.astype(BF16), b_g.astype(BF16)), 0.0)
        b_gt = b_g.T
        for pr in range(g * SSM_HPG // 2, (g + 1) * SSM_HPG // 2):
            tick()
            sl = slice(pr * LANE, (pr + 1) * LANE)
            x_pair = xc_ref[:, sl].astype(BF16)
            h_pair = h_ref[:, sl]
            rhs = jnp.concatenate([x_pair, h_pair.astype(BF16)], axis=0)
            lhs_rows, bw_rows = [], []
            for j in (2 * pr, 2 * pr + 1):
                a_col = jnp.broadcast_to(a2[:, j:j + 1], (CHUNK, CHUNK))
                decay = jnp.exp2(jnp.minimum(a_col - a2_t[j:j + 1, :], 0.0))
                m_j = cb * decay * dt_t[j:j + 1, :]
                e_col = jnp.broadcast_to(e2[:, j:j + 1], (CHUNK, CHUNK))
                lhs_rows.append(jnp.concatenate([m_j, c_g * e_col], axis=1).astype(BF16))
                bw_rows.append((b_gt * w_t[j:j + 1, :]).astype(BF16))
            yy = _dot(jnp.concatenate(lhs_rows, axis=0), rhs)
            up = _dot(jnp.concatenate(bw_rows, axis=0), x_pair)
            y_ref[:, sl] = jnp.where(low, yy[0:CHUNK], yy[CHUNK:2 * CHUNK])
            cd_pair = jnp.where(low_row, cd_t[2 * pr:2 * pr + 1, :], cd_t[2 * pr + 1:2 * pr + 2, :])
            h_ref[:, sl] = h_pair * cd_pair + jnp.where(low, up[0:SSM_STATE], up[SSM_STATE:2 * SSM_STATE])

    gw = SSM_WIDTH // SSM_GROUPS
    out = []
    for g in range(SSM_GROUPS):
        tick()
        sl = slice(g * gw, (g + 1) * gw)
        yg = (y_ref[:, sl] + drow_ref[:, sl] * xc_ref[:, sl]) * _silu(src[:, M_ZM + g * gw:M_ZM + (g + 1) * gw])
        yg = yg * lax.rsqrt(jnp.mean(yg * yg, axis=-1, keepdims=True) + EPS)
        out.append(yg * ng_ref[:, sl])
    return jnp.concatenate(out, axis=-1)


def _ssd_kernel(x0_ref, xa_ref, xb_ref, xres_ref, acc_ref, gpre_ref, w_ref, wbr_ref, wout_ref, gpost_ref,
                cwx_ref, cbx_ref, cwb_ref, cbb_ref, dtb_ref, alog_ref, drow_ref, ng_ref,
                o_ref, slot_a, slot_b, hn_ref, ext_x, ext_b, h_ref, xc_ref, y_ref, *, nb):
    i = pl.program_id(0)

    @pl.when(i == 0)
    def _():
        ext_x[...] = jnp.zeros_like(ext_x)
        ext_b[...] = jnp.zeros_like(ext_b)
        _rms_proj(x0_ref, gpre_ref, w_ref, slot_a, hn_ref)

    chunks = _col_chunks(M_COLS, -(-M_COLS // 256))
    n_ticks = SSM_WIDTH // LANE + SSM_BC // LANE + SSM_HEADS // 2 + SSM_GROUPS

    def mix(src, first, rows, x_next_ref, dst):
        _rms_norm_to(x_next_ref, gpre_ref, hn_ref)
        spread = _Spread([functools.partial(_proj_cols, hn_ref, w_ref, dst, c0, c1) for c0, c1 in chunks],
                         n_ticks)
        y = _ssd_block(src, first, cwx_ref, cbx_ref, cwb_ref, cbb_ref, dtb_ref, alog_ref, drow_ref, ng_ref,
                       ext_x, ext_b, h_ref, xc_ref, y_ref, tick=spread.tick)
        spread.flush()
        merged = (acc_ref[rows, :]
                  + _sigmoid(src[:, M_G:M_G + D_MODEL]) * _dot(y.astype(BF16), wbr_ref[...]))
        out = _dot(merged.astype(BF16), wout_ref[...])
        ms = jnp.mean(out * out, axis=-1, keepdims=True)
        o_ref[rows, :] = xres_ref[rows, :] + out * lax.rsqrt(ms + EPS) * gpost_ref[...]

    first = ((2 * i) % nb) == 0
    mix(slot_a, first, slice(0, CHUNK), xa_ref, slot_b)
    mix(slot_b, None, slice(CHUNK, 2 * CHUNK), xb_ref, slot_a)


def _x_specs(n_blk):
    return [pl.BlockSpec((CHUNK, D_MODEL), lambda i: (0, 0)),
            pl.BlockSpec((CHUNK, D_MODEL), lambda i: (2 * i + 1, 0)),
            pl.BlockSpec((CHUNK, D_MODEL), lambda i: (jnp.minimum(2 * i + 2, n_blk - 1), 0))]


def _const(shape):
    return pl.BlockSpec(shape, lambda i: (0,) * len(shape))


def _rows2():
    return pl.BlockSpec((2 * CHUNK, D_MODEL), lambda i: (i, 0))


def _att_branch(x2, g_pre, w_bf, wbr_bf, rel_bias, sinks, bucket, nb):
    t = x2.shape[0]
    n_blk = t // CHUNK
    smem = pl.BlockSpec(memory_space=pltpu.SMEM)
    return pl.pallas_call(
        functools.partial(_att_kernel, nb=nb),
        grid=(n_blk // 2,),
        in_specs=[smem, smem, _const((CHUNK, 2 * CHUNK))] + _x_specs(n_blk)
                 + [_const((1, D_MODEL)), _const((D_MODEL, A_COLS)), _const((1024, D_MODEL))],
        out_specs=_rows2(),
        out_shape=jax.ShapeDtypeStruct((t, D_MODEL), F32),
        scratch_shapes=[pltpu.VMEM((CHUNK, A_COLS), F32), pltpu.VMEM((CHUNK, A_COLS), F32),
                        pltpu.VMEM((CHUNK, D_MODEL), BF16),
                        pltpu.VMEM((CHUNK, 2 * LANE), F32),
                        pltpu.VMEM((2, ATT_HEADS, CHUNK, CHUNK), F32)],
        compiler_params=_cparams(),
        name="att",
    )(rel_bias, sinks, bucket, x2, x2, x2, g_pre.reshape(1, -1), w_bf, wbr_bf)


def _sg_branch(x2, acc, g_pre, w_bf, wbr_bf, ln_g, ln_b, w_s, b_full):
    t = x2.shape[0]
    n_blk = t // CHUNK
    return pl.pallas_call(
        _sg_kernel,
        grid=(n_blk // 2,),
        in_specs=_x_specs(n_blk) + [_rows2(), _const((1, D_MODEL)), _const((D_MODEL, S_COLS)),
                                    _const((SG_WIDTH, D_MODEL)), _const((1, SG_WIDTH)), _const((1, SG_WIDTH)),
                                    _const((SG_GROUPS, CHUNK, CHUNK)), _const((CHUNK, SG_WIDTH))],
        out_specs=_rows2(),
        out_shape=jax.ShapeDtypeStruct((t, D_MODEL), F32),
        scratch_shapes=[pltpu.VMEM((CHUNK, S_COLS), F32), pltpu.VMEM((CHUNK, S_COLS), F32),
                        pltpu.VMEM((CHUNK, D_MODEL), BF16),
                        pltpu.VMEM((SG_GROUPS, CHUNK, CHUNK), BF16)],
        compiler_params=_cparams(),
        name="sg",
    )(x2, x2, x2, acc, g_pre.reshape(1, -1), w_bf, wbr_bf, ln_g.reshape(1, -1), ln_b.reshape(1, -1),
      w_s, b_full)


def _ssd_branch(x2, acc, g_pre, w_bf, wbr_bf, wout_bf, g_post, cwx, cbx, cwb, cbb, dtb, alog, drow, ng, nb):
    t = x2.shape[0]
    n_blk = t // CHUNK
    return pl.pallas_call(
        functools.partial(_ssd_kernel, nb=nb),
        grid=(n_blk // 2,),
        in_specs=_x_specs(n_blk) + [_rows2(), _rows2(), _const((1, D_MODEL)), _const((D_MODEL, M_COLS)),
                                    _const((SSM_WIDTH, D_MODEL)), _const((D_MODEL, D_MODEL)),
                                    _const((1, D_MODEL)),
                                    _const((SSM_CONV, 1, SSM_WIDTH)), _const((1, SSM_WIDTH)),
                                    _const((SSM_CONV, 1, SSM_BC)), _const((1, SSM_BC)),
                                    _const((1, LANE)), _const((LANE, 1)),
                                    _const((1, SSM_WIDTH)), _const((1, SSM_WIDTH))],
        out_specs=_rows2(),
        out_shape=jax.ShapeDtypeStruct((t, D_MODEL), F32),
        scratch_shapes=[pltpu.VMEM((CHUNK, M_COLS), F32), pltpu.VMEM((CHUNK, M_COLS), F32),
                        pltpu.VMEM((CHUNK, D_MODEL), BF16),
                        pltpu.VMEM((SSM_WIDTH // LANE, 8 + CHUNK, LANE), F32),
                        pltpu.VMEM((SSM_BC // LANE, 8 + CHUNK, LANE), F32),
                        pltpu.VMEM((SSM_STATE, SSM_WIDTH), F32),
                        pltpu.VMEM((CHUNK, SSM_WIDTH), F32),
                        pltpu.VMEM((CHUNK, SSM_WIDTH), F32)],
        compiler_params=_cparams(),
        name="ssd",
    )(x2, x2, x2, x2, acc, g_pre.reshape(1, -1), w_bf, wbr_bf, wout_bf, g_post.reshape(1, -1),
      cwx, cbx, cwb, cbb, dtb, alog, drow, ng)


def _branch_weights(w):
    q, k, v, za = w[:, 0:1024], w[:, 1024:1152], w[:, 1152:1280], w[:, 1280:2304]
    u, vs, zs = w[:, 2304:3328], w[:, 3328:4352], w[:, 4352:5376]
    zm, xbc, dt = w[:, 5376:7424], w[:, 7424:10496], w[:, 10496:10528]
    g0, g1, g2 = w[:, 10528:11552], w[:, 11552:12576], w[:, 12576:13600]
    dt_pad = jnp.zeros((w.shape[0], LANE - SSM_HEADS), w.dtype)
    w_att = jnp.concatenate([q, za, g0, k, v], axis=1).astype(BF16)
    w_sg = jnp.concatenate([u, vs, zs, g1], axis=1).astype(BF16)
    w_ssm = jnp.concatenate([zm, xbc, dt, dt_pad, g2], axis=1).astype(BF16)
    assert w_att.shape[1] == A_COLS and w_sg.shape[1] == S_COLS and w_ssm.shape[1] == M_COLS
    return w_att, w_sg, w_ssm


def _rel_bucket_table():
    qi = jnp.arange(CHUNK, dtype=jnp.int32)[:, None]
    kj = jnp.arange(2 * CHUNK, dtype=jnp.int32)[None, :]
    dist = jnp.maximum(qi + CHUNK - kj, 0)
    max_exact = REL_BUCKETS // 2
    dist_f = jnp.maximum(dist, 1).astype(F32)
    large = max_exact + (jnp.log(dist_f / max_exact) / math.log(REL_MAX_DIST / max_exact)
                         * (REL_BUCKETS - max_exact)).astype(jnp.int32)
    large = jnp.minimum(large, REL_BUCKETS - 1)
    return jnp.where(dist < max_exact, dist, large)


def _pad_lanes(v):
    return jnp.pad(v, (0, LANE - v.shape[0])).reshape(1, LANE)


def kernel(x, w_in, norm_pre, norm_post, rel_bias, att_sinks, sg_ln_g, sg_ln_b, sg_w, sg_b, ssm_conv_w, ssm_conv_b, ssm_dt_bias, ssm_a_log, ssm_d, ssm_norm_g, w_br_att, w_br_sg, w_br_ssm, w_out):
    bsz, seq, d = x.shape
    depth = w_in.shape[0]
    assert d == D_MODEL and seq % (2 * CHUNK) == 0
    nb = seq // CHUNK
    x2 = x.reshape(bsz * seq, d)
    bucket = _rel_bucket_table()
    for l in range(depth):
        w_att, w_sg, w_ssm = _branch_weights(w_in[l])
        acc = _att_branch(x2, norm_pre[l], w_att, w_br_att[l].astype(BF16), rel_bias, att_sinks[l], bucket, nb)
        b_full = jnp.repeat(jnp.transpose(sg_b[l]), CHUNK, axis=1)
        acc = _sg_branch(x2, acc, norm_pre[l], w_sg, w_br_sg[l].astype(BF16),
                         sg_ln_g[l], sg_ln_b[l], sg_w[l], b_full)
        cw, cb = ssm_conv_w[l], ssm_conv_b[l]
        x2 = _ssd_branch(x2, acc, norm_pre[l], w_ssm, w_br_ssm[l].astype(BF16), w_out[l].astype(BF16),
                         norm_post[l],
                         cw[:, None, :SSM_WIDTH], cb[:SSM_WIDTH].reshape(1, -1),
                         cw[:, None, SSM_WIDTH:], cb[SSM_WIDTH:].reshape(1, -1),
                         _pad_lanes(ssm_dt_bias[l]), _pad_lanes(ssm_a_log[l]).reshape(LANE, 1),
                         jnp.repeat(ssm_d[l], SSM_HEAD_DIM).reshape(1, -1),
                         ssm_norm_g[l].reshape(1, -1), nb)
    return x2.reshape(bsz, seq, d)
```

```python
import functools
import math

import jax
import jax.numpy as jnp
from jax import lax
from jax.experimental import pallas as pl
from jax.experimental.pallas import tpu as pltpu

F32 = jnp.float32
BF16 = jnp.bfloat16

D_MODEL = 1024
ATT_HEADS = 16
ATT_KV_HEADS = 2
ATT_HEAD_DIM = 64
ATT_GROUP = ATT_HEADS // ATT_KV_HEADS
CHUNK = 128
REL_BUCKETS = 32
REL_MAX_DIST = 128
SG_GROUPS = 8
SG_WIDTH = 1024
SSM_WIDTH = 2048
SSM_HEAD_DIM = 64
SSM_HEADS = 32
SSM_GROUPS = 4
SSM_STATE = 128
SSM_HPG = SSM_HEADS // SSM_GROUPS
SSM_CONV = 4
SSM_BC = 2 * SSM_GROUPS * SSM_STATE
EPS = 1e-6
NEG = -1e30
LOG2E = 1.4426950408889634
LANE = 128

A_Q, A_ZA, A_G, A_K, A_V, A_COLS = 0, 1024, 2048, 3072, 3200, 3328
S_U, S_VS, S_ZS, S_G, S_COLS = 0, 1024, 2048, 3072, 4096
M_ZM, M_XS, M_BC, M_DT, M_G, M_COLS = 0, 2048, 4096, 5120, 5248, 6272
VMEM_LIMIT = 56 * 1024 * 1024
W_PAD_COLS = D_MODEL + LANE


def _cparams():
    return pltpu.CompilerParams(dimension_semantics=("arbitrary",), vmem_limit_bytes=VMEM_LIMIT)


def _sigmoid(x):
    return 0.5 * jnp.tanh(0.5 * x) + 0.5


def _silu(x):
    h = 0.5 * x
    return h * jnp.tanh(h) + h


def _softplus(x):
    return jnp.maximum(x, 0.0) + jnp.log1p(jnp.exp(-jnp.abs(x)))


def _dot(a, b):
    return jnp.dot(a, b, preferred_element_type=F32)


def _dot_nt(a, b):
    return lax.dot_general(a, b, (((1,), (1,)), ((), ())), preferred_element_type=F32)


def _rms_norm_to(x_ref, g_ref, h_ref):
    x = x_ref[...]
    ms = jnp.mean(x * x, axis=-1, keepdims=True)
    h_ref[...] = (x * lax.rsqrt(ms + EPS) * g_ref[...]).astype(BF16)


def _proj_cols(h_ref, w_ref, dst_ref, c0, c1):
    if c1 > c0:
        dst_ref[:, c0:c1] = _dot(h_ref[...], w_ref[:, c0:c1])


def _rms_proj(x_ref, g_ref, w_ref, dst_ref, h_ref):
    _rms_norm_to(x_ref, g_ref, h_ref)
    _proj_cols(h_ref, w_ref, dst_ref, 0, dst_ref.shape[1])


class _Spread:
    def __init__(self, thunks, n_ticks):
        self.thunks, self.n_ticks, self.ticks, self.done = thunks, n_ticks, 0, 0

    def tick(self):
        self.ticks += 1
        while self.done < len(self.thunks) and self.done * self.n_ticks < self.ticks * len(self.thunks):
            self.thunks[self.done]()
            self.done += 1

    def flush(self):
        self.ticks = self.n_ticks
        self.tick()


def _col_chunks(n_cols, n_chunks, width=256):
    tiles = -(-n_cols // width)
    per = [tiles // n_chunks + (1 if k < tiles % n_chunks else 0) for k in range(n_chunks)]
    out, c = [], 0
    for n_tiles in per:
        out.append((c, min(c + n_tiles * width, n_cols)))
        c = min(c + n_tiles * width, n_cols)
    return out


def _split3(a):
    hi = a.astype(BF16)
    r1 = a - hi.astype(F32)
    mid = r1.astype(BF16)
    lo = (r1 - mid.astype(F32)).astype(BF16)
    return hi, mid, lo


def _att_block(src, kv_prev, bias_ref, sink_ref, variant, between=lambda pr: None):
    qi = lax.broadcasted_iota(jnp.int32, (CHUNK, CHUNK), 0)
    ci = lax.broadcasted_iota(jnp.int32, (CHUNK, CHUNK), 1)
    own = ci <= qi
    low = lax.broadcasted_iota(jnp.int32, (2 * CHUNK, LANE), 1) < ATT_HEAD_DIM
    low_q = lax.broadcasted_iota(jnp.int32, (CHUNK, LANE), 1) < ATT_HEAD_DIM
    scale = ATT_HEAD_DIM ** -0.5
    kcat = jnp.concatenate([src[:, A_K:A_K + LANE], kv_prev[:, 0:LANE]], axis=0)
    vcat = jnp.concatenate([src[:, A_V:A_V + LANE], kv_prev[:, LANE:2 * LANE]], axis=0)
    k_roll = pltpu.roll(kcat, ATT_HEAD_DIM, 1)
    v_roll = pltpu.roll(vcat, ATT_HEAD_DIM, 1)
    n_pair = ATT_GROUP // 2
    pairs = []
    hook = 0
    for kvh in range(ATT_KV_HEADS):
        k_lo, k_hi = (kcat, k_roll) if kvh == 0 else (k_roll, kcat)
        v_lo, v_hi = (vcat, v_roll) if kvh == 0 else (v_roll, vcat)
        kz = (jnp.where(low, k_lo, 0.0).astype(BF16), jnp.where(low, 0.0, k_hi).astype(BF16))
        vz = (jnp.where(low, v_lo, 0.0).astype(BF16), jnp.where(low, 0.0, v_hi).astype(BF16))
        pr0 = kvh * n_pair
        q4 = jnp.concatenate([src[:, A_Q + pr * LANE:A_Q + (pr + 1) * LANE]
                              for pr in range(pr0, pr0 + n_pair)], axis=0)
        q4 = (q4 * scale).astype(BF16)
        lhs_cols = []
        recips = [[None, None] for _ in range(n_pair)]
        for par in range(2):
            between(hook)
            hook += 1
            s2 = _dot_nt(q4, kz[par])
            rows = []
            for k in range(n_pair):
                h = 2 * (pr0 + k) + par
                s2k = s2[k * CHUNK:(k + 1) * CHUNK]
                s = jnp.where(own, s2k[:, 0:CHUNK], s2k[:, CHUNK:2 * CHUNK]) + bias_ref[variant, h]
                sink = sink_ref[h]
                m = jnp.maximum(jnp.max(s, axis=-1, keepdims=True), sink)
                p = jnp.exp(s - m)
                den = jnp.sum(p, axis=-1, keepdims=True) + jnp.exp(sink - m)
                recips[k][par] = 1.0 / den
                rows.append(jnp.concatenate([jnp.where(own, p, 0.0), jnp.where(own, 0.0, p)],
                                            axis=1).astype(BF16))
            lhs_cols.append(jnp.concatenate(rows, axis=0))
        between(hook)
        hook += 1
        o4 = _dot(jnp.concatenate(lhs_cols, axis=1), jnp.concatenate(vz, axis=0))
        for k in range(n_pair):
            r_pair = jnp.where(low_q, jnp.broadcast_to(recips[k][0], (CHUNK, LANE)),
                               jnp.broadcast_to(recips[k][1], (CHUNK, LANE)))
            pairs.append(o4[k * CHUNK:(k + 1) * CHUNK] * r_pair)
    y = jnp.concatenate(pairs, axis=-1) * _silu(src[:, A_ZA:A_ZA + 1024])
    between(hook)
    between(hook + 1)
    return y


def _att_kernel(rb_ref, sink_ref, bkt_ref, x0_ref, xa_ref, xb_ref, gpre_ref, w_ref, wbr_ref,
                o_ref, slot_a, slot_b, h_ref, kv_prev, bias_ref, *, nb):
    i = pl.program_id(0)
    chunks = _col_chunks(A_COLS, ATT_HEADS // 2)

    @pl.when(i == 0)
    def _():
        qi = lax.broadcasted_iota(jnp.int32, (CHUNK, CHUNK), 0)
        ci = lax.broadcasted_iota(jnp.int32, (CHUNK, CHUNK), 1)
        own = ci <= qi
        bkt = jnp.where(own, bkt_ref[:, CHUNK:2 * CHUNK], bkt_ref[:, 0:CHUNK])
        for h in range(ATT_HEADS):
            acc = jnp.zeros((CHUNK, CHUNK), F32)
            for b in range(REL_BUCKETS):
                acc = jnp.where(bkt == b, rb_ref[b, h], acc)
            bias_ref[1, h] = acc
            bias_ref[0, h] = jnp.where(own, acc, NEG)
        kv_prev[...] = jnp.zeros_like(kv_prev)
        _rms_proj(x0_ref, gpre_ref, w_ref, slot_a, h_ref)

    def mix(src, variant, rows, x_next_ref, dst):
        _rms_norm_to(x_next_ref, gpre_ref, h_ref)
        y = _att_block(src, kv_prev, bias_ref, sink_ref, variant,
                       between=lambda pr: _proj_cols(h_ref, w_ref, dst, *chunks[pr]))
        kv_prev[...] = src[:, A_K:A_K + 2 * LANE]
        o_ref[rows, :] = _sigmoid(src[:, A_G:A_G + D_MODEL]) * _dot(y.astype(BF16), wbr_ref[:, 0:D_MODEL])

    first = ((2 * i) % nb) == 0
    mix(slot_a, jnp.where(first, 0, 1), slice(0, CHUNK), xa_ref, slot_b)
    mix(slot_b, 1, slice(CHUNK, 2 * CHUNK), xb_ref, slot_a)


def _sg_block(src, lng_ref, lnb_ref, wt_ref, bfull_ref, tick=lambda: None):
    v = src[:, S_VS:S_VS + SG_WIDTH]
    mu = jnp.mean(v, axis=-1, keepdims=True)
    vc = v - mu
    var = jnp.mean(vc * vc, axis=-1, keepdims=True)
    vn = (vc * lax.rsqrt(var + EPS) * lng_ref[...] + lnb_ref[...]).astype(BF16)
    out = []
    for g in range(SG_GROUPS):
        tick()
        sl = slice(g * LANE, (g + 1) * LANE)
        mixed = _dot(wt_ref[g], vn[:, sl]) + bfull_ref[:, sl]
        out.append(src[:, S_U + g * LANE:S_U + (g + 1) * LANE] * mixed
                   * _silu(src[:, S_ZS + g * LANE:S_ZS + (g + 1) * LANE]))
    return jnp.concatenate(out, axis=-1)


def _sg_kernel(x0_ref, xa_ref, xb_ref, acc_ref, gpre_ref, w_ref, wbr_ref,
               lng_ref, lnb_ref, ws_ref, bfull_ref, o_ref, slot_a, slot_b, h_ref, wt_ref):
    i = pl.program_id(0)

    @pl.when(i == 0)
    def _():
        ti = lax.broadcasted_iota(jnp.int32, (CHUNK, CHUNK), 0)
        si = lax.broadcasted_iota(jnp.int32, (CHUNK, CHUNK), 1)
        for g in range(SG_GROUPS):
            wt_ref[g] = jnp.where(si <= ti, ws_ref[g], 0.0).astype(BF16)
        _rms_proj(x0_ref, gpre_ref, w_ref, slot_a, h_ref)

    chunks = _col_chunks(S_COLS, SG_GROUPS)

    def mix(src, rows, x_next_ref, dst):
        _rms_norm_to(x_next_ref, gpre_ref, h_ref)
        spread = _Spread([functools.partial(_proj_cols, h_ref, w_ref, dst, c0, c1) for c0, c1 in chunks],
                         SG_GROUPS)
        y = _sg_block(src, lng_ref, lnb_ref, wt_ref, bfull_ref, tick=spread.tick)
        spread.flush()
        o_ref[rows, :] = (acc_ref[rows, :]
                          + _sigmoid(src[:, S_G:S_G + D_MODEL]) * _dot(y.astype(BF16), wbr_ref[:, 0:D_MODEL]))

    mix(slot_a, slice(0, CHUNK), xa_ref, slot_b)
    mix(slot_b, slice(CHUNK, 2 * CHUNK), xb_ref, slot_a)


def _ssd_block(src, first, cwx_ref, cbx_ref, cwb_ref, cbb_ref, dtb_ref, alog_ref, drow_ref, ng_ref,
               ext_x, ext_b, h_ref, xc_ref, y_ref, tick=lambda: None):
    n_xt, n_bt = SSM_WIDTH // LANE, SSM_BC // LANE
    tail = slice(CHUNK, CHUNK + 8)

    if first is not None:
        @pl.when(first)
        def _():
            h_ref[...] = jnp.zeros_like(h_ref)

    def stage(ext, col0, n_tiles):
        for t in range(n_tiles):
            prev = ext[t, tail, :]
            ext[t, 0:8, :] = prev if first is None else jnp.where(first, 0.0, prev)
            ext[t, 8:8 + CHUNK, :] = src[:, col0 + t * LANE:col0 + (t + 1) * LANE]

    def conv_tile(ext, w_ref, b_ref, t):
        sl = slice(t * LANE, (t + 1) * LANE)
        acc = b_ref[:, sl]
        for k in range(SSM_CONV):
            acc = acc + w_ref[k, :, sl] * ext[t, pl.ds(8 - (SSM_CONV - 1) + k, CHUNK), :]
        return _silu(acc)

    stage(ext_x, M_XS, n_xt)
    stage(ext_b, M_BC, n_bt)
    for t in range(n_xt):
        tick()
        xc_ref[:, t * LANE:(t + 1) * LANE] = conv_tile(ext_x, cwx_ref, cbx_ref, t)
    bcv = []
    for t in range(n_bt):
        tick()
        bcv.append(conv_tile(ext_b, cwb_ref, cbb_ref, t))

    li = lax.broadcasted_iota(jnp.int32, (CHUNK, CHUNK), 0)
    si = lax.broadcasted_iota(jnp.int32, (CHUNK, CHUNK), 1)
    causal = si <= li
    x_dt_t = (src[:, M_DT:M_DT + LANE] + dtb_ref[...]).T[0:SSM_HEADS, :]
    dt_t = _softplus(x_dt_t)
    a_dt_t = dt_t * (-jnp.exp(alog_ref[0:SSM_HEADS, :]))
    upper = jnp.where(li <= si, 1.0, 0.0).astype(BF16)
    parts = _dot(jnp.concatenate(_split3(a_dt_t), axis=0), upper)
    a_cs_t = (parts[0:SSM_HEADS] + parts[SSM_HEADS:2 * SSM_HEADS]
              + parts[2 * SSM_HEADS:3 * SSM_HEADS])
    a2_t = a_cs_t * LOG2E
    a2_last = a2_t[:, CHUNK - 1:CHUNK]
    w_t = dt_t * jnp.exp2(a2_last - a2_t)
    cd_t = jnp.exp2(a2_last)
    a2 = jnp.concatenate([a2_t, jnp.zeros((CHUNK - SSM_HEADS, CHUNK), F32)], axis=0).T
    e2 = jnp.exp2(a2)
    low = lax.broadcasted_iota(jnp.int32, (CHUNK, LANE), 1) < SSM_HEAD_DIM
    low_row = low[0:1]

    for g in range(SSM_GROUPS):
        b_g, c_g = bcv[g], bcv[SSM_GROUPS + g]
        cb = jnp.where(causal, _dot_nt(c_g.astype(BF16), b_g.astype(BF16)), 0.0)
        b_gt = b_g.T
        for pr in range(g * SSM_HPG // 2, (g + 1) * SSM_HPG // 2):
            tick()
            sl = slice(pr * LANE, (pr + 1) * LANE)
            x_pair = xc_ref[:, sl].astype(BF16)
            h_pair = h_ref[:, sl]
            rhs = jnp.concatenate([x_pair, h_pair.astype(BF16)], axis=0)
            lhs_rows, bw_rows = [], []
            for j in (2 * pr, 2 * pr + 1):
                a_col = jnp.broadcast_to(a2[:, j:j + 1], (CHUNK, CHUNK))
                decay = jnp.exp2(jnp.minimum(a_col - a2_t[j:j + 1, :], 0.0))
                m_j = cb * decay * dt_t[j:j + 1, :]
                e_col = jnp.broadcast_to(e2[:, j:j + 1], (CHUNK, CHUNK))
                lhs_rows.append(jnp.concatenate([m_j, c_g * e_col], axis=1).astype(BF16))
                bw_rows.append((b_gt * w_t[j:j + 1, :]).astype(BF16))
            yy = _dot(jnp.concatenate(lhs_rows, axis=0), rhs)
            up = _dot(jnp.concatenate(bw_rows, axis=0), x_pair)
            y_ref[:, sl] = jnp.where(low, yy[0:CHUNK], yy[CHUNK:2 * CHUNK])
            cd_pair = jnp.where(low_row, cd_t[2 * pr:2 * pr + 1, :], cd_t[2 * pr + 1:2 * pr + 2, :])
            h_ref[:, sl] = h_pair * cd_pair + jnp.where(low, up[0:SSM_STATE], up[SSM_STATE:2 * SSM_STATE])

    gw = SSM_WIDTH // SSM_GROUPS
    out = []
    for g in range(SSM_GROUPS):
        tick()
        sl = slice(g * gw, (g + 1) * gw)
        yg = (y_ref[:, sl] + drow_ref[:, sl] * xc_ref[:, sl]) * _silu(src[:, M_ZM + g * gw:M_ZM + (g + 1) * gw])
        yg = yg * lax.rsqrt(jnp.mean(yg * yg, axis=-1, keepdims=True) + EPS)
        out.append(yg * ng_ref[:, sl])
    return jnp.concatenate(out, axis=-1)


def _ssd_kernel(x0_ref, xa_ref, xb_ref, xres_ref, acc_ref, gpre_ref, w_ref, wbr_ref, wout_ref, gpost_ref,
                cwx_ref, cbx_ref, cwb_ref, cbb_ref, dtb_ref, alog_ref, drow_ref, ng_ref,
                o_ref, slot_a, slot_b, hn_ref, ext_x, ext_b, h_ref, xc_ref, y_ref,
                ybf_ref, mg_ref, out_ref, *, nb):
    i = pl.program_id(0)

    @pl.when(i == 0)
    def _():
        ext_x[...] = jnp.zeros_like(ext_x)
        ext_b[...] = jnp.zeros_like(ext_b)
        _rms_proj(x0_ref, gpre_ref, w_ref, slot_a, hn_ref)

    chunks = _col_chunks(M_COLS, -(-M_COLS // 256))
    n_ticks = SSM_WIDTH // LANE + SSM_BC // LANE + SSM_HEADS // 2 + SSM_GROUPS
    out_cols = _col_chunks(D_MODEL, D_MODEL // 256)

    def epilogue(src, rows):
        def branch(c0, c1):
            gate = _sigmoid(src[:, M_G + c0:M_G + c1])
            mg_ref[:, c0:c1] = (acc_ref[rows, c0:c1]
                                + gate * _dot(ybf_ref[...], wbr_ref[:, c0:c1])).astype(BF16)

        def outproj(c0, c1):
            out_ref[:, c0:c1] = _dot(mg_ref[...], wout_ref[:, c0:c1])

        def finish():
            out = out_ref[...]
            ms = jnp.mean(out * out, axis=-1, keepdims=True)
            o_ref[rows, :] = xres_ref[rows, :] + out * lax.rsqrt(ms + EPS) * gpost_ref[...]

        return ([functools.partial(branch, c0, c1) for c0, c1 in out_cols]
                + [functools.partial(outproj, c0, c1) for c0, c1 in out_cols] + [finish])

    def mix(src, first, x_next_ref, dst, pending):
        _rms_norm_to(x_next_ref, gpre_ref, hn_ref)
        proj = [functools.partial(_proj_cols, hn_ref, w_ref, dst, c0, c1) for c0, c1 in chunks]
        spread = _Spread(pending + proj, n_ticks)
        y = _ssd_block(src, first, cwx_ref, cbx_ref, cwb_ref, cbb_ref, dtb_ref, alog_ref, drow_ref, ng_ref,
                       ext_x, ext_b, h_ref, xc_ref, y_ref, tick=spread.tick)
        spread.flush()
        ybf_ref[...] = y.astype(BF16)

    first = ((2 * i) % nb) == 0
    mix(slot_a, first, xa_ref, slot_b, [])
    mix(slot_b, None, xb_ref, slot_a, epilogue(slot_a, slice(0, CHUNK)))
    for thunk in epilogue(slot_b, slice(CHUNK, 2 * CHUNK)):
        thunk()


def _x_specs(n_blk):
    return [pl.BlockSpec((CHUNK, D_MODEL), lambda i: (0, 0)),
            pl.BlockSpec((CHUNK, D_MODEL), lambda i: (2 * i + 1, 0)),
            pl.BlockSpec((CHUNK, D_MODEL), lambda i: (jnp.minimum(2 * i + 2, n_blk - 1), 0))]


def _const(shape):
    return pl.BlockSpec(shape, lambda i: (0,) * len(shape))


def _rows2():
    return pl.BlockSpec((2 * CHUNK, D_MODEL), lambda i: (i, 0))


def _att_branch(x2, g_pre, w_bf, wbr_bf, rel_bias, sinks, bucket, nb):
    t = x2.shape[0]
    n_blk = t // CHUNK
    smem = pl.BlockSpec(memory_space=pltpu.SMEM)
    return pl.pallas_call(
        functools.partial(_att_kernel, nb=nb),
        grid=(n_blk // 2,),
        in_specs=[smem, smem, _const((CHUNK, 2 * CHUNK))] + _x_specs(n_blk)
                 + [_const((1, D_MODEL)), _const((D_MODEL, A_COLS)), _const((1024, W_PAD_COLS))],
        out_specs=_rows2(),
        out_shape=jax.ShapeDtypeStruct((t, D_MODEL), F32),
        scratch_shapes=[pltpu.VMEM((CHUNK, A_COLS), F32), pltpu.VMEM((CHUNK, A_COLS), F32),
                        pltpu.VMEM((CHUNK, D_MODEL), BF16),
                        pltpu.VMEM((CHUNK, 2 * LANE), F32),
                        pltpu.VMEM((2, ATT_HEADS, CHUNK, CHUNK), F32)],
        compiler_params=_cparams(),
        name="att",
    )(rel_bias, sinks, bucket, x2, x2, x2, g_pre.reshape(1, -1), w_bf, wbr_bf)


def _sg_branch(x2, acc, g_pre, w_bf, wbr_bf, ln_g, ln_b, w_s, b_full):
    t = x2.shape[0]
    n_blk = t // CHUNK
    return pl.pallas_call(
        _sg_kernel,
        grid=(n_blk // 2,),
        in_specs=_x_specs(n_blk) + [_rows2(), _const((1, D_MODEL)), _const((D_MODEL, S_COLS)),
                                    _const((SG_WIDTH, W_PAD_COLS)), _const((1, SG_WIDTH)), _const((1, SG_WIDTH)),
                                    _const((SG_GROUPS, CHUNK, CHUNK)), _const((CHUNK, SG_WIDTH))],
        out_specs=_rows2(),
        out_shape=jax.ShapeDtypeStruct((t, D_MODEL), F32),
        scratch_shapes=[pltpu.VMEM((CHUNK, S_COLS), F32), pltpu.VMEM((CHUNK, S_COLS), F32),
                        pltpu.VMEM((CHUNK, D_MODEL), BF16),
                        pltpu.VMEM((SG_GROUPS, CHUNK, CHUNK), BF16)],
        compiler_params=_cparams(),
        name="sg",
    )(x2, x2, x2, acc, g_pre.reshape(1, -1), w_bf, wbr_bf, ln_g.reshape(1, -1), ln_b.reshape(1, -1),
      w_s, b_full)


def _ssd_branch(x2, acc, g_pre, w_bf, wbr_bf, wout_bf, g_post, cwx, cbx, cwb, cbb, dtb, alog, drow, ng, nb):
    t = x2.shape[0]
    n_blk = t // CHUNK
    return pl.pallas_call(
        functools.partial(_ssd_kernel, nb=nb),
        grid=(n_blk // 2,),
        in_specs=_x_specs(n_blk) + [_rows2(), _rows2(), _const((1, D_MODEL)), _const((D_MODEL, M_COLS)),
                                    _const((SSM_WIDTH, W_PAD_COLS)), _const((D_MODEL, W_PAD_COLS)),
                                    _const((1, D_MODEL)),
                                    _const((SSM_CONV, 1, SSM_WIDTH)), _const((1, SSM_WIDTH)),
                                    _const((SSM_CONV, 1, SSM_BC)), _const((1, SSM_BC)),
                                    _const((1, LANE)), _const((LANE, 1)),
                                    _const((1, SSM_WIDTH)), _const((1, SSM_WIDTH))],
        out_specs=_rows2(),
        out_shape=jax.ShapeDtypeStruct((t, D_MODEL), F32),
        scratch_shapes=[pltpu.VMEM((CHUNK, M_COLS), F32), pltpu.VMEM((CHUNK, M_COLS), F32),
                        pltpu.VMEM((CHUNK, D_MODEL), BF16),
                        pltpu.VMEM((SSM_WIDTH // LANE, 8 + CHUNK, LANE), F32),
                        pltpu.VMEM((SSM_BC // LANE, 8 + CHUNK, LANE), F32),
                        pltpu.VMEM((SSM_STATE, SSM_WIDTH), F32),
                        pltpu.VMEM((CHUNK, SSM_WIDTH), F32),
                        pltpu.VMEM((CHUNK, SSM_WIDTH), F32),
                        pltpu.VMEM((CHUNK, SSM_WIDTH), BF16),
                        pltpu.VMEM((CHUNK, D_MODEL), BF16),
                        pltpu.VMEM((CHUNK, D_MODEL), F32)],
        compiler_params=_cparams(),
        name="ssd",
    )(x2, x2, x2, x2, acc, g_pre.reshape(1, -1), w_bf, wbr_bf, wout_bf, g_post.reshape(1, -1),
      cwx, cbx, cwb, cbb, dtb, alog, drow, ng)


def _branch_weights(w):
    q, k, v, za = w[:, 0:1024], w[:, 1024:1152], w[:, 1152:1280], w[:, 1280:2304]
    u, vs, zs = w[:, 2304:3328], w[:, 3328:4352], w[:, 4352:5376]
    zm, xbc, dt = w[:, 5376:7424], w[:, 7424:10496], w[:, 10496:10528]
    g0, g1, g2 = w[:, 10528:11552], w[:, 11552:12576], w[:, 12576:13600]
    dt_pad = jnp.zeros((w.shape[0], LANE - SSM_HEADS), w.dtype)
    w_att = jnp.concatenate([q, za, g0, k, v], axis=1).astype(BF16)
    w_sg = jnp.concatenate([u, vs, zs, g1], axis=1).astype(BF16)
    w_ssm = jnp.concatenate([zm, xbc, dt, dt_pad, g2], axis=1).astype(BF16)
    assert w_att.shape[1] == A_COLS and w_sg.shape[1] == S_COLS and w_ssm.shape[1] == M_COLS
    return w_att, w_sg, w_ssm


def _rel_bucket_table():
    qi = jnp.arange(CHUNK, dtype=jnp.int32)[:, None]
    kj = jnp.arange(2 * CHUNK, dtype=jnp.int32)[None, :]
    dist = jnp.maximum(qi + CHUNK - kj, 0)
    max_exact = REL_BUCKETS // 2
    dist_f = jnp.maximum(dist, 1).astype(F32)
    large = max_exact + (jnp.log(dist_f / max_exact) / math.log(REL_MAX_DIST / max_exact)
                         * (REL_BUCKETS - max_exact)).astype(jnp.int32)
    large = jnp.minimum(large, REL_BUCKETS - 1)
    return jnp.where(dist < max_exact, dist, large)


def _out_weight(w):
    return jnp.pad(w.astype(BF16), ((0, 0), (0, W_PAD_COLS - w.shape[1])))


def _pad_lanes(v):
    return jnp.pad(v, (0, LANE - v.shape[0])).reshape(1, LANE)


def kernel(x, w_in, norm_pre, norm_post, rel_bias, att_sinks, sg_ln_g, sg_ln_b, sg_w, sg_b, ssm_conv_w, ssm_conv_b, ssm_dt_bias, ssm_a_log, ssm_d, ssm_norm_g, w_br_att, w_br_sg, w_br_ssm, w_out):
    bsz, seq, d = x.shape
    depth = w_in.shape[0]
    assert d == D_MODEL and seq % (2 * CHUNK) == 0
    nb = seq // CHUNK
    x2 = x.reshape(bsz * seq, d)
    bucket = _rel_bucket_table()
    for l in range(depth):
        w_att, w_sg, w_ssm = _branch_weights(w_in[l])
        acc = _att_branch(x2, norm_pre[l], w_att, _out_weight(w_br_att[l]), rel_bias, att_sinks[l], bucket, nb)
        b_full = jnp.repeat(jnp.transpose(sg_b[l]), CHUNK, axis=1)
        acc = _sg_branch(x2, acc, norm_pre[l], w_sg, _out_weight(w_br_sg[l]),
                         sg_ln_g[l], sg_ln_b[l], sg_w[l], b_full)
        cw, cb = ssm_conv_w[l], ssm_conv_b[l]
        x2 = _ssd_branch(x2, acc, norm_pre[l], w_ssm, _out_weight(w_br_ssm[l]), _out_weight(w_out[l]),
                         norm_post[l],
                         cw[:, None, :SSM_WIDTH], cb[:SSM_WIDTH].reshape(1, -1),
                         cw[:, None, SSM_WIDTH:], cb[SSM_WIDTH:].reshape(1, -1),
                         _pad_lanes(ssm_dt_bias[l]), _pad_lanes(ssm_a_log[l]).reshape(LANE, 1),
                         jnp.repeat(ssm_d[l], SSM_HEAD_DIM).reshape(1, -1),
                         ssm_norm_g[l].reshape(1, -1), nb)
    return x2.reshape(bsz, seq, d)
```

```python
import functools
import math

import jax
import jax.numpy as jnp
from jax import lax
from jax.experimental import pallas as pl
from jax.experimental.pallas import tpu as pltpu

F32 = jnp.float32
BF16 = jnp.bfloat16

D_MODEL = 1024
ATT_HEADS = 16
ATT_KV_HEADS = 2
ATT_HEAD_DIM = 64
ATT_GROUP = ATT_HEADS // ATT_KV_HEADS
CHUNK = 128
REL_BUCKETS = 32
REL_MAX_DIST = 128
SG_GROUPS = 8
SG_WIDTH = 1024
SSM_WIDTH = 2048
SSM_HEAD_DIM = 64
SSM_HEADS = 32
SSM_GROUPS = 4
SSM_STATE = 128
SSM_HPG = SSM_HEADS // SSM_GROUPS
SSM_CONV = 4
SSM_BC = 2 * SSM_GROUPS * SSM_STATE
EPS = 1e-6
NEG = -1e30
LOG2E = 1.4426950408889634
LANE = 128

A_Q, A_K, A_V, A_ZA, A_G, A_COLS = 0, 1024, 1152, 1280, 2304, 3328
S_U, S_VS, S_ZS, S_G, S_COLS = 0, 1024, 2048, 3072, 4096
M_ZM, M_XS, M_BC, M_DT, M_G, M_COLS = 0, 2048, 4096, 5120, 5248, 6272
VMEM_LIMIT = 56 * 1024 * 1024
W_PAD_COLS = D_MODEL + LANE
S_MAIN_COLS = S_G + LANE


def _cparams():
    return pltpu.CompilerParams(dimension_semantics=("arbitrary",), vmem_limit_bytes=VMEM_LIMIT)


def _sigmoid(x):
    return 0.5 * jnp.tanh(0.5 * x) + 0.5


def _silu(x):
    h = 0.5 * x
    return h * jnp.tanh(h) + h


def _softplus(x):
    return jnp.maximum(x, 0.0) + jnp.log1p(jnp.exp(-jnp.abs(x)))


def _dot(a, b):
    return jnp.dot(a, b, preferred_element_type=F32)


def _dot_nt(a, b):
    return lax.dot_general(a, b, (((1,), (1,)), ((), ())), preferred_element_type=F32)


def _rms_norm_to(x_ref, g_ref, h_ref):
    x = x_ref[...]
    ms = jnp.mean(x * x, axis=-1, keepdims=True)
    h_ref[...] = (x * lax.rsqrt(ms + EPS) * g_ref[...]).astype(BF16)


def _proj_cols(h_ref, w_ref, dst_ref, c0, c1, d0):
    dst_ref[:, d0:d0 + c1 - c0] = _dot(h_ref[...], w_ref[:, c0:c1])


def _proj_plan(parts, width=256):
    plan, d0 = [], 0
    for w_ref, n_cols in parts:
        for c0 in range(0, n_cols, width):
            c1 = min(c0 + width, n_cols)
            plan.append((w_ref, c0, c1, d0 + c0))
        d0 += n_cols
    return plan


def _proj_thunks(h_ref, dst_ref, plan):
    return [functools.partial(_proj_cols, h_ref, w_ref, dst_ref, c0, c1, d0) for w_ref, c0, c1, d0 in plan]


def _rms_proj(x_ref, g_ref, dst_ref, h_ref, plan):
    _rms_norm_to(x_ref, g_ref, h_ref)
    for thunk in _proj_thunks(h_ref, dst_ref, plan):
        thunk()


class _Spread:
    def __init__(self, thunks, n_ticks):
        self.thunks, self.n_ticks, self.ticks, self.done = thunks, n_ticks, 0, 0

    def tick(self):
        self.ticks += 1
        while self.done < len(self.thunks) and self.done * self.n_ticks < self.ticks * len(self.thunks):
            self.thunks[self.done]()
            self.done += 1

    def flush(self):
        self.ticks = self.n_ticks
        self.tick()


def _col_chunks(n_cols, n_chunks, width=256):
    tiles = -(-n_cols // width)
    per = [tiles // n_chunks + (1 if k < tiles % n_chunks else 0) for k in range(n_chunks)]
    out, c = [], 0
    for n_tiles in per:
        out.append((c, min(c + n_tiles * width, n_cols)))
        c = min(c + n_tiles * width, n_cols)
    return out


def _split3(a):
    hi = a.astype(BF16)
    r1 = a - hi.astype(F32)
    mid = r1.astype(BF16)
    lo = (r1 - mid.astype(F32)).astype(BF16)
    return hi, mid, lo


def _att_block(src, kv_prev, bias_ref, sink_ref, variant, tick=lambda: None):
    qi = lax.broadcasted_iota(jnp.int32, (CHUNK, CHUNK), 0)
    ci = lax.broadcasted_iota(jnp.int32, (CHUNK, CHUNK), 1)
    own = ci <= qi
    low = lax.broadcasted_iota(jnp.int32, (2 * CHUNK, LANE), 1) < ATT_HEAD_DIM
    low_q = lax.broadcasted_iota(jnp.int32, (CHUNK, LANE), 1) < ATT_HEAD_DIM
    scale = ATT_HEAD_DIM ** -0.5
    kcat = jnp.concatenate([src[:, A_K:A_K + LANE], kv_prev[:, 0:LANE]], axis=0)
    vcat = jnp.concatenate([src[:, A_V:A_V + LANE], kv_prev[:, LANE:2 * LANE]], axis=0)
    k_roll = pltpu.roll(kcat, ATT_HEAD_DIM, 1)
    v_roll = pltpu.roll(vcat, ATT_HEAD_DIM, 1)
    n_pair = ATT_GROUP // 2
    pairs = []
    for kvh in range(ATT_KV_HEADS):
        k_lo, k_hi = (kcat, k_roll) if kvh == 0 else (k_roll, kcat)
        v_lo, v_hi = (vcat, v_roll) if kvh == 0 else (v_roll, vcat)
        kz = (jnp.where(low, k_lo, 0.0).astype(BF16), jnp.where(low, 0.0, k_hi).astype(BF16))
        vz = (jnp.where(low, v_lo, 0.0).astype(BF16), jnp.where(low, 0.0, v_hi).astype(BF16))
        pr0 = kvh * n_pair
        q4 = jnp.concatenate([src[:, A_Q + pr * LANE:A_Q + (pr + 1) * LANE]
                              for pr in range(pr0, pr0 + n_pair)], axis=0)
        q4 = (q4 * scale).astype(BF16)
        lhs_cols = []
        recips = [[None, None] for _ in range(n_pair)]
        for par in range(2):
            tick()
            s2 = _dot_nt(q4, kz[par])
            rows = []
            for k in range(n_pair):
                h = 2 * (pr0 + k) + par
                s2k = s2[k * CHUNK:(k + 1) * CHUNK]
                s = jnp.where(own, s2k[:, 0:CHUNK], s2k[:, CHUNK:2 * CHUNK]) + bias_ref[variant, h]
                sink = sink_ref[h]
                m = jnp.maximum(jnp.max(s, axis=-1, keepdims=True), sink)
                p = jnp.exp(s - m)
                den = jnp.sum(p, axis=-1, keepdims=True) + jnp.exp(sink - m)
                recips[k][par] = 1.0 / den
                rows.append(jnp.concatenate([jnp.where(own, p, 0.0), jnp.where(own, 0.0, p)],
                                            axis=1).astype(BF16))
            lhs_cols.append(jnp.concatenate(rows, axis=0))
        tick()
        o4 = _dot(jnp.concatenate(lhs_cols, axis=1), jnp.concatenate(vz, axis=0))
        for k in range(n_pair):
            r_pair = jnp.where(low_q, jnp.broadcast_to(recips[k][0], (CHUNK, LANE)),
                               jnp.broadcast_to(recips[k][1], (CHUNK, LANE)))
            pairs.append(o4[k * CHUNK:(k + 1) * CHUNK] * r_pair)
    y = jnp.concatenate(pairs, axis=-1) * _silu(src[:, A_ZA:A_ZA + 1024])
    tick()
    tick()
    return y


def _att_kernel(rb_ref, sink_ref, bkt_ref, x0_ref, xa_ref, xb_ref, gpre_ref, w_ref, wg_ref, wbr_ref,
                o_ref, slot_a, slot_b, h_ref, kv_prev, bias_ref, *, nb):
    i = pl.program_id(0)
    plan = _proj_plan([(w_ref, A_G), (wg_ref, D_MODEL)])
    n_ticks = 2 * (ATT_KV_HEADS + 1) + 2

    @pl.when(i == 0)
    def _():
        qi = lax.broadcasted_iota(jnp.int32, (CHUNK, CHUNK), 0)
        ci = lax.broadcasted_iota(jnp.int32, (CHUNK, CHUNK), 1)
        own = ci <= qi
        bkt = jnp.where(own, bkt_ref[:, CHUNK:2 * CHUNK], bkt_ref[:, 0:CHUNK])
        for h in range(ATT_HEADS):
            acc = jnp.zeros((CHUNK, CHUNK), F32)
            for b in range(REL_BUCKETS):
                acc = jnp.where(bkt == b, rb_ref[b, h], acc)
            bias_ref[1, h] = acc
            bias_ref[0, h] = jnp.where(own, acc, NEG)
        kv_prev[...] = jnp.zeros_like(kv_prev)
        _rms_proj(x0_ref, gpre_ref, slot_a, h_ref, plan)

    def mix(src, variant, rows, x_next_ref, dst):
        _rms_norm_to(x_next_ref, gpre_ref, h_ref)
        spread = _Spread(_proj_thunks(h_ref, dst, plan), n_ticks)
        y = _att_block(src, kv_prev, bias_ref, sink_ref, variant, tick=spread.tick)
        spread.flush()
        kv_prev[...] = src[:, A_K:A_K + 2 * LANE]
        o_ref[rows, :] = _sigmoid(src[:, A_G:A_G + D_MODEL]) * _dot(y.astype(BF16), wbr_ref[:, 0:D_MODEL])

    first = ((2 * i) % nb) == 0
    mix(slot_a, jnp.where(first, 0, 1), slice(0, CHUNK), xa_ref, slot_b)
    mix(slot_b, 1, slice(CHUNK, 2 * CHUNK), xb_ref, slot_a)


def _sg_block(src, lng_ref, lnb_ref, wt_ref, bfull_ref, tick=lambda: None):
    v = src[:, S_VS:S_VS + SG_WIDTH]
    mu = jnp.mean(v, axis=-1, keepdims=True)
    vc = v - mu
    var = jnp.mean(vc * vc, axis=-1, keepdims=True)
    vn = (vc * lax.rsqrt(var + EPS) * lng_ref[...] + lnb_ref[...]).astype(BF16)
    out = []
    for g in range(SG_GROUPS):
        tick()
        sl = slice(g * LANE, (g + 1) * LANE)
        mixed = _dot(wt_ref[g], vn[:, sl]) + bfull_ref[:, sl]
        out.append(src[:, S_U + g * LANE:S_U + (g + 1) * LANE] * mixed
                   * _silu(src[:, S_ZS + g * LANE:S_ZS + (g + 1) * LANE]))
    return jnp.concatenate(out, axis=-1)


def _sg_kernel(x0_ref, xa_ref, xb_ref, acc_ref, gpre_ref, w_ref, wg_ref, wbr_ref,
               lng_ref, lnb_ref, ws_ref, bfull_ref, o_ref, slot_a, slot_b, h_ref, wt_ref):
    i = pl.program_id(0)
    plan = _proj_plan([(w_ref, S_G), (wg_ref, D_MODEL)])

    @pl.when(i == 0)
    def _():
        ti = lax.broadcasted_iota(jnp.int32, (CHUNK, CHUNK), 0)
        si = lax.broadcasted_iota(jnp.int32, (CHUNK, CHUNK), 1)
        for g in range(SG_GROUPS):
            wt_ref[g] = jnp.where(si <= ti, ws_ref[g], 0.0).astype(BF16)
        _rms_proj(x0_ref, gpre_ref, slot_a, h_ref, plan)

    def mix(src, rows, x_next_ref, dst):
        _rms_norm_to(x_next_ref, gpre_ref, h_ref)
        spread = _Spread(_proj_thunks(h_ref, dst, plan), SG_GROUPS)
        y = _sg_block(src, lng_ref, lnb_ref, wt_ref, bfull_ref, tick=spread.tick)
        spread.flush()
        o_ref[rows, :] = (acc_ref[rows, :]
                          + _sigmoid(src[:, S_G:S_G + D_MODEL]) * _dot(y.astype(BF16), wbr_ref[:, 0:D_MODEL]))

    mix(slot_a, slice(0, CHUNK), xa_ref, slot_b)
    mix(slot_b, slice(CHUNK, 2 * CHUNK), xb_ref, slot_a)


def _ssd_block(src, first, cwx_ref, cbx_ref, cwb_ref, cbb_ref, dtb_ref, alog_ref, drow_ref, ng_ref,
               ext_x, ext_b, h_ref, xc_ref, y_ref, tick=lambda: None):
    n_xt, n_bt = SSM_WIDTH // LANE, SSM_BC // LANE
    tail = slice(CHUNK, CHUNK + 8)

    if first is not None:
        @pl.when(first)
        def _():
            h_ref[...] = jnp.zeros_like(h_ref)

    def stage(ext, col0, n_tiles):
        for t in range(n_tiles):
            prev = ext[t, tail, :]
            ext[t, 0:8, :] = prev if first is None else jnp.where(first, 0.0, prev)
            ext[t, 8:8 + CHUNK, :] = src[:, col0 + t * LANE:col0 + (t + 1) * LANE]

    def conv_tile(ext, w_ref, b_ref, t):
        sl = slice(t * LANE, (t + 1) * LANE)
        acc = b_ref[:, sl]
        for k in range(SSM_CONV):
            acc = acc + w_ref[k, :, sl] * ext[t, pl.ds(8 - (SSM_CONV - 1) + k, CHUNK), :]
        return _silu(acc)

    stage(ext_x, M_XS, n_xt)
    stage(ext_b, M_BC, n_bt)
    for t in range(n_xt):
        tick()
        xc_ref[:, t * LANE:(t + 1) * LANE] = conv_tile(ext_x, cwx_ref, cbx_ref, t)
    bcv = []
    for t in range(n_bt):
        tick()
        bcv.append(conv_tile(ext_b, cwb_ref, cbb_ref, t))

    li = lax.broadcasted_iota(jnp.int32, (CHUNK, CHUNK), 0)
    si = lax.broadcasted_iota(jnp.int32, (CHUNK, CHUNK), 1)
    causal = si <= li
    x_dt_t = (src[:, M_DT:M_DT + LANE] + dtb_ref[...]).T[0:SSM_HEADS, :]
    dt_t = _softplus(x_dt_t)
    a_dt_t = dt_t * (-jnp.exp(alog_ref[0:SSM_HEADS, :]))
    upper = jnp.where(li <= si, 1.0, 0.0).astype(BF16)
    parts = _dot(jnp.concatenate(_split3(a_dt_t), axis=0), upper)
    a_cs_t = (parts[0:SSM_HEADS] + parts[SSM_HEADS:2 * SSM_HEADS]
              + parts[2 * SSM_HEADS:3 * SSM_HEADS])
    a2_t = a_cs_t * LOG2E
    a2_last = a2_t[:, CHUNK - 1:CHUNK]
    w_t = dt_t * jnp.exp2(a2_last - a2_t)
    cd_t = jnp.exp2(a2_last)
    a2 = jnp.concatenate([a2_t, jnp.zeros((CHUNK - SSM_HEADS, CHUNK), F32)], axis=0).T
    e2 = jnp.exp2(a2)
    low = lax.broadcasted_iota(jnp.int32, (CHUNK, LANE), 1) < SSM_HEAD_DIM
    low_row = low[0:1]

    for g in range(SSM_GROUPS):
        b_g, c_g = bcv[g], bcv[SSM_GROUPS + g]
        cb = jnp.where(causal, _dot_nt(c_g.astype(BF16), b_g.astype(BF16)), 0.0)
        b_gt = b_g.T
        for pr in range(g * SSM_HPG // 2, (g + 1) * SSM_HPG // 2):
            tick()
            sl = slice(pr * LANE, (pr + 1) * LANE)
            x_pair = xc_ref[:, sl].astype(BF16)
            h_pair = h_ref[:, sl]
            rhs = jnp.concatenate([x_pair, h_pair.astype(BF16)], axis=0)
            lhs_rows, bw_rows = [], []
            for j in (2 * pr, 2 * pr + 1):
                a_col = jnp.broadcast_to(a2[:, j:j + 1], (CHUNK, CHUNK))
                decay = jnp.exp2(jnp.minimum(a_col - a2_t[j:j + 1, :], 0.0))
                m_j = cb * decay * dt_t[j:j + 1, :]
                e_col = jnp.broadcast_to(e2[:, j:j + 1], (CHUNK, CHUNK))
                lhs_rows.append(jnp.concatenate([m_j, c_g * e_col], axis=1).astype(BF16))
                bw_rows.append((b_gt * w_t[j:j + 1, :]).astype(BF16))
            yy = _dot(jnp.concatenate(lhs_rows, axis=0), rhs)
            up = _dot(jnp.concatenate(bw_rows, axis=0), x_pair)
            y_ref[:, sl] = jnp.where(low, yy[0:CHUNK], yy[CHUNK:2 * CHUNK])
            cd_pair = jnp.where(low_row, cd_t[2 * pr:2 * pr + 1, :], cd_t[2 * pr + 1:2 * pr + 2, :])
            h_ref[:, sl] = h_pair * cd_pair + jnp.where(low, up[0:SSM_STATE], up[SSM_STATE:2 * SSM_STATE])

    gw = SSM_WIDTH // SSM_GROUPS
    out = []
    for g in range(SSM_GROUPS):
        tick()
        sl = slice(g * gw, (g + 1) * gw)
        yg = (y_ref[:, sl] + drow_ref[:, sl] * xc_ref[:, sl]) * _silu(src[:, M_ZM + g * gw:M_ZM + (g + 1) * gw])
        yg = yg * lax.rsqrt(jnp.mean(yg * yg, axis=-1, keepdims=True) + EPS)
        out.append(yg * ng_ref[:, sl])
    return jnp.concatenate(out, axis=-1)


def _ssd_kernel(x0_ref, xa_ref, xb_ref, xres_ref, acc_ref, gpre_ref, w_ref, wg_ref, wbr_ref, wout_ref, gpost_ref,
                cwx_ref, cbx_ref, cwb_ref, cbb_ref, dtb_ref, alog_ref, drow_ref, ng_ref,
                o_ref, slot_a, slot_b, hn_ref, ext_x, ext_b, h_ref, xc_ref, y_ref,
                ybf_ref, mg_ref, out_ref, *, nb):
    i = pl.program_id(0)
    plan = _proj_plan([(w_ref, M_G), (wg_ref, D_MODEL)])

    @pl.when(i == 0)
    def _():
        ext_x[...] = jnp.zeros_like(ext_x)
        ext_b[...] = jnp.zeros_like(ext_b)
        _rms_proj(x0_ref, gpre_ref, slot_a, hn_ref, plan)

    n_ticks = SSM_WIDTH // LANE + SSM_BC // LANE + SSM_HEADS // 2 + SSM_GROUPS
    out_cols = _col_chunks(D_MODEL, D_MODEL // 256)

    def epilogue(src, rows):
        def branch(c0, c1):
            gate = _sigmoid(src[:, M_G + c0:M_G + c1])
            mg_ref[:, c0:c1] = (acc_ref[rows, c0:c1]
                                + gate * _dot(ybf_ref[...], wbr_ref[:, c0:c1])).astype(BF16)

        def outproj(c0, c1):
            out_ref[:, c0:c1] = _dot(mg_ref[...], wout_ref[:, c0:c1])

        def finish():
            out = out_ref[...]
            ms = jnp.mean(out * out, axis=-1, keepdims=True)
            o_ref[rows, :] = xres_ref[rows, :] + out * lax.rsqrt(ms + EPS) * gpost_ref[...]

        return ([functools.partial(branch, c0, c1) for c0, c1 in out_cols]
                + [functools.partial(outproj, c0, c1) for c0, c1 in out_cols] + [finish])

    def mix(src, first, x_next_ref, dst, pending):
        _rms_norm_to(x_next_ref, gpre_ref, hn_ref)
        spread = _Spread(pending + _proj_thunks(hn_ref, dst, plan), n_ticks)
        y = _ssd_block(src, first, cwx_ref, cbx_ref, cwb_ref, cbb_ref, dtb_ref, alog_ref, drow_ref, ng_ref,
                       ext_x, ext_b, h_ref, xc_ref, y_ref, tick=spread.tick)
        spread.flush()
        ybf_ref[...] = y.astype(BF16)

    first = ((2 * i) % nb) == 0
    mix(slot_a, first, xa_ref, slot_b, [])
    mix(slot_b, None, xb_ref, slot_a, epilogue(slot_a, slice(0, CHUNK)))
    for thunk in epilogue(slot_b, slice(CHUNK, 2 * CHUNK)):
        thunk()


def _x_specs(n_blk):
    return [pl.BlockSpec((CHUNK, D_MODEL), lambda i: (0, 0)),
            pl.BlockSpec((CHUNK, D_MODEL), lambda i: (2 * i + 1, 0)),
            pl.BlockSpec((CHUNK, D_MODEL), lambda i: (jnp.minimum(2 * i + 2, n_blk - 1), 0))]


def _const(shape):
    return pl.BlockSpec(shape, lambda i: (0,) * len(shape))


def _rows2():
    return pl.BlockSpec((2 * CHUNK, D_MODEL), lambda i: (i, 0))


def _att_branch(x2, g_pre, w_bf, wg_bf, wbr_bf, rel_bias, sinks, bucket, nb):
    t = x2.shape[0]
    n_blk = t // CHUNK
    smem = pl.BlockSpec(memory_space=pltpu.SMEM)
    return pl.pallas_call(
        functools.partial(_att_kernel, nb=nb),
        grid=(n_blk // 2,),
        in_specs=[smem, smem, _const((CHUNK, 2 * CHUNK))] + _x_specs(n_blk)
                 + [_const((1, D_MODEL)), _const((D_MODEL, A_G)), _const((D_MODEL, W_PAD_COLS)),
                    _const((1024, W_PAD_COLS))],
        out_specs=_rows2(),
        out_shape=jax.ShapeDtypeStruct((t, D_MODEL), F32),
        scratch_shapes=[pltpu.VMEM((CHUNK, A_COLS), F32), pltpu.VMEM((CHUNK, A_COLS), F32),
                        pltpu.VMEM((CHUNK, D_MODEL), BF16),
                        pltpu.VMEM((CHUNK, 2 * LANE), F32),
                        pltpu.VMEM((2, ATT_HEADS, CHUNK, CHUNK), F32)],
        compiler_params=_cparams(),
        name="att",
    )(rel_bias, sinks, bucket, x2, x2, x2, g_pre.reshape(1, -1), w_bf, wg_bf, wbr_bf)


def _sg_branch(x2, acc, g_pre, w_bf, wg_bf, wbr_bf, ln_g, ln_b, w_s, b_full):
    t = x2.shape[0]
    n_blk = t // CHUNK
    return pl.pallas_call(
        _sg_kernel,
        grid=(n_blk // 2,),
        in_specs=_x_specs(n_blk) + [_rows2(), _const((1, D_MODEL)), _const((D_MODEL, S_MAIN_COLS)),
                                    _const((D_MODEL, W_PAD_COLS)),
                                    _const((SG_WIDTH, W_PAD_COLS)), _const((1, SG_WIDTH)), _const((1, SG_WIDTH)),
                                    _const((SG_GROUPS, CHUNK, CHUNK)), _const((CHUNK, SG_WIDTH))],
        out_specs=_rows2(),
        out_shape=jax.ShapeDtypeStruct((t, D_MODEL), F32),
        scratch_shapes=[pltpu.VMEM((CHUNK, S_COLS), F32), pltpu.VMEM((CHUNK, S_COLS), F32),
                        pltpu.VMEM((CHUNK, D_MODEL), BF16),
                        pltpu.VMEM((SG_GROUPS, CHUNK, CHUNK), BF16)],
        compiler_params=_cparams(),
        name="sg",
    )(x2, x2, x2, acc, g_pre.reshape(1, -1), w_bf, wg_bf, wbr_bf, ln_g.reshape(1, -1), ln_b.reshape(1, -1),
      w_s, b_full)


def _ssd_branch(x2, acc, g_pre, w_bf, wg_bf, wbr_bf, wout_bf, g_post, cwx, cbx, cwb, cbb, dtb, alog, drow, ng, nb):
    t = x2.shape[0]
    n_blk = t // CHUNK
    return pl.pallas_call(
        functools.partial(_ssd_kernel, nb=nb),
        grid=(n_blk // 2,),
        in_specs=_x_specs(n_blk) + [_rows2(), _rows2(), _const((1, D_MODEL)), _const((D_MODEL, M_G)),
                                    _const((D_MODEL, W_PAD_COLS)),
                                    _const((SSM_WIDTH, W_PAD_COLS)), _const((D_MODEL, W_PAD_COLS)),
                                    _const((1, D_MODEL)),
                                    _const((SSM_CONV, 1, SSM_WIDTH)), _const((1, SSM_WIDTH)),
                                    _const((SSM_CONV, 1, SSM_BC)), _const((1, SSM_BC)),
                                    _const((1, LANE)), _const((LANE, 1)),
                                    _const((1, SSM_WIDTH)), _const((1, SSM_WIDTH))],
        out_specs=_rows2(),
        out_shape=jax.ShapeDtypeStruct((t, D_MODEL), F32),
        scratch_shapes=[pltpu.VMEM((CHUNK, M_COLS), F32), pltpu.VMEM((CHUNK, M_COLS), F32),
                        pltpu.VMEM((CHUNK, D_MODEL), BF16),
                        pltpu.VMEM((SSM_WIDTH // LANE, 8 + CHUNK, LANE), F32),
                        pltpu.VMEM((SSM_BC // LANE, 8 + CHUNK, LANE), F32),
                        pltpu.VMEM((SSM_STATE, SSM_WIDTH), F32),
                        pltpu.VMEM((CHUNK, SSM_WIDTH), F32),
                        pltpu.VMEM((CHUNK, SSM_WIDTH), F32),
                        pltpu.VMEM((CHUNK, SSM_WIDTH), BF16),
                        pltpu.VMEM((CHUNK, D_MODEL), BF16),
                        pltpu.VMEM((CHUNK, D_MODEL), F32)],
        compiler_params=_cparams(),
        name="ssd",
    )(x2, x2, x2, x2, acc, g_pre.reshape(1, -1), w_bf, wg_bf, wbr_bf, wout_bf, g_post.reshape(1, -1),
      cwx, cbx, cwb, cbb, dtb, alog, drow, ng)


def _pad_cols(w, n_cols):
    return jnp.pad(w, ((0, 0), (0, n_cols - w.shape[1])))


def _branch_weights(w):
    cast = lambda a: a.astype(BF16)
    gate = lambda i: _pad_cols(cast(w[:, 10528 + i * D_MODEL:10528 + (i + 1) * D_MODEL]), W_PAD_COLS)
    att = cast(w[:, 0:A_G])
    sg = _pad_cols(cast(w[:, 2304:2304 + S_G]), S_MAIN_COLS)
    ssm = _pad_cols(cast(w[:, 5376:10528]), M_G)
    return (att, gate(0)), (sg, gate(1)), (ssm, gate(2))


def _rel_bucket_table():
    qi = jnp.arange(CHUNK, dtype=jnp.int32)[:, None]
    kj = jnp.arange(2 * CHUNK, dtype=jnp.int32)[None, :]
    dist = jnp.maximum(qi + CHUNK - kj, 0)
    max_exact = REL_BUCKETS // 2
    dist_f = jnp.maximum(dist, 1).astype(F32)
    large = max_exact + (jnp.log(dist_f / max_exact) / math.log(REL_MAX_DIST / max_exact)
                         * (REL_BUCKETS - max_exact)).astype(jnp.int32)
    large = jnp.minimum(large, REL_BUCKETS - 1)
    return jnp.where(dist < max_exact, dist, large)


def _out_weight(w):
    return _pad_cols(w.astype(BF16), W_PAD_COLS)


def _pad_lanes(v):
    return jnp.pad(v, (0, LANE - v.shape[0])).reshape(1, LANE)


def kernel(x, w_in, norm_pre, norm_post, rel_bias, att_sinks, sg_ln_g, sg_ln_b, sg_w, sg_b, ssm_conv_w, ssm_conv_b, ssm_dt_bias, ssm_a_log, ssm_d, ssm_norm_g, w_br_att, w_br_sg, w_br_ssm, w_out):
    bsz, seq, d = x.shape
    depth = w_in.shape[0]
    assert d == D_MODEL and seq % (2 * CHUNK) == 0
    nb = seq // CHUNK
    x2 = x.reshape(bsz * seq, d)
    bucket = _rel_bucket_table()
    for l in range(depth):
        (w_att, wg_att), (w_sg, wg_sg), (w_ssm, wg_ssm) = _branch_weights(w_in[l])
        acc = _att_branch(x2, norm_pre[l], w_att, wg_att, _out_weight(w_br_att[l]), rel_bias, att_sinks[l], bucket, nb)
        b_full = jnp.repeat(jnp.transpose(sg_b[l]), CHUNK, axis=1)
        acc = _sg_branch(x2, acc, norm_pre[l], w_sg, wg_sg, _out_weight(w_br_sg[l]),
                         sg_ln_g[l], sg_ln_b[l], sg_w[l], b_full)
        cw, cb = ssm_conv_w[l], ssm_conv_b[l]
        x2 = _ssd_branch(x2, acc, norm_pre[l], w_ssm, wg_ssm, _out_weight(w_br_ssm[l]), _out_weight(w_out[l]),
                         norm_post[l],
                         cw[:, None, :SSM_WIDTH], cb[:SSM_WIDTH].reshape(1, -1),
                         cw[:, None, SSM_WIDTH:], cb[SSM_WIDTH:].reshape(1, -1),
                         _pad_lanes(ssm_dt_bias[l]), _pad_lanes(ssm_a_log[l]).reshape(LANE, 1),
                         jnp.repeat(ssm_d[l], SSM_HEAD_DIM).reshape(1, -1),
                         ssm_norm_g[l].reshape(1, -1), nb)
    return x2.reshape(bsz, seq, d)
```

```python
import functools
import math

import jax
import jax.numpy as jnp
from jax import lax
from jax.experimental import pallas as pl
from jax.experimental.pallas import tpu as pltpu

F32 = jnp.float32
BF16 = jnp.bfloat16

D_MODEL = 1024
ATT_HEADS = 16
ATT_KV_HEADS = 2
ATT_HEAD_DIM = 64
ATT_GROUP = ATT_HEADS // ATT_KV_HEADS
CHUNK = 128
REL_BUCKETS = 32
REL_MAX_DIST = 128
SG_GROUPS = 8
SG_WIDTH = 1024
SSM_WIDTH = 2048
SSM_HEAD_DIM = 64
SSM_HEADS = 32
SSM_GROUPS = 4
SSM_STATE = 128
SSM_HPG = SSM_HEADS // SSM_GROUPS
SSM_CONV = 4
SSM_BC = 2 * SSM_GROUPS * SSM_STATE
EPS = 1e-6
NEG = -1e30
LOG2E = 1.4426950408889634
LANE = 128

A_Q, A_K, A_V, A_ZA, A_G, A_COLS = 0, 1024, 1152, 1280, 2304, 3328
S_U, S_VS, S_ZS, S_G, S_COLS = 0, 1024, 2048, 3072, 4096
M_ZM, M_XS, M_BC, M_DT, M_G, M_COLS = 0, 2048, 4096, 5120, 5248, 6272
VMEM_LIMIT = 56 * 1024 * 1024
W_PAD_COLS = D_MODEL + LANE
S_MAIN_COLS = S_G + LANE
W_SG_COL0, W_SSM_COL0, W_GATE_COL0 = 2304, 5376, 10528


def _cparams():
    return pltpu.CompilerParams(dimension_semantics=("arbitrary",), vmem_limit_bytes=VMEM_LIMIT)


def _sigmoid(x):
    return 0.5 * jnp.tanh(0.5 * x) + 0.5


def _silu(x):
    h = 0.5 * x
    return h * jnp.tanh(h) + h


def _softplus(x):
    return jnp.maximum(x, 0.0) + jnp.log1p(jnp.exp(-jnp.abs(x)))


def _dot(a, b):
    return jnp.dot(a, b, preferred_element_type=F32)


def _dot_nt(a, b):
    return lax.dot_general(a, b, (((1,), (1,)), ((), ())), preferred_element_type=F32)


def _rms_norm_to(x_ref, g_ref, h_ref):
    x = x_ref[...]
    ms = jnp.mean(x * x, axis=-1, keepdims=True)
    h_ref[...] = (x * lax.rsqrt(ms + EPS) * g_ref[...]).astype(BF16)


def _proj_cols(h_ref, w_ref, dst_ref, c0, c1, d0):
    dst_ref[:, d0:d0 + c1 - c0] = _dot(h_ref[...], w_ref[:, c0:c1])


def _proj_plan(parts, width=256):
    plan, d0 = [], 0
    for w_ref, n_cols in parts:
        for c0 in range(0, n_cols, width):
            c1 = min(c0 + width, n_cols)
            plan.append((w_ref, c0, c1, d0 + c0))
        d0 += n_cols
    return plan


def _proj_thunks(h_ref, dst_ref, plan):
    return [functools.partial(_proj_cols, h_ref, w_ref, dst_ref, c0, c1, d0) for w_ref, c0, c1, d0 in plan]


def _rms_proj(x_ref, g_ref, dst_ref, h_ref, plan):
    _rms_norm_to(x_ref, g_ref, h_ref)
    for thunk in _proj_thunks(h_ref, dst_ref, plan):
        thunk()


STAGE_ROWS, STAGE_COLS = 1024, 512


def _weight_jobs(src_hbm, layer, k_rows, col0, n_cols, dst_ref):
    jobs = []
    for r0 in range(0, k_rows, STAGE_ROWS):
        for c in range(0, n_cols, STAGE_COLS):
            w = min(STAGE_COLS, n_cols - c)
            jobs.append((src_hbm.at[layer, r0:r0 + STAGE_ROWS, col0 + c:col0 + c + w], dst_ref, r0, c, w))
    return jobs


def _load_weights(jobs, stage, sem):
    def dma(k):
        w = jobs[k][4]
        return pltpu.make_async_copy(jobs[k][0], stage.at[k % 2, :, 0:w], sem.at[k % 2])

    dma(0).start()
    for k, (_, dst, r0, d0, w) in enumerate(jobs):
        if k + 1 < len(jobs):
            dma(k + 1).start()
        dma(k).wait()
        dst[r0:r0 + STAGE_ROWS, d0:d0 + w] = stage[k % 2, :, 0:w].astype(BF16)


class _Spread:
    def __init__(self, thunks, n_ticks):
        self.thunks, self.n_ticks, self.ticks, self.done = thunks, n_ticks, 0, 0

    def tick(self):
        self.ticks += 1
        while self.done < len(self.thunks) and self.done * self.n_ticks < self.ticks * len(self.thunks):
            self.thunks[self.done]()
            self.done += 1

    def flush(self):
        self.ticks = self.n_ticks
        self.tick()


def _col_chunks(n_cols, n_chunks, width=256):
    tiles = -(-n_cols // width)
    per = [tiles // n_chunks + (1 if k < tiles % n_chunks else 0) for k in range(n_chunks)]
    out, c = [], 0
    for n_tiles in per:
        out.append((c, min(c + n_tiles * width, n_cols)))
        c = min(c + n_tiles * width, n_cols)
    return out


def _split3(a):
    hi = a.astype(BF16)
    r1 = a - hi.astype(F32)
    mid = r1.astype(BF16)
    lo = (r1 - mid.astype(F32)).astype(BF16)
    return hi, mid, lo


def _att_block(src, kv_prev, bias_ref, sink_ref, variant, tick=lambda: None):
    qi = lax.broadcasted_iota(jnp.int32, (CHUNK, CHUNK), 0)
    ci = lax.broadcasted_iota(jnp.int32, (CHUNK, CHUNK), 1)
    own = ci <= qi
    low = lax.broadcasted_iota(jnp.int32, (2 * CHUNK, LANE), 1) < ATT_HEAD_DIM
    low_q = lax.broadcasted_iota(jnp.int32, (CHUNK, LANE), 1) < ATT_HEAD_DIM
    scale = ATT_HEAD_DIM ** -0.5
    kcat = jnp.concatenate([src[:, A_K:A_K + LANE], kv_prev[:, 0:LANE]], axis=0)
    vcat = jnp.concatenate([src[:, A_V:A_V + LANE], kv_prev[:, LANE:2 * LANE]], axis=0)
    k_roll = pltpu.roll(kcat, ATT_HEAD_DIM, 1)
    v_roll = pltpu.roll(vcat, ATT_HEAD_DIM, 1)
    n_pair = ATT_GROUP // 2
    pairs = []
    for kvh in range(ATT_KV_HEADS):
        k_lo, k_hi = (kcat, k_roll) if kvh == 0 else (k_roll, kcat)
        v_lo, v_hi = (vcat, v_roll) if kvh == 0 else (v_roll, vcat)
        kz = (jnp.where(low, k_lo, 0.0).astype(BF16), jnp.where(low, 0.0, k_hi).astype(BF16))
        vz = (jnp.where(low, v_lo, 0.0).astype(BF16), jnp.where(low, 0.0, v_hi).astype(BF16))
        pr0 = kvh * n_pair
        q4 = jnp.concatenate([src[:, A_Q + pr * LANE:A_Q + (pr + 1) * LANE]
                              for pr in range(pr0, pr0 + n_pair)], axis=0)
        q4 = (q4 * scale).astype(BF16)
        lhs_cols = []
        recips = [[None, None] for _ in range(n_pair)]
        for par in range(2):
            tick()
            s2 = _dot_nt(q4, kz[par])
            rows = []
            for k in range(n_pair):
                h = 2 * (pr0 + k) + par
                s2k = s2[k * CHUNK:(k + 1) * CHUNK]
                s = jnp.where(own, s2k[:, 0:CHUNK], s2k[:, CHUNK:2 * CHUNK]) + bias_ref[variant, h]
                sink = sink_ref[h]
                m = jnp.maximum(jnp.max(s, axis=-1, keepdims=True), sink)
                p = jnp.exp(s - m)
                den = jnp.sum(p, axis=-1, keepdims=True) + jnp.exp(sink - m)
                recips[k][par] = 1.0 / den
                rows.append(jnp.concatenate([jnp.where(own, p, 0.0), jnp.where(own, 0.0, p)],
                                            axis=1).astype(BF16))
            lhs_cols.append(jnp.concatenate(rows, axis=0))
        tick()
        o4 = _dot(jnp.concatenate(lhs_cols, axis=1), jnp.concatenate(vz, axis=0))
        for k in range(n_pair):
            r_pair = jnp.where(low_q, jnp.broadcast_to(recips[k][0], (CHUNK, LANE)),
                               jnp.broadcast_to(recips[k][1], (CHUNK, LANE)))
            pairs.append(o4[k * CHUNK:(k + 1) * CHUNK] * r_pair)
    y = jnp.concatenate(pairs, axis=-1) * _silu(src[:, A_ZA:A_ZA + 1024])
    tick()
    tick()
    return y


def _att_kernel(rb_ref, sink_ref, bkt_ref, x0_ref, xa_ref, xb_ref, gpre_ref, win_hbm, wgate_hbm, wbr_hbm,
                o_ref, w_ref, wg_ref, wbr_ref, stage, sem, slot_a, slot_b, h_ref, kv_prev, bias_ref,
                *, nb, layer):
    i = pl.program_id(0)
    plan = _proj_plan([(w_ref, A_G), (wg_ref, D_MODEL)])
    n_ticks = 2 * (ATT_KV_HEADS + 1) + 2

    @pl.when(i == 0)
    def _():
        qi = lax.broadcasted_iota(jnp.int32, (CHUNK, CHUNK), 0)
        ci = lax.broadcasted_iota(jnp.int32, (CHUNK, CHUNK), 1)
        own = ci <= qi
        bkt = jnp.where(own, bkt_ref[:, CHUNK:2 * CHUNK], bkt_ref[:, 0:CHUNK])
        for h in range(ATT_HEADS):
            acc = jnp.zeros((CHUNK, CHUNK), F32)
            for b in range(REL_BUCKETS):
                acc = jnp.where(bkt == b, rb_ref[b, h], acc)
            bias_ref[1, h] = acc
            bias_ref[0, h] = jnp.where(own, acc, NEG)
        kv_prev[...] = jnp.zeros_like(kv_prev)
        _load_weights(_weight_jobs(win_hbm, layer, D_MODEL, 0, A_G, w_ref)
                      + _weight_jobs(wgate_hbm, layer, D_MODEL, 0, D_MODEL, wg_ref)
                      + _weight_jobs(wbr_hbm, layer, 1024, 0, D_MODEL, wbr_ref), stage, sem)
        _rms_proj(x0_ref, gpre_ref, slot_a, h_ref, plan)

    def mix(src, variant, rows, x_next_ref, dst):
        _rms_norm_to(x_next_ref, gpre_ref, h_ref)
        spread = _Spread(_proj_thunks(h_ref, dst, plan), n_ticks)
        y = _att_block(src, kv_prev, bias_ref, sink_ref, variant, tick=spread.tick)
        spread.flush()
        kv_prev[...] = src[:, A_K:A_K + 2 * LANE]
        o_ref[rows, :] = _sigmoid(src[:, A_G:A_G + D_MODEL]) * _dot(y.astype(BF16), wbr_ref[:, 0:D_MODEL])

    first = ((2 * i) % nb) == 0
    mix(slot_a, jnp.where(first, 0, 1), slice(0, CHUNK), xa_ref, slot_b)
    mix(slot_b, 1, slice(CHUNK, 2 * CHUNK), xb_ref, slot_a)


def _sg_block(src, lng_ref, lnb_ref, wt_ref, bfull_ref, tick=lambda: None):
    v = src[:, S_VS:S_VS + SG_WIDTH]
    mu = jnp.mean(v, axis=-1, keepdims=True)
    vc = v - mu
    var = jnp.mean(vc * vc, axis=-1, keepdims=True)
    vn = (vc * lax.rsqrt(var + EPS) * lng_ref[...] + lnb_ref[...]).astype(BF16)
    out = []
    for g in range(SG_GROUPS):
        tick()
        sl = slice(g * LANE, (g + 1) * LANE)
        mixed = _dot(wt_ref[g], vn[:, sl]) + bfull_ref[:, sl]
        out.append(src[:, S_U + g * LANE:S_U + (g + 1) * LANE] * mixed
                   * _silu(src[:, S_ZS + g * LANE:S_ZS + (g + 1) * LANE]))
    return jnp.concatenate(out, axis=-1)


def _sg_kernel(x0_ref, xa_ref, xb_ref, acc_ref, gpre_ref, win_hbm, wgate_hbm, wbr_hbm,
               lng_ref, lnb_ref, ws_ref, bfull_ref, o_ref, w_ref, wg_ref, wbr_ref, stage, sem,
               slot_a, slot_b, h_ref, wt_ref, *, layer):
    i = pl.program_id(0)
    plan = _proj_plan([(w_ref, S_G), (wg_ref, D_MODEL)])

    @pl.when(i == 0)
    def _():
        ti = lax.broadcasted_iota(jnp.int32, (CHUNK, CHUNK), 0)
        si = lax.broadcasted_iota(jnp.int32, (CHUNK, CHUNK), 1)
        for g in range(SG_GROUPS):
            wt_ref[g] = jnp.where(si <= ti, ws_ref[g], 0.0).astype(BF16)
        _load_weights(_weight_jobs(win_hbm, layer, D_MODEL, W_SG_COL0, S_G, w_ref)
                      + _weight_jobs(wgate_hbm, layer, D_MODEL, D_MODEL, D_MODEL, wg_ref)
                      + _weight_jobs(wbr_hbm, layer, SG_WIDTH, 0, D_MODEL, wbr_ref), stage, sem)
        _rms_proj(x0_ref, gpre_ref, slot_a, h_ref, plan)

    def mix(src, rows, x_next_ref, dst):
        _rms_norm_to(x_next_ref, gpre_ref, h_ref)
        spread = _Spread(_proj_thunks(h_ref, dst, plan), SG_GROUPS)
        y = _sg_block(src, lng_ref, lnb_ref, wt_ref, bfull_ref, tick=spread.tick)
        spread.flush()
        o_ref[rows, :] = (acc_ref[rows, :]
                          + _sigmoid(src[:, S_G:S_G + D_MODEL]) * _dot(y.astype(BF16), wbr_ref[:, 0:D_MODEL]))

    mix(slot_a, slice(0, CHUNK), xa_ref, slot_b)
    mix(slot_b, slice(CHUNK, 2 * CHUNK), xb_ref, slot_a)


def _ssd_block(src, first, cwx_ref, cbx_ref, cwb_ref, cbb_ref, dtb_ref, alog_ref, drow_ref, ng_ref,
               ext_x, ext_b, h_ref, xc_ref, y_ref, tick=lambda: None):
    n_xt, n_bt = SSM_WIDTH // LANE, SSM_BC // LANE
    tail = slice(CHUNK, CHUNK + 8)

    if first is not None:
        @pl.when(first)
        def _():
            h_ref[...] = jnp.zeros_like(h_ref)

    def stage(ext, col0, n_tiles):
        for t in range(n_tiles):
            prev = ext[t, tail, :]
            ext[t, 0:8, :] = prev if first is None else jnp.where(first, 0.0, prev)
            ext[t, 8:8 + CHUNK, :] = src[:, col0 + t * LANE:col0 + (t + 1) * LANE]

    def conv_tile(ext, w_ref, b_ref, t):
        sl = slice(t * LANE, (t + 1) * LANE)
        acc = b_ref[:, sl]
        for k in range(SSM_CONV):
            acc = acc + w_ref[k, :, sl] * ext[t, pl.ds(8 - (SSM_CONV - 1) + k, CHUNK), :]
        return _silu(acc)

    stage(ext_x, M_XS, n_xt)
    stage(ext_b, M_BC, n_bt)
    for t in range(n_xt):
        tick()
        xc_ref[:, t * LANE:(t + 1) * LANE] = conv_tile(ext_x, cwx_ref, cbx_ref, t)
    bcv = []
    for t in range(n_bt):
        tick()
        bcv.append(conv_tile(ext_b, cwb_ref, cbb_ref, t))

    li = lax.broadcasted_iota(jnp.int32, (CHUNK, CHUNK), 0)
    si = lax.broadcasted_iota(jnp.int32, (CHUNK, CHUNK), 1)
    causal = si <= li
    x_dt_t = (src[:, M_DT:M_DT + LANE] + dtb_ref[...]).T[0:SSM_HEADS, :]
    dt_t = _softplus(x_dt_t)
    a_dt_t = dt_t * (-jnp.exp(alog_ref[0:SSM_HEADS, :]))
    upper = jnp.where(li <= si, 1.0, 0.0).astype(BF16)
    parts = _dot(jnp.concatenate(_split3(a_dt_t), axis=0), upper)
    a_cs_t = (parts[0:SSM_HEADS] + parts[SSM_HEADS:2 * SSM_HEADS]
              + parts[2 * SSM_HEADS:3 * SSM_HEADS])
    a2_t = a_cs_t * LOG2E
    a2_last = a2_t[:, CHUNK - 1:CHUNK]
    w_t = dt_t * jnp.exp2(a2_last - a2_t)
    cd_t = jnp.exp2(a2_last)
    a2 = jnp.concatenate([a2_t, jnp.zeros((CHUNK - SSM_HEADS, CHUNK), F32)], axis=0).T
    e2 = jnp.exp2(a2)
    low = lax.broadcasted_iota(jnp.int32, (CHUNK, LANE), 1) < SSM_HEAD_DIM
    low_row = low[0:1]

    for g in range(SSM_GROUPS):
        b_g, c_g = bcv[g], bcv[SSM_GROUPS + g]
        cb = jnp.where(causal, _dot_nt(c_g.astype(BF16), b_g.astype(BF16)), 0.0)
        b_gt = b_g.T
        for pr in range(g * SSM_HPG // 2, (g + 1) * SSM_HPG // 2):
            tick()
            sl = slice(pr * LANE, (pr + 1) * LANE)
            x_pair = xc_ref[:, sl].astype(BF16)
            h_pair = h_ref[:, sl]
            rhs = jnp.concatenate([x_pair, h_pair.astype(BF16)], axis=0)
            lhs_rows, bw_rows = [], []
            for j in (2 * pr, 2 * pr + 1):
                a_col = jnp.broadcast_to(a2[:, j:j + 1], (CHUNK, CHUNK))
                decay = jnp.exp2(jnp.minimum(a_col - a2_t[j:j + 1, :], 0.0))
                m_j = cb * decay * dt_t[j:j + 1, :]
                e_col = jnp.broadcast_to(e2[:, j:j + 1], (CHUNK, CHUNK))
                lhs_rows.append(jnp.concatenate([m_j, c_g * e_col], axis=1).astype(BF16))
                bw_rows.append((b_gt * w_t[j:j + 1, :]).astype(BF16))
            yy = _dot(jnp.concatenate(lhs_rows, axis=0), rhs)
            up = _dot(jnp.concatenate(bw_rows, axis=0), x_pair)
            y_ref[:, sl] = jnp.where(low, yy[0:CHUNK], yy[CHUNK:2 * CHUNK])
            cd_pair = jnp.where(low_row, cd_t[2 * pr:2 * pr + 1, :], cd_t[2 * pr + 1:2 * pr + 2, :])
            h_ref[:, sl] = h_pair * cd_pair + jnp.where(low, up[0:SSM_STATE], up[SSM_STATE:2 * SSM_STATE])

    gw = SSM_WIDTH // SSM_GROUPS
    out = []
    for g in range(SSM_GROUPS):
        tick()
        sl = slice(g * gw, (g + 1) * gw)
        yg = (y_ref[:, sl] + drow_ref[:, sl] * xc_ref[:, sl]) * _silu(src[:, M_ZM + g * gw:M_ZM + (g + 1) * gw])
        yg = yg * lax.rsqrt(jnp.mean(yg * yg, axis=-1, keepdims=True) + EPS)
        out.append(yg * ng_ref[:, sl])
    return jnp.concatenate(out, axis=-1)


def _ssd_kernel(x0_ref, xa_ref, xb_ref, xres_ref, acc_ref, gpre_ref, win_hbm, wgate_hbm, wbr_hbm, wout_hbm,
                gpost_ref, cwx_ref, cbx_ref, cwb_ref, cbb_ref, dtb_ref, alog_ref, drow_ref, ng_ref,
                o_ref, w_ref, wg_ref, wbr_ref, wout_ref, stage, sem, slot_a, slot_b, hn_ref, ext_x, ext_b,
                h_ref, xc_ref, y_ref, ybf_ref, mg_ref, out_ref, *, nb, layer):
    i = pl.program_id(0)
    plan = _proj_plan([(w_ref, M_G), (wg_ref, D_MODEL)])

    @pl.when(i == 0)
    def _():
        ext_x[...] = jnp.zeros_like(ext_x)
        ext_b[...] = jnp.zeros_like(ext_b)
        _load_weights(_weight_jobs(win_hbm, layer, D_MODEL, W_SSM_COL0, M_G, w_ref)
                      + _weight_jobs(wgate_hbm, layer, D_MODEL, 2 * D_MODEL, D_MODEL, wg_ref)
                      + _weight_jobs(wbr_hbm, layer, SSM_WIDTH, 0, D_MODEL, wbr_ref)
                      + _weight_jobs(wout_hbm, layer, D_MODEL, 0, D_MODEL, wout_ref), stage, sem)
        _rms_proj(x0_ref, gpre_ref, slot_a, hn_ref, plan)

    n_ticks = SSM_WIDTH // LANE + SSM_BC // LANE + SSM_HEADS // 2 + SSM_GROUPS
    out_cols = _col_chunks(D_MODEL, D_MODEL // 256)

    def epilogue(src, rows):
        def branch(c0, c1):
            gate = _sigmoid(src[:, M_G + c0:M_G + c1])
            mg_ref[:, c0:c1] = (acc_ref[rows, c0:c1]
                                + gate * _dot(ybf_ref[...], wbr_ref[:, c0:c1])).astype(BF16)

        def outproj(c0, c1):
            out_ref[:, c0:c1] = _dot(mg_ref[...], wout_ref[:, c0:c1])

        def finish():
            out = out_ref[...]
            ms = jnp.mean(out * out, axis=-1, keepdims=True)
            o_ref[rows, :] = xres_ref[rows, :] + out * lax.rsqrt(ms + EPS) * gpost_ref[...]

        return ([functools.partial(branch, c0, c1) for c0, c1 in out_cols]
                + [functools.partial(outproj, c0, c1) for c0, c1 in out_cols] + [finish])

    def mix(src, first, x_next_ref, dst, pending):
        _rms_norm_to(x_next_ref, gpre_ref, hn_ref)
        spread = _Spread(pending + _proj_thunks(hn_ref, dst, plan), n_ticks)
        y = _ssd_block(src, first, cwx_ref, cbx_ref, cwb_ref, cbb_ref, dtb_ref, alog_ref, drow_ref, ng_ref,
                       ext_x, ext_b, h_ref, xc_ref, y_ref, tick=spread.tick)
        spread.flush()
        ybf_ref[...] = y.astype(BF16)

    first = ((2 * i) % nb) == 0
    mix(slot_a, first, xa_ref, slot_b, [])
    mix(slot_b, None, xb_ref, slot_a, epilogue(slot_a, slice(0, CHUNK)))
    for thunk in epilogue(slot_b, slice(CHUNK, 2 * CHUNK)):
        thunk()


def _x_specs(n_blk):
    return [pl.BlockSpec((CHUNK, D_MODEL), lambda i: (0, 0)),
            pl.BlockSpec((CHUNK, D_MODEL), lambda i: (2 * i + 1, 0)),
            pl.BlockSpec((CHUNK, D_MODEL), lambda i: (jnp.minimum(2 * i + 2, n_blk - 1), 0))]


def _const(shape):
    return pl.BlockSpec(shape, lambda i: (0,) * len(shape))


def _rows2():
    return pl.BlockSpec((2 * CHUNK, D_MODEL), lambda i: (i, 0))


def _weight_scratch(shapes):
    return ([pltpu.VMEM(s, BF16) for s in shapes]
            + [pltpu.VMEM((2, STAGE_ROWS, STAGE_COLS), F32), pltpu.SemaphoreType.DMA((2,))])


def _any():
    return pl.BlockSpec(memory_space=pl.ANY)


def _att_branch(x2, g_pre, w_in, w_gate, w_br, layer, rel_bias, sinks, bucket, nb):
    t = x2.shape[0]
    n_blk = t // CHUNK
    smem = pl.BlockSpec(memory_space=pltpu.SMEM)
    return pl.pallas_call(
        functools.partial(_att_kernel, nb=nb, layer=layer),
        grid=(n_blk // 2,),
        in_specs=[smem, smem, _const((CHUNK, 2 * CHUNK))] + _x_specs(n_blk)
                 + [_const((1, D_MODEL)), _any(), _any(), _any()],
        out_specs=_rows2(),
        out_shape=jax.ShapeDtypeStruct((t, D_MODEL), F32),
        scratch_shapes=_weight_scratch([(D_MODEL, A_G), (D_MODEL, W_PAD_COLS), (1024, W_PAD_COLS)])
                       + [pltpu.VMEM((CHUNK, A_COLS), F32), pltpu.VMEM((CHUNK, A_COLS), F32),
                        pltpu.VMEM((CHUNK, D_MODEL), BF16),
                        pltpu.VMEM((CHUNK, 2 * LANE), F32),
                        pltpu.VMEM((2, ATT_HEADS, CHUNK, CHUNK), F32)],
        compiler_params=_cparams(),
        name="att",
    )(rel_bias, sinks, bucket, x2, x2, x2, g_pre.reshape(1, -1), w_in, w_gate, w_br)


def _sg_branch(x2, acc, g_pre, w_in, w_gate, w_br, layer, ln_g, ln_b, w_s, b_full):
    t = x2.shape[0]
    n_blk = t // CHUNK
    return pl.pallas_call(
        functools.partial(_sg_kernel, layer=layer),
        grid=(n_blk // 2,),
        in_specs=_x_specs(n_blk) + [_rows2(), _const((1, D_MODEL)), _any(), _any(), _any(),
                                    _const((1, SG_WIDTH)), _const((1, SG_WIDTH)),
                                    _const((SG_GROUPS, CHUNK, CHUNK)), _const((CHUNK, SG_WIDTH))],
        out_specs=_rows2(),
        out_shape=jax.ShapeDtypeStruct((t, D_MODEL), F32),
        scratch_shapes=_weight_scratch([(D_MODEL, S_MAIN_COLS), (D_MODEL, W_PAD_COLS), (SG_WIDTH, W_PAD_COLS)])
                       + [pltpu.VMEM((CHUNK, S_COLS), F32), pltpu.VMEM((CHUNK, S_COLS), F32),
                        pltpu.VMEM((CHUNK, D_MODEL), BF16),
                        pltpu.VMEM((SG_GROUPS, CHUNK, CHUNK), BF16)],
        compiler_params=_cparams(),
        name="sg",
    )(x2, x2, x2, acc, g_pre.reshape(1, -1), w_in, w_gate, w_br, ln_g.reshape(1, -1), ln_b.reshape(1, -1),
      w_s, b_full)


def _ssd_branch(x2, acc, g_pre, w_in, w_gate, w_br, w_out, layer, g_post, cwx, cbx, cwb, cbb, dtb, alog, drow,
                ng, nb):
    t = x2.shape[0]
    n_blk = t // CHUNK
    return pl.pallas_call(
        functools.partial(_ssd_kernel, nb=nb, layer=layer),
        grid=(n_blk // 2,),
        in_specs=_x_specs(n_blk) + [_rows2(), _rows2(), _const((1, D_MODEL)), _any(), _any(), _any(), _any(),
                                    _const((1, D_MODEL)),
                                    _const((SSM_CONV, 1, SSM_WIDTH)), _const((1, SSM_WIDTH)),
                                    _const((SSM_CONV, 1, SSM_BC)), _const((1, SSM_BC)),
                                    _const((1, LANE)), _const((LANE, 1)),
                                    _const((1, SSM_WIDTH)), _const((1, SSM_WIDTH))],
        out_specs=_rows2(),
        out_shape=jax.ShapeDtypeStruct((t, D_MODEL), F32),
        scratch_shapes=_weight_scratch([(D_MODEL, M_G), (D_MODEL, W_PAD_COLS), (SSM_WIDTH, W_PAD_COLS),
                                        (D_MODEL, W_PAD_COLS)])
                       + [pltpu.VMEM((CHUNK, M_COLS), F32), pltpu.VMEM((CHUNK, M_COLS), F32),
                        pltpu.VMEM((CHUNK, D_MODEL), BF16),
                        pltpu.VMEM((SSM_WIDTH // LANE, 8 + CHUNK, LANE), F32),
                        pltpu.VMEM((SSM_BC // LANE, 8 + CHUNK, LANE), F32),
                        pltpu.VMEM((SSM_STATE, SSM_WIDTH), F32),
                        pltpu.VMEM((CHUNK, SSM_WIDTH), F32),
                        pltpu.VMEM((CHUNK, SSM_WIDTH), F32),
                        pltpu.VMEM((CHUNK, SSM_WIDTH), BF16),
                        pltpu.VMEM((CHUNK, D_MODEL), BF16),
                        pltpu.VMEM((CHUNK, D_MODEL), F32)],
        compiler_params=_cparams(),
        name="ssd",
    )(x2, x2, x2, x2, acc, g_pre.reshape(1, -1), w_in, w_gate, w_br, w_out, g_post.reshape(1, -1),
      cwx, cbx, cwb, cbb, dtb, alog, drow, ng)


def _rel_bucket_table():
    qi = jnp.arange(CHUNK, dtype=jnp.int32)[:, None]
    kj = jnp.arange(2 * CHUNK, dtype=jnp.int32)[None, :]
    dist = jnp.maximum(qi + CHUNK - kj, 0)
    max_exact = REL_BUCKETS // 2
    dist_f = jnp.maximum(dist, 1).astype(F32)
    large = max_exact + (jnp.log(dist_f / max_exact) / math.log(REL_MAX_DIST / max_exact)
                         * (REL_BUCKETS - max_exact)).astype(jnp.int32)
    large = jnp.minimum(large, REL_BUCKETS - 1)
    return jnp.where(dist < max_exact, dist, large)


def _pad_lanes(v):
    return jnp.pad(v, (0, LANE - v.shape[0])).reshape(1, LANE)


def kernel(x, w_in, norm_pre, norm_post, rel_bias, att_sinks, sg_ln_g, sg_ln_b, sg_w, sg_b, ssm_conv_w, ssm_conv_b, ssm_dt_bias, ssm_a_log, ssm_d, ssm_norm_g, w_br_att, w_br_sg, w_br_ssm, w_out):
    bsz, seq, d = x.shape
    depth = w_in.shape[0]
    assert d == D_MODEL and seq % (2 * CHUNK) == 0
    nb = seq // CHUNK
    x2 = x.reshape(bsz * seq, d)
    bucket = _rel_bucket_table()
    w_gate = w_in[:, :, W_GATE_COL0:]
    for l in range(depth):
        acc = _att_branch(x2, norm_pre[l], w_in, w_gate, w_br_att, l, rel_bias, att_sinks[l], bucket, nb)
        b_full = jnp.repeat(jnp.transpose(sg_b[l]), CHUNK, axis=1)
        acc = _sg_branch(x2, acc, norm_pre[l], w_in, w_gate, w_br_sg, l, sg_ln_g[l], sg_ln_b[l], sg_w[l], b_full)
        cw, cb = ssm_conv_w[l], ssm_conv_b[l]
        x2 = _ssd_branch(x2, acc, norm_pre[l], w_in, w_gate, w_br_ssm, w_out, l, norm_post[l],
                         cw[:, None, :SSM_WIDTH], cb[:SSM_WIDTH].reshape(1, -1),
                         cw[:, None, SSM_WIDTH:], cb[SSM_WIDTH:].reshape(1, -1),
                         _pad_lanes(ssm_dt_bias[l]), _pad_lanes(ssm_a_log[l]).reshape(LANE, 1),
                         jnp.repeat(ssm_d[l], SSM_HEAD_DIM).reshape(1, -1),
                         ssm_norm_g[l].reshape(1, -1), nb)
    return x2.reshape(bsz, seq, d)
```

```python
import functools
import math

import jax
import jax.numpy as jnp
from jax import lax
from jax.experimental import pallas as pl
from jax.experimental.pallas import tpu as pltpu

F32 = jnp.float32
BF16 = jnp.bfloat16

D_MODEL = 1024
ATT_HEADS = 16
ATT_KV_HEADS = 2
ATT_HEAD_DIM = 64
ATT_GROUP = ATT_HEADS // ATT_KV_HEADS
CHUNK = 128
REL_BUCKETS = 32
REL_MAX_DIST = 128
SG_GROUPS = 8
SG_WIDTH = 1024
SSM_WIDTH = 2048
SSM_HEAD_DIM = 64
SSM_HEADS = 32
SSM_GROUPS = 4
SSM_STATE = 128
SSM_HPG = SSM_HEADS // SSM_GROUPS
SSM_CONV = 4
SSM_BC = 2 * SSM_GROUPS * SSM_STATE
EPS = 1e-6
NEG = -1e30
LOG2E = 1.4426950408889634
LANE = 128

A_Q, A_K, A_V, A_ZA, A_G, A_COLS = 0, 1024, 1152, 1280, 2304, 3328
S_U, S_VS, S_ZS, S_G, S_COLS = 0, 1024, 2048, 3072, 4096
M_ZM, M_XS, M_BC, M_DT, M_G, M_COLS = 0, 2048, 4096, 5120, 5248, 6272
VMEM_LIMIT = 56 * 1024 * 1024
W_PAD_COLS = D_MODEL + LANE
S_MAIN_COLS = S_G + LANE
W_SG_COL0, W_SSM_COL0, W_GATE_COL0 = 2304, 5376, 10528


def _cparams():
    return pltpu.CompilerParams(dimension_semantics=("arbitrary",), vmem_limit_bytes=VMEM_LIMIT)


def _sigmoid(x):
    return 0.5 * jnp.tanh(0.5 * x) + 0.5


def _silu(x):
    h = 0.5 * x
    return h * jnp.tanh(h) + h


def _softplus(x):
    return jnp.maximum(x, 0.0) + jnp.log1p(jnp.exp(-jnp.abs(x)))


def _dot(a, b):
    return jnp.dot(a, b, preferred_element_type=F32)


def _dot_nt(a, b):
    return lax.dot_general(a, b, (((1,), (1,)), ((), ())), preferred_element_type=F32)


def _rms_norm_to(x_ref, g_ref, h_ref):
    x = x_ref[...]
    ms = jnp.mean(x * x, axis=-1, keepdims=True)
    h_ref[...] = (x * lax.rsqrt(ms + EPS) * g_ref[...]).astype(BF16)


def _proj_cols(h_ref, w_ref, dst_ref, c0, c1, d0):
    dst_ref[:, d0:d0 + c1 - c0] = _dot(h_ref[...], w_ref[:, c0:c1])


def _proj_plan(parts, width=256):
    plan, d0 = [], 0
    for w_ref, n_cols in parts:
        for c0 in range(0, n_cols, width):
            c1 = min(c0 + width, n_cols)
            plan.append((w_ref, c0, c1, d0 + c0))
        d0 += n_cols
    return plan


def _proj_thunks(h_ref, dst_ref, plan):
    return [functools.partial(_proj_cols, h_ref, w_ref, dst_ref, c0, c1, d0) for w_ref, c0, c1, d0 in plan]


def _rms_proj(x_ref, g_ref, dst_ref, h_ref, plan):
    _rms_norm_to(x_ref, g_ref, h_ref)
    for thunk in _proj_thunks(h_ref, dst_ref, plan):
        thunk()


STAGE_ROWS, STAGE_COLS = 1024, 512


def _weight_jobs(src_hbm, layer, k_rows, col0, n_cols, dst_ref):
    jobs = []
    for r0 in range(0, k_rows, STAGE_ROWS):
        for c in range(0, n_cols, STAGE_COLS):
            w = min(STAGE_COLS, n_cols - c)
            jobs.append((src_hbm.at[layer, r0:r0 + STAGE_ROWS, col0 + c:col0 + c + w], dst_ref, r0, c, w))
    return jobs


def _load_weights(direct, jobs, stage, sem):
    copies = [pltpu.make_async_copy(src, dst, sem.at[2 + n]) for n, (src, dst) in enumerate(direct)]
    for copy in copies:
        copy.start()

    def dma(k):
        w = jobs[k][4]
        return pltpu.make_async_copy(jobs[k][0], stage.at[k % 2, :, 0:w], sem.at[k % 2])

    dma(0).start()
    for k, (_, dst, r0, d0, w) in enumerate(jobs):
        if k + 1 < len(jobs):
            dma(k + 1).start()
        dma(k).wait()
        dst[r0:r0 + STAGE_ROWS, d0:d0 + w] = stage[k % 2, :, 0:w].astype(BF16)
    for copy in copies:
        copy.wait()


class _Spread:
    def __init__(self, thunks, n_ticks):
        self.thunks, self.n_ticks, self.ticks, self.done = thunks, n_ticks, 0, 0

    def tick(self):
        self.ticks += 1
        while self.done < len(self.thunks) and self.done * self.n_ticks < self.ticks * len(self.thunks):
            self.thunks[self.done]()
            self.done += 1

    def flush(self):
        self.ticks = self.n_ticks
        self.tick()


def _col_chunks(n_cols, n_chunks, width=256):
    tiles = -(-n_cols // width)
    per = [tiles // n_chunks + (1 if k < tiles % n_chunks else 0) for k in range(n_chunks)]
    out, c = [], 0
    for n_tiles in per:
        out.append((c, min(c + n_tiles * width, n_cols)))
        c = min(c + n_tiles * width, n_cols)
    return out


def _split3(a):
    hi = a.astype(BF16)
    r1 = a - hi.astype(F32)
    mid = r1.astype(BF16)
    lo = (r1 - mid.astype(F32)).astype(BF16)
    return hi, mid, lo


def _att_block(src, kv_prev, bias_ref, sink_ref, variant, tick=lambda: None):
    qi = lax.broadcasted_iota(jnp.int32, (CHUNK, CHUNK), 0)
    ci = lax.broadcasted_iota(jnp.int32, (CHUNK, CHUNK), 1)
    own = ci <= qi
    low = lax.broadcasted_iota(jnp.int32, (2 * CHUNK, LANE), 1) < ATT_HEAD_DIM
    low_q = lax.broadcasted_iota(jnp.int32, (CHUNK, LANE), 1) < ATT_HEAD_DIM
    scale = ATT_HEAD_DIM ** -0.5
    kcat = jnp.concatenate([src[:, A_K:A_K + LANE], kv_prev[:, 0:LANE]], axis=0)
    vcat = jnp.concatenate([src[:, A_V:A_V + LANE], kv_prev[:, LANE:2 * LANE]], axis=0)
    k_roll = pltpu.roll(kcat, ATT_HEAD_DIM, 1)
    v_roll = pltpu.roll(vcat, ATT_HEAD_DIM, 1)
    n_pair = ATT_GROUP // 2
    pairs = []
    for kvh in range(ATT_KV_HEADS):
        k_lo, k_hi = (kcat, k_roll) if kvh == 0 else (k_roll, kcat)
        v_lo, v_hi = (vcat, v_roll) if kvh == 0 else (v_roll, vcat)
        kz = (jnp.where(low, k_lo, 0.0).astype(BF16), jnp.where(low, 0.0, k_hi).astype(BF16))
        vz = (jnp.where(low, v_lo, 0.0).astype(BF16), jnp.where(low, 0.0, v_hi).astype(BF16))
        pr0 = kvh * n_pair
        q4 = jnp.concatenate([src[:, A_Q + pr * LANE:A_Q + (pr + 1) * LANE]
                              for pr in range(pr0, pr0 + n_pair)], axis=0)
        q4 = (q4 * scale).astype(BF16)
        lhs_cols = []
        recips = [[None, None] for _ in range(n_pair)]
        for par in range(2):
            tick()
            s2 = _dot_nt(q4, kz[par])
            rows = []
            for k in range(n_pair):
                h = 2 * (pr0 + k) + par
                s2k = s2[k * CHUNK:(k + 1) * CHUNK]
                s = jnp.where(own, s2k[:, 0:CHUNK], s2k[:, CHUNK:2 * CHUNK]) + bias_ref[variant, h]
                sink = sink_ref[h]
                m = jnp.maximum(jnp.max(s, axis=-1, keepdims=True), sink)
                p = jnp.exp(s - m)
                den = jnp.sum(p, axis=-1, keepdims=True) + jnp.exp(sink - m)
                recips[k][par] = 1.0 / den
                rows.append(jnp.concatenate([jnp.where(own, p, 0.0), jnp.where(own, 0.0, p)],
                                            axis=1).astype(BF16))
            lhs_cols.append(jnp.concatenate(rows, axis=0))
        tick()
        o4 = _dot(jnp.concatenate(lhs_cols, axis=1), jnp.concatenate(vz, axis=0))
        for k in range(n_pair):
            r_pair = jnp.where(low_q, jnp.broadcast_to(recips[k][0], (CHUNK, LANE)),
                               jnp.broadcast_to(recips[k][1], (CHUNK, LANE)))
            pairs.append(o4[k * CHUNK:(k + 1) * CHUNK] * r_pair)
    y = jnp.concatenate(pairs, axis=-1) * _silu(src[:, A_ZA:A_ZA + 1024])
    tick()
    tick()
    return y


def _att_kernel(rb_ref, sink_ref, bkt_ref, x0_ref, xa_ref, xb_ref, gpre_ref, win_hbm, wgate_hbm, wbr_hbm,
                o_ref, w_ref, wg_ref, wbr_ref, stage, sem, slot_a, slot_b, h_ref, kv_prev, bias_ref,
                *, nb, layer):
    i = pl.program_id(0)
    plan = _proj_plan([(w_ref, A_G), (wg_ref, D_MODEL)])
    n_ticks = 2 * (ATT_KV_HEADS + 1) + 2

    @pl.when(i == 0)
    def _():
        qi = lax.broadcasted_iota(jnp.int32, (CHUNK, CHUNK), 0)
        ci = lax.broadcasted_iota(jnp.int32, (CHUNK, CHUNK), 1)
        own = ci <= qi
        bkt = jnp.where(own, bkt_ref[:, CHUNK:2 * CHUNK], bkt_ref[:, 0:CHUNK])
        for h in range(ATT_HEADS):
            acc = jnp.zeros((CHUNK, CHUNK), F32)
            for b in range(REL_BUCKETS):
                acc = jnp.where(bkt == b, rb_ref[b, h], acc)
            bias_ref[1, h] = acc
            bias_ref[0, h] = jnp.where(own, acc, NEG)
        kv_prev[...] = jnp.zeros_like(kv_prev)
        _load_weights([(win_hbm.at[layer, :, 0:A_G], w_ref),
                       (wgate_hbm.at[layer, :, 0:D_MODEL], wg_ref.at[:, 0:D_MODEL])],
                      _weight_jobs(wbr_hbm, layer, 1024, 0, D_MODEL, wbr_ref), stage, sem)
        _rms_proj(x0_ref, gpre_ref, slot_a, h_ref, plan)

    def mix(src, variant, rows, x_next_ref, dst):
        _rms_norm_to(x_next_ref, gpre_ref, h_ref)
        spread = _Spread(_proj_thunks(h_ref, dst, plan), n_ticks)
        y = _att_block(src, kv_prev, bias_ref, sink_ref, variant, tick=spread.tick)
        spread.flush()
        kv_prev[...] = src[:, A_K:A_K + 2 * LANE]
        o_ref[rows, :] = _sigmoid(src[:, A_G:A_G + D_MODEL]) * _dot(y.astype(BF16), wbr_ref[:, 0:D_MODEL])

    first = ((2 * i) % nb) == 0
    mix(slot_a, jnp.where(first, 0, 1), slice(0, CHUNK), xa_ref, slot_b)
    mix(slot_b, 1, slice(CHUNK, 2 * CHUNK), xb_ref, slot_a)


def _sg_block(src, lng_ref, lnb_ref, wt_ref, bfull_ref, tick=lambda: None):
    v = src[:, S_VS:S_VS + SG_WIDTH]
    mu = jnp.mean(v, axis=-1, keepdims=True)
    vc = v - mu
    var = jnp.mean(vc * vc, axis=-1, keepdims=True)
    vn = (vc * lax.rsqrt(var + EPS) * lng_ref[...] + lnb_ref[...]).astype(BF16)
    out = []
    for g in range(SG_GROUPS):
        tick()
        sl = slice(g * LANE, (g + 1) * LANE)
        mixed = _dot(wt_ref[g], vn[:, sl]) + bfull_ref[:, sl]
        out.append(src[:, S_U + g * LANE:S_U + (g + 1) * LANE] * mixed
                   * _silu(src[:, S_ZS + g * LANE:S_ZS + (g + 1) * LANE]))
    return jnp.concatenate(out, axis=-1)


def _sg_kernel(x0_ref, xa_ref, xb_ref, acc_ref, gpre_ref, win_hbm, wgate_hbm, wbr_hbm,
               lng_ref, lnb_ref, ws_ref, bfull_ref, o_ref, w_ref, wg_ref, wbr_ref, stage, sem,
               slot_a, slot_b, h_ref, wt_ref, *, layer):
    i = pl.program_id(0)
    plan = _proj_plan([(w_ref, S_G), (wg_ref, D_MODEL)])

    @pl.when(i == 0)
    def _():
        ti = lax.broadcasted_iota(jnp.int32, (CHUNK, CHUNK), 0)
        si = lax.broadcasted_iota(jnp.int32, (CHUNK, CHUNK), 1)
        for g in range(SG_GROUPS):
            wt_ref[g] = jnp.where(si <= ti, ws_ref[g], 0.0).astype(BF16)
        _load_weights([(win_hbm.at[layer, :, W_SG_COL0:W_SG_COL0 + S_MAIN_COLS], w_ref),
                       (wgate_hbm.at[layer, :, D_MODEL:2 * D_MODEL], wg_ref.at[:, 0:D_MODEL])],
                      _weight_jobs(wbr_hbm, layer, SG_WIDTH, 0, D_MODEL, wbr_ref), stage, sem)
        _rms_proj(x0_ref, gpre_ref, slot_a, h_ref, plan)

    def mix(src, rows, x_next_ref, dst):
        _rms_norm_to(x_next_ref, gpre_ref, h_ref)
        spread = _Spread(_proj_thunks(h_ref, dst, plan), SG_GROUPS)
        y = _sg_block(src, lng_ref, lnb_ref, wt_ref, bfull_ref, tick=spread.tick)
        spread.flush()
        o_ref[rows, :] = (acc_ref[rows, :]
                          + _sigmoid(src[:, S_G:S_G + D_MODEL]) * _dot(y.astype(BF16), wbr_ref[:, 0:D_MODEL]))

    mix(slot_a, slice(0, CHUNK), xa_ref, slot_b)
    mix(slot_b, slice(CHUNK, 2 * CHUNK), xb_ref, slot_a)


def _ssd_block(src, first, cwx_ref, cbx_ref, cwb_ref, cbb_ref, dtb_ref, alog_ref, drow_ref, ng_ref,
               ext_x, ext_b, h_ref, xc_ref, y_ref, tick=lambda: None):
    n_xt, n_bt = SSM_WIDTH // LANE, SSM_BC // LANE
    tail = slice(CHUNK, CHUNK + 8)

    if first is not None:
        @pl.when(first)
        def _():
            h_ref[...] = jnp.zeros_like(h_ref)

    def stage(ext, col0, n_tiles):
        for t in range(n_tiles):
            prev = ext[t, tail, :]
            ext[t, 0:8, :] = prev if first is None else jnp.where(first, 0.0, prev)
            ext[t, 8:8 + CHUNK, :] = src[:, col0 + t * LANE:col0 + (t + 1) * LANE]

    def conv_tile(ext, w_ref, b_ref, t):
        sl = slice(t * LANE, (t + 1) * LANE)
        acc = b_ref[:, sl]
        for k in range(SSM_CONV):
            acc = acc + w_ref[k, :, sl] * ext[t, pl.ds(8 - (SSM_CONV - 1) + k, CHUNK), :]
        return _silu(acc)

    stage(ext_x, M_XS, n_xt)
    stage(ext_b, M_BC, n_bt)
    for t in range(n_xt):
        tick()
        xc_ref[:, t * LANE:(t + 1) * LANE] = conv_tile(ext_x, cwx_ref, cbx_ref, t)
    bcv = []
    for t in range(n_bt):
        tick()
        bcv.append(conv_tile(ext_b, cwb_ref, cbb_ref, t))

    li = lax.broadcasted_iota(jnp.int32, (CHUNK, CHUNK), 0)
    si = lax.broadcasted_iota(jnp.int32, (CHUNK, CHUNK), 1)
    causal = si <= li
    x_dt_t = (src[:, M_DT:M_DT + LANE] + dtb_ref[...]).T[0:SSM_HEADS, :]
    dt_t = _softplus(x_dt_t)
    a_dt_t = dt_t * (-jnp.exp(alog_ref[0:SSM_HEADS, :]))
    upper = jnp.where(li <= si, 1.0, 0.0).astype(BF16)
    parts = _dot(jnp.concatenate(_split3(a_dt_t), axis=0), upper)
    a_cs_t = (parts[0:SSM_HEADS] + parts[SSM_HEADS:2 * SSM_HEADS]
              + parts[2 * SSM_HEADS:3 * SSM_HEADS])
    a2_t = a_cs_t * LOG2E
    a2_last = a2_t[:, CHUNK - 1:CHUNK]
    w_t = dt_t * jnp.exp2(a2_last - a2_t)
    cd_t = jnp.exp2(a2_last)
    a2 = jnp.concatenate([a2_t, jnp.zeros((CHUNK - SSM_HEADS, CHUNK), F32)], axis=0).T
    e2 = jnp.exp2(a2)
    low = lax.broadcasted_iota(jnp.int32, (CHUNK, LANE), 1) < SSM_HEAD_DIM
    low_row = low[0:1]

    for g in range(SSM_GROUPS):
        b_g, c_g = bcv[g], bcv[SSM_GROUPS + g]
        cb = jnp.where(causal, _dot_nt(c_g.astype(BF16), b_g.astype(BF16)), 0.0)
        b_gt = b_g.T
        for pr in range(g * SSM_HPG // 2, (g + 1) * SSM_HPG // 2):
            tick()
            sl = slice(pr * LANE, (pr + 1) * LANE)
            x_pair = xc_ref[:, sl].astype(BF16)
            h_pair = h_ref[:, sl]
            rhs = jnp.concatenate([x_pair, h_pair.astype(BF16)], axis=0)
            lhs_rows, bw_rows = [], []
            for j in (2 * pr, 2 * pr + 1):
                a_col = jnp.broadcast_to(a2[:, j:j + 1], (CHUNK, CHUNK))
                decay = jnp.exp2(jnp.minimum(a_col - a2_t[j:j + 1, :], 0.0))
                m_j = cb * decay * dt_t[j:j + 1, :]
                e_col = jnp.broadcast_to(e2[:, j:j + 1], (CHUNK, CHUNK))
                lhs_rows.append(jnp.concatenate([m_j, c_g * e_col], axis=1).astype(BF16))
                bw_rows.append((b_gt * w_t[j:j + 1, :]).astype(BF16))
            yy = _dot(jnp.concatenate(lhs_rows, axis=0), rhs)
            up = _dot(jnp.concatenate(bw_rows, axis=0), x_pair)
            y_ref[:, sl] = jnp.where(low, yy[0:CHUNK], yy[CHUNK:2 * CHUNK])
            cd_pair = jnp.where(low_row, cd_t[2 * pr:2 * pr + 1, :], cd_t[2 * pr + 1:2 * pr + 2, :])
            h_ref[:, sl] = h_pair * cd_pair + jnp.where(low, up[0:SSM_STATE], up[SSM_STATE:2 * SSM_STATE])

    gw = SSM_WIDTH // SSM_GROUPS
    out = []
    for g in range(SSM_GROUPS):
        tick()
        sl = slice(g * gw, (g + 1) * gw)
        yg = (y_ref[:, sl] + drow_ref[:, sl] * xc_ref[:, sl]) * _silu(src[:, M_ZM + g * gw:M_ZM + (g + 1) * gw])
        yg = yg * lax.rsqrt(jnp.mean(yg * yg, axis=-1, keepdims=True) + EPS)
        out.append(yg * ng_ref[:, sl])
    return jnp.concatenate(out, axis=-1)


def _ssd_kernel(x0_ref, xa_ref, xb_ref, xres_ref, acc_ref, gpre_ref, win_hbm, wgate_hbm, wbr_hbm, wout_hbm,
                gpost_ref, cwx_ref, cbx_ref, cwb_ref, cbb_ref, dtb_ref, alog_ref, drow_ref, ng_ref,
                o_ref, w_ref, wg_ref, wbr_ref, wout_ref, stage, sem, slot_a, slot_b, hn_ref, ext_x, ext_b,
                h_ref, xc_ref, y_ref, ybf_ref, mg_ref, out_ref, *, nb, layer):
    i = pl.program_id(0)
    plan = _proj_plan([(w_ref, M_G), (wg_ref, D_MODEL)])

    @pl.when(i == 0)
    def _():
        ext_x[...] = jnp.zeros_like(ext_x)
        ext_b[...] = jnp.zeros_like(ext_b)
        _load_weights([(win_hbm.at[layer, :, W_SSM_COL0:W_SSM_COL0 + M_G], w_ref),
                       (wgate_hbm.at[layer, :, 2 * D_MODEL:3 * D_MODEL], wg_ref.at[:, 0:D_MODEL])],
                      _weight_jobs(wbr_hbm, layer, SSM_WIDTH, 0, D_MODEL, wbr_ref)
                      + _weight_jobs(wout_hbm, layer, D_MODEL, 0, D_MODEL, wout_ref), stage, sem)
        _rms_proj(x0_ref, gpre_ref, slot_a, hn_ref, plan)

    n_ticks = SSM_WIDTH // LANE + SSM_BC // LANE + SSM_HEADS // 2 + SSM_GROUPS
    out_cols = _col_chunks(D_MODEL, D_MODEL // 256)

    def epilogue(src, rows):
        def branch(c0, c1):
            gate = _sigmoid(src[:, M_G + c0:M_G + c1])
            mg_ref[:, c0:c1] = (acc_ref[rows, c0:c1]
                                + gate * _dot(ybf_ref[...], wbr_ref[:, c0:c1])).astype(BF16)

        def outproj(c0, c1):
            out_ref[:, c0:c1] = _dot(mg_ref[...], wout_ref[:, c0:c1])

        def finish():
            out = out_ref[...]
            ms = jnp.mean(out * out, axis=-1, keepdims=True)
            o_ref[rows, :] = xres_ref[rows, :] + out * lax.rsqrt(ms + EPS) * gpost_ref[...]

        return ([functools.partial(branch, c0, c1) for c0, c1 in out_cols]
                + [functools.partial(outproj, c0, c1) for c0, c1 in out_cols] + [finish])

    def mix(src, first, x_next_ref, dst, pending):
        _rms_norm_to(x_next_ref, gpre_ref, hn_ref)
        spread = _Spread(pending + _proj_thunks(hn_ref, dst, plan), n_ticks)
        y = _ssd_block(src, first, cwx_ref, cbx_ref, cwb_ref, cbb_ref, dtb_ref, alog_ref, drow_ref, ng_ref,
                       ext_x, ext_b, h_ref, xc_ref, y_ref, tick=spread.tick)
        spread.flush()
        ybf_ref[...] = y.astype(BF16)

    first = ((2 * i) % nb) == 0
    mix(slot_a, first, xa_ref, slot_b, [])
    mix(slot_b, None, xb_ref, slot_a, epilogue(slot_a, slice(0, CHUNK)))
    for thunk in epilogue(slot_b, slice(CHUNK, 2 * CHUNK)):
        thunk()


def _x_specs(n_blk):
    return [pl.BlockSpec((CHUNK, D_MODEL), lambda i: (0, 0)),
            pl.BlockSpec((CHUNK, D_MODEL), lambda i: (2 * i + 1, 0)),
            pl.BlockSpec((CHUNK, D_MODEL), lambda i: (jnp.minimum(2 * i + 2, n_blk - 1), 0))]


def _const(shape):
    return pl.BlockSpec(shape, lambda i: (0,) * len(shape))


def _rows2():
    return pl.BlockSpec((2 * CHUNK, D_MODEL), lambda i: (i, 0))


def _weight_scratch(shapes):
    return ([pltpu.VMEM(s, BF16) for s in shapes]
            + [pltpu.VMEM((2, STAGE_ROWS, STAGE_COLS), F32), pltpu.SemaphoreType.DMA((4,))])


def _any():
    return pl.BlockSpec(memory_space=pl.ANY)


def _att_branch(x2, g_pre, w_in, w_gate, w_br, layer, rel_bias, sinks, bucket, nb):
    t = x2.shape[0]
    n_blk = t // CHUNK
    smem = pl.BlockSpec(memory_space=pltpu.SMEM)
    return pl.pallas_call(
        functools.partial(_att_kernel, nb=nb, layer=layer),
        grid=(n_blk // 2,),
        in_specs=[smem, smem, _const((CHUNK, 2 * CHUNK))] + _x_specs(n_blk)
                 + [_const((1, D_MODEL)), _any(), _any(), _any()],
        out_specs=_rows2(),
        out_shape=jax.ShapeDtypeStruct((t, D_MODEL), F32),
        scratch_shapes=_weight_scratch([(D_MODEL, A_G), (D_MODEL, W_PAD_COLS), (1024, W_PAD_COLS)])
                       + [pltpu.VMEM((CHUNK, A_COLS), F32), pltpu.VMEM((CHUNK, A_COLS), F32),
                        pltpu.VMEM((CHUNK, D_MODEL), BF16),
                        pltpu.VMEM((CHUNK, 2 * LANE), F32),
                        pltpu.VMEM((2, ATT_HEADS, CHUNK, CHUNK), F32)],
        compiler_params=_cparams(),
        name="att",
    )(rel_bias, sinks, bucket, x2, x2, x2, g_pre.reshape(1, -1), w_in, w_gate, w_br)


def _sg_branch(x2, acc, g_pre, w_in, w_gate, w_br, layer, ln_g, ln_b, w_s, b_full):
    t = x2.shape[0]
    n_blk = t // CHUNK
    return pl.pallas_call(
        functools.partial(_sg_kernel, layer=layer),
        grid=(n_blk // 2,),
        in_specs=_x_specs(n_blk) + [_rows2(), _const((1, D_MODEL)), _any(), _any(), _any(),
                                    _const((1, SG_WIDTH)), _const((1, SG_WIDTH)),
                                    _const((SG_GROUPS, CHUNK, CHUNK)), _const((CHUNK, SG_WIDTH))],
        out_specs=_rows2(),
        out_shape=jax.ShapeDtypeStruct((t, D_MODEL), F32),
        scratch_shapes=_weight_scratch([(D_MODEL, S_MAIN_COLS), (D_MODEL, W_PAD_COLS), (SG_WIDTH, W_PAD_COLS)])
                       + [pltpu.VMEM((CHUNK, S_COLS), F32), pltpu.VMEM((CHUNK, S_COLS), F32),
                        pltpu.VMEM((CHUNK, D_MODEL), BF16),
                        pltpu.VMEM((SG_GROUPS, CHUNK, CHUNK), BF16)],
        compiler_params=_cparams(),
        name="sg",
    )(x2, x2, x2, acc, g_pre.reshape(1, -1), w_in, w_gate, w_br, ln_g.reshape(1, -1), ln_b.reshape(1, -1),
      w_s, b_full)


def _ssd_branch(x2, acc, g_pre, w_in, w_gate, w_br, w_out, layer, g_post, cwx, cbx, cwb, cbb, dtb, alog, drow,
                ng, nb):
    t = x2.shape[0]
    n_blk = t // CHUNK
    return pl.pallas_call(
        functools.partial(_ssd_kernel, nb=nb, layer=layer),
        grid=(n_blk // 2,),
        in_specs=_x_specs(n_blk) + [_rows2(), _rows2(), _const((1, D_MODEL)), _any(), _any(), _any(), _any(),
                                    _const((1, D_MODEL)),
                                    _const((SSM_CONV, 1, SSM_WIDTH)), _const((1, SSM_WIDTH)),
                                    _const((SSM_CONV, 1, SSM_BC)), _const((1, SSM_BC)),
                                    _const((1, LANE)), _const((LANE, 1)),
                                    _const((1, SSM_WIDTH)), _const((1, SSM_WIDTH))],
        out_specs=_rows2(),
        out_shape=jax.ShapeDtypeStruct((t, D_MODEL), F32),
        scratch_shapes=_weight_scratch([(D_MODEL, M_G), (D_MODEL, W_PAD_COLS), (SSM_WIDTH, W_PAD_COLS),
                                        (D_MODEL, W_PAD_COLS)])
                       + [pltpu.VMEM((CHUNK, M_COLS), F32), pltpu.VMEM((CHUNK, M_COLS), F32),
                        pltpu.VMEM((CHUNK, D_MODEL), BF16),
                        pltpu.VMEM((SSM_WIDTH // LANE, 8 + CHUNK, LANE), F32),
                        pltpu.VMEM((SSM_BC // LANE, 8 + CHUNK, LANE), F32),
                        pltpu.VMEM((SSM_STATE, SSM_WIDTH), F32),
                        pltpu.VMEM((CHUNK, SSM_WIDTH), F32),
                        pltpu.VMEM((CHUNK, SSM_WIDTH), F32),
                        pltpu.VMEM((CHUNK, SSM_WIDTH), BF16),
                        pltpu.VMEM((CHUNK, D_MODEL), BF16),
                        pltpu.VMEM((CHUNK, D_MODEL), F32)],
        compiler_params=_cparams(),
        name="ssd",
    )(x2, x2, x2, x2, acc, g_pre.reshape(1, -1), w_in, w_gate, w_br, w_out, g_post.reshape(1, -1),
      cwx, cbx, cwb, cbb, dtb, alog, drow, ng)


def _rel_bucket_table():
    qi = jnp.arange(CHUNK, dtype=jnp.int32)[:, None]
    kj = jnp.arange(2 * CHUNK, dtype=jnp.int32)[None, :]
    dist = jnp.maximum(qi + CHUNK - kj, 0)
    max_exact = REL_BUCKETS // 2
    dist_f = jnp.maximum(dist, 1).astype(F32)
    large = max_exact + (jnp.log(dist_f / max_exact) / math.log(REL_MAX_DIST / max_exact)
                         * (REL_BUCKETS - max_exact)).astype(jnp.int32)
    large = jnp.minimum(large, REL_BUCKETS - 1)
    return jnp.where(dist < max_exact, dist, large)


def _pad_lanes(v):
    return jnp.pad(v, (0, LANE - v.shape[0])).reshape(1, LANE)


def kernel(x, w_in, norm_pre, norm_post, rel_bias, att_sinks, sg_ln_g, sg_ln_b, sg_w, sg_b, ssm_conv_w, ssm_conv_b, ssm_dt_bias, ssm_a_log, ssm_d, ssm_norm_g, w_br_att, w_br_sg, w_br_ssm, w_out):
    bsz, seq, d = x.shape
    depth = w_in.shape[0]
    assert d == D_MODEL and seq % (2 * CHUNK) == 0
    nb = seq // CHUNK
    x2 = x.reshape(bsz * seq, d)
    bucket = _rel_bucket_table()
    w_in = w_in.astype(BF16)
    w_gate = w_in[:, :, W_GATE_COL0:]
    for l in range(depth):
        acc = _att_branch(x2, norm_pre[l], w_in, w_gate, w_br_att, l, rel_bias, att_sinks[l], bucket, nb)
        b_full = jnp.repeat(jnp.transpose(sg_b[l]), CHUNK, axis=1)
        acc = _sg_branch(x2, acc, norm_pre[l], w_in, w_gate, w_br_sg, l, sg_ln_g[l], sg_ln_b[l], sg_w[l], b_full)
        cw, cb = ssm_conv_w[l], ssm_conv_b[l]
        x2 = _ssd_branch(x2, acc, norm_pre[l], w_in, w_gate, w_br_ssm, w_out, l, norm_post[l],
                         cw[:, None, :SSM_WIDTH], cb[:SSM_WIDTH].reshape(1, -1),
                         cw[:, None, SSM_WIDTH:], cb[SSM_WIDTH:].reshape(1, -1),
                         _pad_lanes(ssm_dt_bias[l]), _pad_lanes(ssm_a_log[l]).reshape(LANE, 1),
                         jnp.repeat(ssm_d[l], SSM_HEAD_DIM).reshape(1, -1),
                         ssm_norm_g[l].reshape(1, -1), nb)
    return x2.reshape(bsz, seq, d)
```

```python
import functools
import math

import jax
import jax.numpy as jnp
from jax import lax
from jax.experimental import pallas as pl
from jax.experimental.pallas import tpu as pltpu

F32 = jnp.float32
BF16 = jnp.bfloat16

D_MODEL = 1024
ATT_HEADS = 16
ATT_KV_HEADS = 2
ATT_HEAD_DIM = 64
ATT_GROUP = ATT_HEADS // ATT_KV_HEADS
CHUNK = 128
REL_BUCKETS = 32
REL_MAX_DIST = 128
SG_GROUPS = 8
SG_WIDTH = 1024
SSM_WIDTH = 2048
SSM_HEAD_DIM = 64
SSM_HEADS = 32
SSM_GROUPS = 4
SSM_STATE = 128
SSM_HPG = SSM_HEADS // SSM_GROUPS
SSM_CONV = 4
SSM_BC = 2 * SSM_GROUPS * SSM_STATE
EPS = 1e-6
NEG = -1e30
LOG2E = 1.4426950408889634
LANE = 128

A_Q, A_K, A_V, A_ZA, A_G, A_COLS = 0, 1024, 1152, 1280, 2304, 3328
S_U, S_VS, S_ZS, S_G, S_COLS = 0, 1024, 2048, 3072, 4096
M_ZM, M_XS, M_BC, M_DT, M_G, M_COLS = 0, 2048, 4096, 5120, 5248, 6272
VMEM_LIMIT = 56 * 1024 * 1024
W_PAD_COLS = D_MODEL + LANE
S_MAIN_COLS = S_G + LANE
W_SG_COL0, W_SSM_COL0, W_GATE_COL0 = 2304, 5376, 10528


def _cparams():
    return pltpu.CompilerParams(dimension_semantics=("arbitrary",), vmem_limit_bytes=VMEM_LIMIT)


def _sigmoid(x):
    return 0.5 * jnp.tanh(0.5 * x) + 0.5


def _silu(x):
    h = 0.5 * x
    return h * jnp.tanh(h) + h


def _softplus(x):
    return jnp.maximum(x, 0.0) + jnp.log1p(jnp.exp(-jnp.abs(x)))


def _dot(a, b):
    return jnp.dot(a, b, preferred_element_type=F32)


def _dot_nt(a, b):
    return lax.dot_general(a, b, (((1,), (1,)), ((), ())), preferred_element_type=F32)


def _rms_norm_to(x_ref, g_ref, h_ref):
    x = x_ref[...]
    ms = jnp.mean(x * x, axis=-1, keepdims=True)
    h_ref[...] = (x * lax.rsqrt(ms + EPS) * g_ref[...]).astype(BF16)


def _proj_cols(h_ref, w_ref, dst_ref, c0, c1, d0):
    dst_ref[:, d0:d0 + c1 - c0] = _dot(h_ref[...], w_ref[:, c0:c1])


def _proj_plan(parts, width=256):
    plan, d0 = [], 0
    for w_ref, n_cols in parts:
        for c0 in range(0, n_cols, width):
            c1 = min(c0 + width, n_cols)
            plan.append((w_ref, c0, c1, d0 + c0))
        d0 += n_cols
    return plan


def _proj_thunks(h_ref, dst_ref, plan):
    return [functools.partial(_proj_cols, h_ref, w_ref, dst_ref, c0, c1, d0) for w_ref, c0, c1, d0 in plan]


def _rms_proj(x_ref, g_ref, dst_ref, h_ref, plan):
    _rms_norm_to(x_ref, g_ref, h_ref)
    for thunk in _proj_thunks(h_ref, dst_ref, plan):
        thunk()


STAGE_ROWS, STAGE_COLS = 1024, 512


def _weight_jobs(src_hbm, layer, k_rows, col0, n_cols, dst_ref):
    jobs = []
    for r0 in range(0, k_rows, STAGE_ROWS):
        for c in range(0, n_cols, STAGE_COLS):
            w = min(STAGE_COLS, n_cols - c)
            jobs.append((src_hbm.at[layer, r0:r0 + STAGE_ROWS, col0 + c:col0 + c + w], dst_ref, r0, c, w))
    return jobs


def _weight_jobs_t(src_hbm, layer, row0, n_rows, dst_ref):
    jobs = []
    for c in range(0, n_rows, STAGE_COLS):
        w = min(STAGE_COLS, n_rows - c)
        jobs.append((src_hbm.at[layer, row0 + c:row0 + c + w, :], dst_ref, None, c, w))
    return jobs


def _load_weights(jobs, stage, stage_t, sem):
    def dma(k):
        src, _, r0, _, w = jobs[k]
        dst = stage_t.at[k % 2, 0:w, :] if r0 is None else stage.at[k % 2, :, 0:w]
        return pltpu.make_async_copy(src, dst, sem.at[k % 2])

    dma(0).start()
    for k, (_, dst, r0, d0, w) in enumerate(jobs):
        if k + 1 < len(jobs):
            dma(k + 1).start()
        dma(k).wait()
        if r0 is None:
            dst[:, d0:d0 + w] = stage_t[k % 2, 0:w, :].T.astype(BF16)
        else:
            dst[r0:r0 + STAGE_ROWS, d0:d0 + w] = stage[k % 2, :, 0:w].astype(BF16)


class _Spread:
    def __init__(self, thunks, n_ticks):
        self.thunks, self.n_ticks, self.ticks, self.done = thunks, n_ticks, 0, 0

    def tick(self):
        self.ticks += 1
        while self.done < len(self.thunks) and self.done * self.n_ticks < self.ticks * len(self.thunks):
            self.thunks[self.done]()
            self.done += 1

    def flush(self):
        self.ticks = self.n_ticks
        self.tick()


def _col_chunks(n_cols, n_chunks, width=256):
    tiles = -(-n_cols // width)
    per = [tiles // n_chunks + (1 if k < tiles % n_chunks else 0) for k in range(n_chunks)]
    out, c = [], 0
    for n_tiles in per:
        out.append((c, min(c + n_tiles * width, n_cols)))
        c = min(c + n_tiles * width, n_cols)
    return out


def _split3(a):
    hi = a.astype(BF16)
    r1 = a - hi.astype(F32)
    mid = r1.astype(BF16)
    lo = (r1 - mid.astype(F32)).astype(BF16)
    return hi, mid, lo


def _att_block(src, kv_prev, bias_ref, sink_ref, variant, tick=lambda: None):
    qi = lax.broadcasted_iota(jnp.int32, (CHUNK, CHUNK), 0)
    ci = lax.broadcasted_iota(jnp.int32, (CHUNK, CHUNK), 1)
    own = ci <= qi
    low = lax.broadcasted_iota(jnp.int32, (2 * CHUNK, LANE), 1) < ATT_HEAD_DIM
    low_q = lax.broadcasted_iota(jnp.int32, (CHUNK, LANE), 1) < ATT_HEAD_DIM
    scale = ATT_HEAD_DIM ** -0.5
    kcat = jnp.concatenate([src[:, A_K:A_K + LANE], kv_prev[:, 0:LANE]], axis=0)
    vcat = jnp.concatenate([src[:, A_V:A_V + LANE], kv_prev[:, LANE:2 * LANE]], axis=0)
    k_roll = pltpu.roll(kcat, ATT_HEAD_DIM, 1)
    v_roll = pltpu.roll(vcat, ATT_HEAD_DIM, 1)
    n_pair = ATT_GROUP // 2
    pairs = []
    for kvh in range(ATT_KV_HEADS):
        k_lo, k_hi = (kcat, k_roll) if kvh == 0 else (k_roll, kcat)
        v_lo, v_hi = (vcat, v_roll) if kvh == 0 else (v_roll, vcat)
        kz = (jnp.where(low, k_lo, 0.0).astype(BF16), jnp.where(low, 0.0, k_hi).astype(BF16))
        vz = (jnp.where(low, v_lo, 0.0).astype(BF16), jnp.where(low, 0.0, v_hi).astype(BF16))
        pr0 = kvh * n_pair
        q4 = jnp.concatenate([src[:, A_Q + pr * LANE:A_Q + (pr + 1) * LANE]
                              for pr in range(pr0, pr0 + n_pair)], axis=0)
        q4 = (q4 * scale).astype(BF16)
        lhs_cols = []
        recips = [[None, None] for _ in range(n_pair)]
        for par in range(2):
            tick()
            s2 = _dot_nt(q4, kz[par])
            rows = []
            for k in range(n_pair):
                h = 2 * (pr0 + k) + par
                s2k = s2[k * CHUNK:(k + 1) * CHUNK]
                s = jnp.where(own, s2k[:, 0:CHUNK], s2k[:, CHUNK:2 * CHUNK]) + bias_ref[variant, h]
                sink = sink_ref[h]
                m = jnp.maximum(jnp.max(s, axis=-1, keepdims=True), sink)
                p = jnp.exp(s - m)
                den = jnp.sum(p, axis=-1, keepdims=True) + jnp.exp(sink - m)
                recips[k][par] = 1.0 / den
                rows.append(jnp.concatenate([jnp.where(own, p, 0.0), jnp.where(own, 0.0, p)],
                                            axis=1).astype(BF16))
            lhs_cols.append(jnp.concatenate(rows, axis=0))
        tick()
        o4 = _dot(jnp.concatenate(lhs_cols, axis=1), jnp.concatenate(vz, axis=0))
        for k in range(n_pair):
            r_pair = jnp.where(low_q, jnp.broadcast_to(recips[k][0], (CHUNK, LANE)),
                               jnp.broadcast_to(recips[k][1], (CHUNK, LANE)))
            pairs.append(o4[k * CHUNK:(k + 1) * CHUNK] * r_pair)
    y = jnp.concatenate(pairs, axis=-1) * _silu(src[:, A_ZA:A_ZA + 1024])
    tick()
    tick()
    return y


def _att_kernel(rb_ref, sink_ref, bkt_ref, x0_ref, xa_ref, xb_ref, gpre_ref, win_hbm, wbr_hbm,
                o_ref, w_ref, wg_ref, wbr_ref, stage, stage_t, sem, slot_a, slot_b, h_ref, kv_prev, bias_ref,
                *, nb, layer):
    i = pl.program_id(0)
    plan = _proj_plan([(w_ref, A_G), (wg_ref, D_MODEL)])
    n_ticks = 2 * (ATT_KV_HEADS + 1) + 2

    @pl.when(i == 0)
    def _():
        qi = lax.broadcasted_iota(jnp.int32, (CHUNK, CHUNK), 0)
        ci = lax.broadcasted_iota(jnp.int32, (CHUNK, CHUNK), 1)
        own = ci <= qi
        bkt = jnp.where(own, bkt_ref[:, CHUNK:2 * CHUNK], bkt_ref[:, 0:CHUNK])
        for h in range(ATT_HEADS):
            acc = jnp.zeros((CHUNK, CHUNK), F32)
            for b in range(REL_BUCKETS):
                acc = jnp.where(bkt == b, rb_ref[b, h], acc)
            bias_ref[1, h] = acc
            bias_ref[0, h] = jnp.where(own, acc, NEG)
        kv_prev[...] = jnp.zeros_like(kv_prev)
        _load_weights(_weight_jobs_t(win_hbm, layer, 0, A_G, w_ref)
                      + _weight_jobs_t(win_hbm, layer, W_GATE_COL0, D_MODEL, wg_ref)
                      + _weight_jobs(wbr_hbm, layer, 1024, 0, D_MODEL, wbr_ref), stage, stage_t, sem)
        _rms_proj(x0_ref, gpre_ref, slot_a, h_ref, plan)

    def mix(src, variant, rows, x_next_ref, dst):
        _rms_norm_to(x_next_ref, gpre_ref, h_ref)
        spread = _Spread(_proj_thunks(h_ref, dst, plan), n_ticks)
        y = _att_block(src, kv_prev, bias_ref, sink_ref, variant, tick=spread.tick)
        spread.flush()
        kv_prev[...] = src[:, A_K:A_K + 2 * LANE]
        o_ref[rows, :] = _sigmoid(src[:, A_G:A_G + D_MODEL]) * _dot(y.astype(BF16), wbr_ref[:, 0:D_MODEL])

    first = ((2 * i) % nb) == 0
    mix(slot_a, jnp.where(first, 0, 1), slice(0, CHUNK), xa_ref, slot_b)
    mix(slot_b, 1, slice(CHUNK, 2 * CHUNK), xb_ref, slot_a)


def _sg_block(src, lng_ref, lnb_ref, wt_ref, bfull_ref, tick=lambda: None):
    v = src[:, S_VS:S_VS + SG_WIDTH]
    mu = jnp.mean(v, axis=-1, keepdims=True)
    vc = v - mu
    var = jnp.mean(vc * vc, axis=-1, keepdims=True)
    vn = (vc * lax.rsqrt(var + EPS) * lng_ref[...] + lnb_ref[...]).astype(BF16)
    out = []
    for g in range(SG_GROUPS):
        tick()
        sl = slice(g * LANE, (g + 1) * LANE)
        mixed = _dot(wt_ref[g], vn[:, sl]) + bfull_ref[:, sl]
        out.append(src[:, S_U + g * LANE:S_U + (g + 1) * LANE] * mixed
                   * _silu(src[:, S_ZS + g * LANE:S_ZS + (g + 1) * LANE]))
    return jnp.concatenate(out, axis=-1)


def _sg_kernel(x0_ref, xa_ref, xb_ref, acc_ref, gpre_ref, win_hbm, wbr_hbm,
               lng_ref, lnb_ref, ws_ref, bfull_ref, o_ref, w_ref, wg_ref, wbr_ref, stage, stage_t, sem,
               slot_a, slot_b, h_ref, wt_ref, *, layer):
    i = pl.program_id(0)
    plan = _proj_plan([(w_ref, S_G), (wg_ref, D_MODEL)])

    @pl.when(i == 0)
    def _():
        ti = lax.broadcasted_iota(jnp.int32, (CHUNK, CHUNK), 0)
        si = lax.broadcasted_iota(jnp.int32, (CHUNK, CHUNK), 1)
        for g in range(SG_GROUPS):
            wt_ref[g] = jnp.where(si <= ti, ws_ref[g], 0.0).astype(BF16)
        _load_weights(_weight_jobs_t(win_hbm, layer, W_SG_COL0, S_G, w_ref)
                      + _weight_jobs_t(win_hbm, layer, W_GATE_COL0 + D_MODEL, D_MODEL, wg_ref)
                      + _weight_jobs(wbr_hbm, layer, SG_WIDTH, 0, D_MODEL, wbr_ref), stage, stage_t, sem)
        _rms_proj(x0_ref, gpre_ref, slot_a, h_ref, plan)

    def mix(src, rows, x_next_ref, dst):
        _rms_norm_to(x_next_ref, gpre_ref, h_ref)
        spread = _Spread(_proj_thunks(h_ref, dst, plan), SG_GROUPS)
        y = _sg_block(src, lng_ref, lnb_ref, wt_ref, bfull_ref, tick=spread.tick)
        spread.flush()
        o_ref[rows, :] = (acc_ref[rows, :]
                          + _sigmoid(src[:, S_G:S_G + D_MODEL]) * _dot(y.astype(BF16), wbr_ref[:, 0:D_MODEL]))

    mix(slot_a, slice(0, CHUNK), xa_ref, slot_b)
    mix(slot_b, slice(CHUNK, 2 * CHUNK), xb_ref, slot_a)


def _ssd_block(src, first, cwx_ref, cbx_ref, cwb_ref, cbb_ref, dtb_ref, alog_ref, drow_ref, ng_ref,
               ext_x, ext_b, h_ref, xc_ref, y_ref, tick=lambda: None):
    n_xt, n_bt = SSM_WIDTH // LANE, SSM_BC // LANE
    tail = slice(CHUNK, CHUNK + 8)

    if first is not None:
        @pl.when(first)
        def _():
            h_ref[...] = jnp.zeros_like(h_ref)

    def stage(ext, col0, n_tiles):
        for t in range(n_tiles):
            prev = ext[t, tail, :]
            ext[t, 0:8, :] = prev if first is None else jnp.where(first, 0.0, prev)
            ext[t, 8:8 + CHUNK, :] = src[:, col0 + t * LANE:col0 + (t + 1) * LANE]

    def conv_tile(ext, w_ref, b_ref, t):
        sl = slice(t * LANE, (t + 1) * LANE)
        acc = b_ref[:, sl]
        for k in range(SSM_CONV):
            acc = acc + w_ref[k, :, sl] * ext[t, pl.ds(8 - (SSM_CONV - 1) + k, CHUNK), :]
        return _silu(acc)

    stage(ext_x, M_XS, n_xt)
    stage(ext_b, M_BC, n_bt)
    for t in range(n_xt):
        tick()
        xc_ref[:, t * LANE:(t + 1) * LANE] = conv_tile(ext_x, cwx_ref, cbx_ref, t)
    bcv = []
    for t in range(n_bt):
        tick()
        bcv.append(conv_tile(ext_b, cwb_ref, cbb_ref, t))

    li = lax.broadcasted_iota(jnp.int32, (CHUNK, CHUNK), 0)
    si = lax.broadcasted_iota(jnp.int32, (CHUNK, CHUNK), 1)
    causal = si <= li
    x_dt_t = (src[:, M_DT:M_DT + LANE] + dtb_ref[...]).T[0:SSM_HEADS, :]
    dt_t = _softplus(x_dt_t)
    a_dt_t = dt_t * (-jnp.exp(alog_ref[0:SSM_HEADS, :]))
    upper = jnp.where(li <= si, 1.0, 0.0).astype(BF16)
    parts = _dot(jnp.concatenate(_split3(a_dt_t), axis=0), upper)
    a_cs_t = (parts[0:SSM_HEADS] + parts[SSM_HEADS:2 * SSM_HEADS]
              + parts[2 * SSM_HEADS:3 * SSM_HEADS])
    a2_t = a_cs_t * LOG2E
    a2_last = a2_t[:, CHUNK - 1:CHUNK]
    w_t = dt_t * jnp.exp2(a2_last - a2_t)
    cd_t = jnp.exp2(a2_last)
    a2 = jnp.concatenate([a2_t, jnp.zeros((CHUNK - SSM_HEADS, CHUNK), F32)], axis=0).T
    e2 = jnp.exp2(a2)
    low = lax.broadcasted_iota(jnp.int32, (CHUNK, LANE), 1) < SSM_HEAD_DIM
    low_row = low[0:1]

    for g in range(SSM_GROUPS):
        b_g, c_g = bcv[g], bcv[SSM_GROUPS + g]
        cb = jnp.where(causal, _dot_nt(c_g.astype(BF16), b_g.astype(BF16)), 0.0)
        b_gt = b_g.T
        for pr in range(g * SSM_HPG // 2, (g + 1) * SSM_HPG // 2):
            tick()
            sl = slice(pr * LANE, (pr + 1) * LANE)
            x_pair = xc_ref[:, sl].astype(BF16)
            h_pair = h_ref[:, sl]
            rhs = jnp.concatenate([x_pair, h_pair.astype(BF16)], axis=0)
            lhs_rows, bw_rows = [], []
            for j in (2 * pr, 2 * pr + 1):
                a_col = jnp.broadcast_to(a2[:, j:j + 1], (CHUNK, CHUNK))
                decay = jnp.exp2(jnp.minimum(a_col - a2_t[j:j + 1, :], 0.0))
                m_j = cb * decay * dt_t[j:j + 1, :]
                e_col = jnp.broadcast_to(e2[:, j:j + 1], (CHUNK, CHUNK))
                lhs_rows.append(jnp.concatenate([m_j, c_g * e_col], axis=1).astype(BF16))
                bw_rows.append((b_gt * w_t[j:j + 1, :]).astype(BF16))
            yy = _dot(jnp.concatenate(lhs_rows, axis=0), rhs)
            up = _dot(jnp.concatenate(bw_rows, axis=0), x_pair)
            y_ref[:, sl] = jnp.where(low, yy[0:CHUNK], yy[CHUNK:2 * CHUNK])
            cd_pair = jnp.where(low_row, cd_t[2 * pr:2 * pr + 1, :], cd_t[2 * pr + 1:2 * pr + 2, :])
            h_ref[:, sl] = h_pair * cd_pair + jnp.where(low, up[0:SSM_STATE], up[SSM_STATE:2 * SSM_STATE])

    gw = SSM_WIDTH // SSM_GROUPS
    out = []
    for g in range(SSM_GROUPS):
        tick()
        sl = slice(g * gw, (g + 1) * gw)
        yg = (y_ref[:, sl] + drow_ref[:, sl] * xc_ref[:, sl]) * _silu(src[:, M_ZM + g * gw:M_ZM + (g + 1) * gw])
        yg = yg * lax.rsqrt(jnp.mean(yg * yg, axis=-1, keepdims=True) + EPS)
        out.append(yg * ng_ref[:, sl])
    return jnp.concatenate(out, axis=-1)


def _ssd_kernel(x0_ref, xa_ref, xb_ref, xres_ref, acc_ref, gpre_ref, win_hbm, wbr_hbm, wout_hbm,
                gpost_ref, cwx_ref, cbx_ref, cwb_ref, cbb_ref, dtb_ref, alog_ref, drow_ref, ng_ref,
                o_ref, w_ref, wg_ref, wbr_ref, wout_ref, stage, stage_t, sem, slot_a, slot_b, hn_ref, ext_x, ext_b,
                h_ref, xc_ref, y_ref, ybf_ref, mg_ref, out_ref, *, nb, layer):
    i = pl.program_id(0)
    plan = _proj_plan([(w_ref, M_G), (wg_ref, D_MODEL)])

    @pl.when(i == 0)
    def _():
        ext_x[...] = jnp.zeros_like(ext_x)
        ext_b[...] = jnp.zeros_like(ext_b)
        _load_weights(_weight_jobs_t(win_hbm, layer, W_SSM_COL0, M_G, w_ref)
                      + _weight_jobs_t(win_hbm, layer, W_GATE_COL0 + 2 * D_MODEL, D_MODEL, wg_ref)
                      + _weight_jobs(wbr_hbm, layer, SSM_WIDTH, 0, D_MODEL, wbr_ref)
                      + _weight_jobs(wout_hbm, layer, D_MODEL, 0, D_MODEL, wout_ref), stage, stage_t, sem)
        _rms_proj(x0_ref, gpre_ref, slot_a, hn_ref, plan)

    n_ticks = SSM_WIDTH // LANE + SSM_BC // LANE + SSM_HEADS // 2 + SSM_GROUPS
    out_cols = _col_chunks(D_MODEL, D_MODEL // 256)

    def epilogue(src, rows):
        def branch(c0, c1):
            gate = _sigmoid(src[:, M_G + c0:M_G + c1])
            mg_ref[:, c0:c1] = (acc_ref[rows, c0:c1]
                                + gate * _dot(ybf_ref[...], wbr_ref[:, c0:c1])).astype(BF16)

        def outproj(c0, c1):
            out_ref[:, c0:c1] = _dot(mg_ref[...], wout_ref[:, c0:c1])

        def finish():
            out = out_ref[...]
            ms = jnp.mean(out * out, axis=-1, keepdims=True)
            o_ref[rows, :] = xres_ref[rows, :] + out * lax.rsqrt(ms + EPS) * gpost_ref[...]

        return ([functools.partial(branch, c0, c1) for c0, c1 in out_cols]
                + [functools.partial(outproj, c0, c1) for c0, c1 in out_cols] + [finish])

    def mix(src, first, x_next_ref, dst, pending):
        _rms_norm_to(x_next_ref, gpre_ref, hn_ref)
        spread = _Spread(pending + _proj_thunks(hn_ref, dst, plan), n_ticks)
        y = _ssd_block(src, first, cwx_ref, cbx_ref, cwb_ref, cbb_ref, dtb_ref, alog_ref, drow_ref, ng_ref,
                       ext_x, ext_b, h_ref, xc_ref, y_ref, tick=spread.tick)
        spread.flush()
        ybf_ref[...] = y.astype(BF16)

    first = ((2 * i) % nb) == 0
    mix(slot_a, first, xa_ref, slot_b, [])
    mix(slot_b, None, xb_ref, slot_a, epilogue(slot_a, slice(0, CHUNK)))
    for thunk in epilogue(slot_b, slice(CHUNK, 2 * CHUNK)):
        thunk()


def _x_specs(n_blk):
    return [pl.BlockSpec((CHUNK, D_MODEL), lambda i: (0, 0)),
            pl.BlockSpec((CHUNK, D_MODEL), lambda i: (2 * i + 1, 0)),
            pl.BlockSpec((CHUNK, D_MODEL), lambda i: (jnp.minimum(2 * i + 2, n_blk - 1), 0))]


def _const(shape):
    return pl.BlockSpec(shape, lambda i: (0,) * len(shape))


def _rows2():
    return pl.BlockSpec((2 * CHUNK, D_MODEL), lambda i: (i, 0))


def _weight_scratch(shapes):
    return ([pltpu.VMEM(s, BF16) for s in shapes]
            + [pltpu.VMEM((2, STAGE_ROWS, STAGE_COLS), F32), pltpu.VMEM((2, STAGE_COLS, STAGE_ROWS), F32),
               pltpu.SemaphoreType.DMA((2,))])


def _any():
    return pl.BlockSpec(memory_space=pl.ANY)


def _att_branch(x2, g_pre, w_in_t, w_br, layer, rel_bias, sinks, bucket, nb):
    t = x2.shape[0]
    n_blk = t // CHUNK
    smem = pl.BlockSpec(memory_space=pltpu.SMEM)
    return pl.pallas_call(
        functools.partial(_att_kernel, nb=nb, layer=layer),
        grid=(n_blk // 2,),
        in_specs=[smem, smem, _const((CHUNK, 2 * CHUNK))] + _x_specs(n_blk)
                 + [_const((1, D_MODEL)), _any(), _any()],
        out_specs=_rows2(),
        out_shape=jax.ShapeDtypeStruct((t, D_MODEL), F32),
        scratch_shapes=_weight_scratch([(D_MODEL, A_G), (D_MODEL, W_PAD_COLS), (1024, W_PAD_COLS)])
                       + [pltpu.VMEM((CHUNK, A_COLS), F32), pltpu.VMEM((CHUNK, A_COLS), F32),
                        pltpu.VMEM((CHUNK, D_MODEL), BF16),
                        pltpu.VMEM((CHUNK, 2 * LANE), F32),
                        pltpu.VMEM((2, ATT_HEADS, CHUNK, CHUNK), F32)],
        compiler_params=_cparams(),
        name="att",
    )(rel_bias, sinks, bucket, x2, x2, x2, g_pre.reshape(1, -1), w_in_t, w_br)


def _sg_branch(x2, acc, g_pre, w_in_t, w_br, layer, ln_g, ln_b, w_s, b_full):
    t = x2.shape[0]
    n_blk = t // CHUNK
    return pl.pallas_call(
        functools.partial(_sg_kernel, layer=layer),
        grid=(n_blk // 2,),
        in_specs=_x_specs(n_blk) + [_rows2(), _const((1, D_MODEL)), _any(), _any(),
                                    _const((1, SG_WIDTH)), _const((1, SG_WIDTH)),
                                    _const((SG_GROUPS, CHUNK, CHUNK)), _const((CHUNK, SG_WIDTH))],
        out_specs=_rows2(),
        out_shape=jax.ShapeDtypeStruct((t, D_MODEL), F32),
        scratch_shapes=_weight_scratch([(D_MODEL, S_MAIN_COLS), (D_MODEL, W_PAD_COLS), (SG_WIDTH, W_PAD_COLS)])
                       + [pltpu.VMEM((CHUNK, S_COLS), F32), pltpu.VMEM((CHUNK, S_COLS), F32),
                        pltpu.VMEM((CHUNK, D_MODEL), BF16),
                        pltpu.VMEM((SG_GROUPS, CHUNK, CHUNK), BF16)],
        compiler_params=_cparams(),
        name="sg",
    )(x2, x2, x2, acc, g_pre.reshape(1, -1), w_in_t, w_br, ln_g.reshape(1, -1), ln_b.reshape(1, -1),
      w_s, b_full)


def _ssd_branch(x2, acc, g_pre, w_in_t, w_br, w_out, layer, g_post, cwx, cbx, cwb, cbb, dtb, alog, drow,
                ng, nb):
    t = x2.shape[0]
    n_blk = t // CHUNK
    return pl.pallas_call(
        functools.partial(_ssd_kernel, nb=nb, layer=layer),
        grid=(n_blk // 2,),
        in_specs=_x_specs(n_blk) + [_rows2(), _rows2(), _const((1, D_MODEL)), _any(), _any(), _any(),
                                    _const((1, D_MODEL)),
                                    _const((SSM_CONV, 1, SSM_WIDTH)), _const((1, SSM_WIDTH)),
                                    _const((SSM_CONV, 1, SSM_BC)), _const((1, SSM_BC)),
                                    _const((1, LANE)), _const((LANE, 1)),
                                    _const((1, SSM_WIDTH)), _const((1, SSM_WIDTH))],
        out_specs=_rows2(),
        out_shape=jax.ShapeDtypeStruct((t, D_MODEL), F32),
        scratch_shapes=_weight_scratch([(D_MODEL, M_G), (D_MODEL, W_PAD_COLS), (SSM_WIDTH, W_PAD_COLS),
                                        (D_MODEL, W_PAD_COLS)])
                       + [pltpu.VMEM((CHUNK, M_COLS), F32), pltpu.VMEM((CHUNK, M_COLS), F32),
                        pltpu.VMEM((CHUNK, D_MODEL), BF16),
                        pltpu.VMEM((SSM_WIDTH // LANE, 8 + CHUNK, LANE), F32),
                        pltpu.VMEM((SSM_BC // LANE, 8 + CHUNK, LANE), F32),
                        pltpu.VMEM((SSM_STATE, SSM_WIDTH), F32),
                        pltpu.VMEM((CHUNK, SSM_WIDTH), F32),
                        pltpu.VMEM((CHUNK, SSM_WIDTH), F32),
                        pltpu.VMEM((CHUNK, SSM_WIDTH), BF16),
                        pltpu.VMEM((CHUNK, D_MODEL), BF16),
                        pltpu.VMEM((CHUNK, D_MODEL), F32)],
        compiler_params=_cparams(),
        name="ssd",
    )(x2, x2, x2, x2, acc, g_pre.reshape(1, -1), w_in_t, w_br, w_out, g_post.reshape(1, -1),
      cwx, cbx, cwb, cbb, dtb, alog, drow, ng)


def _rel_bucket_table():
    qi = jnp.arange(CHUNK, dtype=jnp.int32)[:, None]
    kj = jnp.arange(2 * CHUNK, dtype=jnp.int32)[None, :]
    dist = jnp.maximum(qi + CHUNK - kj, 0)
    max_exact = REL_BUCKETS // 2
    dist_f = jnp.maximum(dist, 1).astype(F32)
    large = max_exact + (jnp.log(dist_f / max_exact) / math.log(REL_MAX_DIST / max_exact)
                         * (REL_BUCKETS - max_exact)).astype(jnp.int32)
    large = jnp.minimum(large, REL_BUCKETS - 1)
    return jnp.where(dist < max_exact, dist, large)


def _pad_lanes(v):
    return jnp.pad(v, (0, LANE - v.shape[0])).reshape(1, LANE)


def kernel(x, w_in, norm_pre, norm_post, rel_bias, att_sinks, sg_ln_g, sg_ln_b, sg_w, sg_b, ssm_conv_w, ssm_conv_b, ssm_dt_bias, ssm_a_log, ssm_d, ssm_norm_g, w_br_att, w_br_sg, w_br_ssm, w_out):
    bsz, seq, d = x.shape
    depth = w_in.shape[0]
    assert d == D_MODEL and seq % (2 * CHUNK) == 0
    nb = seq // CHUNK
    x2 = x.reshape(bsz * seq, d)
    bucket = _rel_bucket_table()
    w_in_t = jnp.swapaxes(w_in, 1, 2)
    for l in range(depth):
        acc = _att_branch(x2, norm_pre[l], w_in_t, w_br_att, l, rel_bias, att_sinks[l], bucket, nb)
        b_full = jnp.repeat(jnp.transpose(sg_b[l]), CHUNK, axis=1)
        acc = _sg_branch(x2, acc, norm_pre[l], w_in_t, w_br_sg, l, sg_ln_g[l], sg_ln_b[l], sg_w[l], b_full)
        cw, cb = ssm_conv_w[l], ssm_conv_b[l]
        x2 = _ssd_branch(x2, acc, norm_pre[l], w_in_t, w_br_ssm, w_out, l, norm_post[l],
                         cw[:, None, :SSM_WIDTH], cb[:SSM_WIDTH].reshape(1, -1),
                         cw[:, None, SSM_WIDTH:], cb[SSM_WIDTH:].reshape(1, -1),
                         _pad_lanes(ssm_dt_bias[l]), _pad_lanes(ssm_a_log[l]).reshape(LANE, 1),
                         jnp.repeat(ssm_d[l], SSM_HEAD_DIM).reshape(1, -1),
                         ssm_norm_g[l].reshape(1, -1), nb)
    return x2.reshape(bsz, seq, d)
```

```python
import functools
import math

import jax
import jax.numpy as jnp
from jax import lax
from jax.experimental import pallas as pl
from jax.experimental.pallas import tpu as pltpu

F32 = jnp.float32
BF16 = jnp.bfloat16

D_MODEL = 1024
ATT_HEADS = 16
ATT_KV_HEADS = 2
ATT_HEAD_DIM = 64
ATT_GROUP = ATT_HEADS // ATT_KV_HEADS
CHUNK = 128
REL_BUCKETS = 32
REL_MAX_DIST = 128
SG_GROUPS = 8
SG_WIDTH = 1024
SSM_WIDTH = 2048
SSM_HEAD_DIM = 64
SSM_HEADS = 32
SSM_GROUPS = 4
SSM_STATE = 128
SSM_HPG = SSM_HEADS // SSM_GROUPS
SSM_CONV = 4
SSM_BC = 2 * SSM_GROUPS * SSM_STATE
EPS = 1e-6
NEG = -1e30
LOG2E = 1.4426950408889634
LANE = 128

A_Q, A_K, A_V, A_ZA, A_G, A_COLS = 0, 1024, 1152, 1280, 2304, 3328
S_U, S_VS, S_ZS, S_G, S_COLS = 0, 1024, 2048, 3072, 4096
M_ZM, M_XS, M_BC, M_DT, M_G, M_COLS = 0, 2048, 4096, 5120, 5248, 6272
VMEM_LIMIT = 56 * 1024 * 1024
W_PAD_COLS = D_MODEL + LANE
S_MAIN_COLS = S_G + LANE
W_SG_COL0, W_SSM_COL0, W_GATE_COL0 = 2304, 5376, 10528


def _cparams():
    return pltpu.CompilerParams(dimension_semantics=("arbitrary",), vmem_limit_bytes=VMEM_LIMIT)


def _sigmoid(x):
    return 0.5 * jnp.tanh(0.5 * x) + 0.5


def _silu(x):
    h = 0.5 * x
    return h * jnp.tanh(h) + h


def _softplus(x):
    return jnp.maximum(x, 0.0) + jnp.log1p(jnp.exp(-jnp.abs(x)))


def _dot(a, b):
    return jnp.dot(a, b, preferred_element_type=F32)


def _dot_nt(a, b):
    return lax.dot_general(a, b, (((1,), (1,)), ((), ())), preferred_element_type=F32)


def _rms_norm_to(x_ref, g_ref, h_ref):
    x = x_ref[...]
    ms = jnp.mean(x * x, axis=-1, keepdims=True)
    h_ref[...] = (x * lax.rsqrt(ms + EPS) * g_ref[...]).astype(BF16)


def _proj_cols(h_ref, w_ref, dst_ref, c0, c1, d0):
    dst_ref[:, d0:d0 + c1 - c0] = _dot(h_ref[...], w_ref[:, c0:c1])


def _proj_plan(parts, width=256):
    plan, d0 = [], 0
    for w_ref, n_cols in parts:
        for c0 in range(0, n_cols, width):
            c1 = min(c0 + width, n_cols)
            plan.append((w_ref, c0, c1, d0 + c0))
        d0 += n_cols
    return plan


def _proj_thunks(h_ref, dst_ref, plan):
    return [functools.partial(_proj_cols, h_ref, w_ref, dst_ref, c0, c1, d0) for w_ref, c0, c1, d0 in plan]


def _rms_proj(x_ref, g_ref, dst_ref, h_ref, plan):
    _rms_norm_to(x_ref, g_ref, h_ref)
    for thunk in _proj_thunks(h_ref, dst_ref, plan):
        thunk()


STAGE_ROWS, STAGE_COLS = 1024, 512


def _weight_jobs(src_hbm, layer, k_rows, col0, n_cols, dst_ref):
    jobs = []
    for r0 in range(0, k_rows, STAGE_ROWS):
        for c in range(0, n_cols, STAGE_COLS):
            w = min(STAGE_COLS, n_cols - c)
            jobs.append((src_hbm.at[layer, r0:r0 + STAGE_ROWS, col0 + c:col0 + c + w], dst_ref, r0, c, w))
    return jobs


def _weight_jobs_t(src_hbm, layer, row0, n_rows, dst_ref):
    jobs = []
    for c in range(0, n_rows, STAGE_COLS):
        w = min(STAGE_COLS, n_rows - c)
        jobs.append((src_hbm.at[layer, row0 + c:row0 + c + w, :], dst_ref, None, c, w))
    return jobs


def _load_weights(jobs, stage, stage_t, sem):
    def dma(k):
        src, _, r0, _, w = jobs[k]
        dst = stage_t.at[k % 2, 0:w, :] if r0 is None else stage.at[k % 2, :, 0:w]
        return pltpu.make_async_copy(src, dst, sem.at[k % 2])

    dma(0).start()
    for k, (_, dst, r0, d0, w) in enumerate(jobs):
        if k + 1 < len(jobs):
            dma(k + 1).start()
        dma(k).wait()
        if r0 is None:
            dst[:, d0:d0 + w] = stage_t[k % 2, 0:w, :].T.astype(BF16)
        else:
            dst[r0:r0 + STAGE_ROWS, d0:d0 + w] = stage[k % 2, :, 0:w].astype(BF16)


class _Spread:
    def __init__(self, thunks, n_ticks):
        self.thunks, self.n_ticks, self.ticks, self.done = thunks, n_ticks, 0, 0

    def tick(self):
        self.ticks += 1
        while self.done < len(self.thunks) and self.done * self.n_ticks < self.ticks * len(self.thunks):
            self.thunks[self.done]()
            self.done += 1

    def flush(self):
        self.ticks = self.n_ticks
        self.tick()


def _col_chunks(n_cols, n_chunks, width=256):
    tiles = -(-n_cols // width)
    per = [tiles // n_chunks + (1 if k < tiles % n_chunks else 0) for k in range(n_chunks)]
    out, c = [], 0
    for n_tiles in per:
        out.append((c, min(c + n_tiles * width, n_cols)))
        c = min(c + n_tiles * width, n_cols)
    return out


def _split3(a):
    hi = a.astype(BF16)
    r1 = a - hi.astype(F32)
    mid = r1.astype(BF16)
    lo = (r1 - mid.astype(F32)).astype(BF16)
    return hi, mid, lo


def _att_block(src, kv_prev, bias_ref, sink_ref, variant, tick=lambda: None):
    qi = lax.broadcasted_iota(jnp.int32, (CHUNK, CHUNK), 0)
    ci = lax.broadcasted_iota(jnp.int32, (CHUNK, CHUNK), 1)
    own = ci <= qi
    low = lax.broadcasted_iota(jnp.int32, (2 * CHUNK, LANE), 1) < ATT_HEAD_DIM
    low_q = lax.broadcasted_iota(jnp.int32, (CHUNK, LANE), 1) < ATT_HEAD_DIM
    scale = ATT_HEAD_DIM ** -0.5
    kcat = jnp.concatenate([src[:, A_K:A_K + LANE], kv_prev[:, 0:LANE]], axis=0)
    vcat = jnp.concatenate([src[:, A_V:A_V + LANE], kv_prev[:, LANE:2 * LANE]], axis=0)
    k_roll = pltpu.roll(kcat, ATT_HEAD_DIM, 1)
    v_roll = pltpu.roll(vcat, ATT_HEAD_DIM, 1)
    n_pair = ATT_GROUP // 2
    pairs = []
    for kvh in range(ATT_KV_HEADS):
        k_lo, k_hi = (kcat, k_roll) if kvh == 0 else (k_roll, kcat)
        v_lo, v_hi = (vcat, v_roll) if kvh == 0 else (v_roll, vcat)
        kz = (jnp.where(low, k_lo, 0.0).astype(BF16), jnp.where(low, 0.0, k_hi).astype(BF16))
        vz = (jnp.where(low, v_lo, 0.0).astype(BF16), jnp.where(low, 0.0, v_hi).astype(BF16))
        pr0 = kvh * n_pair
        q4 = jnp.concatenate([src[:, A_Q + pr * LANE:A_Q + (pr + 1) * LANE]
                              for pr in range(pr0, pr0 + n_pair)], axis=0)
        q4 = (q4 * scale).astype(BF16)
        lhs_cols = []
        recips = [[None, None] for _ in range(n_pair)]
        for par in range(2):
            tick()
            s2 = _dot_nt(q4, kz[par])
            rows = []
            for k in range(n_pair):
                h = 2 * (pr0 + k) + par
                s2k = s2[k * CHUNK:(k + 1) * CHUNK]
                s = jnp.where(own, s2k[:, 0:CHUNK], s2k[:, CHUNK:2 * CHUNK]) + bias_ref[variant, h]
                sink = sink_ref[h]
                m = jnp.maximum(jnp.max(s, axis=-1, keepdims=True), sink)
                p = jnp.exp(s - m)
                den = jnp.sum(p, axis=-1, keepdims=True) + jnp.exp(sink - m)
                recips[k][par] = 1.0 / den
                rows.append(jnp.concatenate([jnp.where(own, p, 0.0), jnp.where(own, 0.0, p)],
                                            axis=1).astype(BF16))
            lhs_cols.append(jnp.concatenate(rows, axis=0))
        tick()
        o4 = _dot(jnp.concatenate(lhs_cols, axis=1), jnp.concatenate(vz, axis=0))
        for k in range(n_pair):
            r_pair = jnp.where(low_q, jnp.broadcast_to(recips[k][0], (CHUNK, LANE)),
                               jnp.broadcast_to(recips[k][1], (CHUNK, LANE)))
            pairs.append(o4[k * CHUNK:(k + 1) * CHUNK] * r_pair)
    y = jnp.concatenate(pairs, axis=-1) * _silu(src[:, A_ZA:A_ZA + 1024])
    tick()
    tick()
    return y


def _att_kernel(rb_ref, sink_ref, bkt_ref, x0_ref, xa_ref, xb_ref, gpre_ref, win_hbm, wbr_hbm,
                o_ref, w_ref, wg_ref, wbr_ref, stage, stage_t, sem, slot_a, slot_b, h_ref, kv_prev, bias_ref,
                *, nb, layer):
    i = pl.program_id(0)
    plan = _proj_plan([(w_ref, A_G), (wg_ref, D_MODEL)])
    n_ticks = 2 * (ATT_KV_HEADS + 1) + 2

    @pl.when(i == 0)
    def _():
        qi = lax.broadcasted_iota(jnp.int32, (CHUNK, CHUNK), 0)
        ci = lax.broadcasted_iota(jnp.int32, (CHUNK, CHUNK), 1)
        own = ci <= qi
        bkt = jnp.where(own, bkt_ref[:, CHUNK:2 * CHUNK], bkt_ref[:, 0:CHUNK])
        for h in range(ATT_HEADS):
            acc = jnp.zeros((CHUNK, CHUNK), F32)
            for b in range(REL_BUCKETS):
                acc = jnp.where(bkt == b, rb_ref[b, h], acc)
            bias_ref[1, h] = acc
            bias_ref[0, h] = jnp.where(own, acc, NEG)
        kv_prev[...] = jnp.zeros_like(kv_prev)
        _load_weights(_weight_jobs_t(win_hbm, layer, 0, A_G, w_ref)
                      + _weight_jobs_t(win_hbm, layer, W_GATE_COL0, D_MODEL, wg_ref)
                      + _weight_jobs(wbr_hbm, layer, 1024, 0, D_MODEL, wbr_ref), stage, stage_t, sem)
        _rms_proj(x0_ref, gpre_ref, slot_a, h_ref, plan)

    def mix(src, variant, rows, x_next_ref, dst):
        _rms_norm_to(x_next_ref, gpre_ref, h_ref)
        spread = _Spread(_proj_thunks(h_ref, dst, plan), n_ticks)
        y = _att_block(src, kv_prev, bias_ref, sink_ref, variant, tick=spread.tick)
        spread.flush()
        kv_prev[...] = src[:, A_K:A_K + 2 * LANE]
        o_ref[rows, :] = _sigmoid(src[:, A_G:A_G + D_MODEL]) * _dot(y.astype(BF16), wbr_ref[:, 0:D_MODEL])

    first = ((2 * i) % nb) == 0
    mix(slot_a, jnp.where(first, 0, 1), slice(0, CHUNK), xa_ref, slot_b)
    mix(slot_b, 1, slice(CHUNK, 2 * CHUNK), xb_ref, slot_a)


def _sg_block(src, lng_ref, lnb_ref, wt_ref, bfull_ref, tick=lambda: None):
    v = src[:, S_VS:S_VS + SG_WIDTH]
    mu = jnp.mean(v, axis=-1, keepdims=True)
    vc = v - mu
    var = jnp.mean(vc * vc, axis=-1, keepdims=True)
    vn = (vc * lax.rsqrt(var + EPS) * lng_ref[...] + lnb_ref[...]).astype(BF16)
    out = []
    for g in range(SG_GROUPS):
        tick()
        sl = slice(g * LANE, (g + 1) * LANE)
        mixed = _dot(wt_ref[g], vn[:, sl]) + bfull_ref[:, sl]
        out.append(src[:, S_U + g * LANE:S_U + (g + 1) * LANE] * mixed
                   * _silu(src[:, S_ZS + g * LANE:S_ZS + (g + 1) * LANE]))
    return jnp.concatenate(out, axis=-1)


def _sg_kernel(x0_ref, xa_ref, xb_ref, acc_ref, gpre_ref, win_hbm, wbr_hbm,
               lng_ref, lnb_ref, ws_ref, bfull_ref, o_ref, w_ref, wg_ref, wbr_ref, stage, stage_t, sem,
               slot_a, slot_b, h_ref, wt_ref, *, layer):
    i = pl.program_id(0)
    plan = _proj_plan([(w_ref, S_G), (wg_ref, D_MODEL)])

    @pl.when(i == 0)
    def _():
        ti = lax.broadcasted_iota(jnp.int32, (CHUNK, CHUNK), 0)
        si = lax.broadcasted_iota(jnp.int32, (CHUNK, CHUNK), 1)
        for g in range(SG_GROUPS):
            wt_ref[g] = jnp.where(si <= ti, ws_ref[g], 0.0).astype(BF16)
        _load_weights(_weight_jobs_t(win_hbm, layer, W_SG_COL0, S_G, w_ref)
                      + _weight_jobs_t(win_hbm, layer, W_GATE_COL0 + D_MODEL, D_MODEL, wg_ref)
                      + _weight_jobs(wbr_hbm, layer, SG_WIDTH, 0, D_MODEL, wbr_ref), stage, stage_t, sem)
        _rms_proj(x0_ref, gpre_ref, slot_a, h_ref, plan)

    def mix(src, rows, x_next_ref, dst):
        _rms_norm_to(x_next_ref, gpre_ref, h_ref)
        spread = _Spread(_proj_thunks(h_ref, dst, plan), SG_GROUPS)
        y = _sg_block(src, lng_ref, lnb_ref, wt_ref, bfull_ref, tick=spread.tick)
        spread.flush()
        o_ref[rows, :] = (acc_ref[rows, :]
                          + _sigmoid(src[:, S_G:S_G + D_MODEL]) * _dot(y.astype(BF16), wbr_ref[:, 0:D_MODEL]))

    mix(slot_a, slice(0, CHUNK), xa_ref, slot_b)
    mix(slot_b, slice(CHUNK, 2 * CHUNK), xb_ref, slot_a)


def _ssd_block(src, first, cwx_ref, cbx_ref, cwb_ref, cbb_ref, dtb_ref, alog_ref, drow_ref, ng_ref,
               ext_x, ext_b, h_ref, xc_ref, y_ref, tick=lambda: None):
    n_xt, n_bt = SSM_WIDTH // LANE, SSM_BC // LANE
    tail = slice(CHUNK, CHUNK + 8)

    if first is not None:
        @pl.when(first)
        def _():
            h_ref[...] = jnp.zeros_like(h_ref)

    def stage(ext, col0, n_tiles):
        for t in range(n_tiles):
            prev = ext[t, tail, :]
            ext[t, 0:8, :] = prev if first is None else jnp.where(first, 0.0, prev)
            ext[t, 8:8 + CHUNK, :] = src[:, col0 + t * LANE:col0 + (t + 1) * LANE]

    def conv_tile(ext, w_ref, b_ref, t):
        sl = slice(t * LANE, (t + 1) * LANE)
        acc = b_ref[:, sl]
        for k in range(SSM_CONV):
            acc = acc + w_ref[k, :, sl] * ext[t, pl.ds(8 - (SSM_CONV - 1) + k, CHUNK), :]
        return _silu(acc)

    stage(ext_x, M_XS, n_xt)
    stage(ext_b, M_BC, n_bt)
    for t in range(n_xt):
        xc_ref[:, t * LANE:(t + 1) * LANE] = conv_tile(ext_x, cwx_ref, cbx_ref, t)
    bcv = []
    for t in range(n_bt):
        bcv.append(conv_tile(ext_b, cwb_ref, cbb_ref, t))

    li = lax.broadcasted_iota(jnp.int32, (CHUNK, CHUNK), 0)
    si = lax.broadcasted_iota(jnp.int32, (CHUNK, CHUNK), 1)
    causal = si <= li
    x_dt_t = (src[:, M_DT:M_DT + LANE] + dtb_ref[...]).T[0:SSM_HEADS, :]
    dt_t = _softplus(x_dt_t)
    a_dt_t = dt_t * (-jnp.exp(alog_ref[0:SSM_HEADS, :]))
    upper = jnp.where(li <= si, 1.0, 0.0).astype(BF16)
    parts = _dot(jnp.concatenate(_split3(a_dt_t), axis=0), upper)
    a_cs_t = (parts[0:SSM_HEADS] + parts[SSM_HEADS:2 * SSM_HEADS]
              + parts[2 * SSM_HEADS:3 * SSM_HEADS])
    a2_t = a_cs_t * LOG2E
    a2_last = a2_t[:, CHUNK - 1:CHUNK]
    w_t = dt_t * jnp.exp2(a2_last - a2_t)
    cd_t = jnp.exp2(a2_last)
    a2 = jnp.concatenate([a2_t, jnp.zeros((CHUNK - SSM_HEADS, CHUNK), F32)], axis=0).T
    e2 = jnp.exp2(a2)
    low = lax.broadcasted_iota(jnp.int32, (CHUNK, LANE), 1) < SSM_HEAD_DIM
    low_row = low[0:1]
    keep_lo = jnp.where(low, 1.0, 0.0).astype(BF16)
    keep_hi = jnp.where(low, 0.0, 1.0).astype(BF16)
    gw = SSM_WIDTH // SSM_GROUPS

    for g in range(SSM_GROUPS):
        b_g, c_g = bcv[g], bcv[SSM_GROUPS + g]
        cb = jnp.where(causal, _dot_nt(c_g.astype(BF16), b_g.astype(BF16)), 0.0)
        b_gt = b_g.T
        y_off = _dot(c_g.astype(BF16), h_ref[:, g * gw:(g + 1) * gw].astype(BF16))
        for pr in range(g * SSM_HPG // 2, (g + 1) * SSM_HPG // 2):
            tick()
            sl = slice(pr * LANE, (pr + 1) * LANE)
            x_pair = xc_ref[:, sl].astype(BF16)
            x_bd = jnp.concatenate([x_pair * keep_lo, x_pair * keep_hi], axis=0)
            m_cols, bw_cols, e_cols = [], [], []
            for j in (2 * pr, 2 * pr + 1):
                a_col = jnp.broadcast_to(a2[:, j:j + 1], (CHUNK, CHUNK))
                decay = jnp.exp2(jnp.minimum(a_col - a2_t[j:j + 1, :], 0.0))
                m_cols.append((cb * decay * dt_t[j:j + 1, :]).astype(BF16))
                bw_cols.append((b_gt * w_t[j:j + 1, :]).astype(BF16))
                e_cols.append(jnp.broadcast_to(e2[:, j:j + 1], (CHUNK, LANE)))
            y_diag = _dot(jnp.concatenate(m_cols, axis=1), x_bd)
            up = _dot(jnp.concatenate(bw_cols, axis=1), x_bd)
            k = pr - g * SSM_HPG // 2
            y_ref[:, sl] = y_diag + jnp.where(low, e_cols[0], e_cols[1]) * y_off[:, k * LANE:(k + 1) * LANE]
            cd_pair = jnp.where(low_row, cd_t[2 * pr:2 * pr + 1, :], cd_t[2 * pr + 1:2 * pr + 2, :])
            h_ref[:, sl] = h_ref[:, sl] * cd_pair + up

    out = []
    for g in range(SSM_GROUPS):
        tick()
        sl = slice(g * gw, (g + 1) * gw)
        yg = (y_ref[:, sl] + drow_ref[:, sl] * xc_ref[:, sl]) * _silu(src[:, M_ZM + g * gw:M_ZM + (g + 1) * gw])
        yg = yg * lax.rsqrt(jnp.mean(yg * yg, axis=-1, keepdims=True) + EPS)
        out.append(yg * ng_ref[:, sl])
    return jnp.concatenate(out, axis=-1)


def _ssd_kernel(x0_ref, xa_ref, xb_ref, xres_ref, acc_ref, gpre_ref, win_hbm, wbr_hbm, wout_hbm,
                gpost_ref, cwx_ref, cbx_ref, cwb_ref, cbb_ref, dtb_ref, alog_ref, drow_ref, ng_ref,
                o_ref, w_ref, wg_ref, wbr_ref, wout_ref, stage, stage_t, sem, slot_a, slot_b, hn_ref, ext_x, ext_b,
                h_ref, xc_ref, y_ref, ybf_ref, mg_ref, out_ref, *, nb, layer):
    i = pl.program_id(0)
    plan = _proj_plan([(w_ref, M_G), (wg_ref, D_MODEL)])

    @pl.when(i == 0)
    def _():
        ext_x[...] = jnp.zeros_like(ext_x)
        ext_b[...] = jnp.zeros_like(ext_b)
        _load_weights(_weight_jobs_t(win_hbm, layer, W_SSM_COL0, M_G, w_ref)
                      + _weight_jobs_t(win_hbm, layer, W_GATE_COL0 + 2 * D_MODEL, D_MODEL, wg_ref)
                      + _weight_jobs(wbr_hbm, layer, SSM_WIDTH, 0, D_MODEL, wbr_ref)
                      + _weight_jobs(wout_hbm, layer, D_MODEL, 0, D_MODEL, wout_ref), stage, stage_t, sem)
        _rms_proj(x0_ref, gpre_ref, slot_a, hn_ref, plan)

    n_ticks = SSM_HEADS // 2 + SSM_GROUPS
    out_cols = _col_chunks(D_MODEL, D_MODEL // 256)

    def epilogue(src, rows):
        def branch(c0, c1):
            gate = _sigmoid(src[:, M_G + c0:M_G + c1])
            mg_ref[:, c0:c1] = (acc_ref[rows, c0:c1]
                                + gate * _dot(ybf_ref[...], wbr_ref[:, c0:c1])).astype(BF16)

        def outproj(c0, c1):
            out_ref[:, c0:c1] = _dot(mg_ref[...], wout_ref[:, c0:c1])

        def finish():
            out = out_ref[...]
            ms = jnp.mean(out * out, axis=-1, keepdims=True)
            o_ref[rows, :] = xres_ref[rows, :] + out * lax.rsqrt(ms + EPS) * gpost_ref[...]

        return ([functools.partial(branch, c0, c1) for c0, c1 in out_cols]
                + [functools.partial(outproj, c0, c1) for c0, c1 in out_cols] + [finish])

    def mix(src, first, x_next_ref, dst, pending):
        _rms_norm_to(x_next_ref, gpre_ref, hn_ref)
        spread = _Spread(pending + _proj_thunks(hn_ref, dst, plan), n_ticks)
        y = _ssd_block(src, first, cwx_ref, cbx_ref, cwb_ref, cbb_ref, dtb_ref, alog_ref, drow_ref, ng_ref,
                       ext_x, ext_b, h_ref, xc_ref, y_ref, tick=spread.tick)
        spread.flush()
        ybf_ref[...] = y.astype(BF16)

    first = ((2 * i) % nb) == 0
    mix(slot_a, first, xa_ref, slot_b, [])
    mix(slot_b, None, xb_ref, slot_a, epilogue(slot_a, slice(0, CHUNK)))
    for thunk in epilogue(slot_b, slice(CHUNK, 2 * CHUNK)):
        thunk()


def _x_specs(n_blk):
    return [pl.BlockSpec((CHUNK, D_MODEL), lambda i: (0, 0)),
            pl.BlockSpec((CHUNK, D_MODEL), lambda i: (2 * i + 1, 0)),
            pl.BlockSpec((CHUNK, D_MODEL), lambda i: (jnp.minimum(2 * i + 2, n_blk - 1), 0))]


def _const(shape):
    return pl.BlockSpec(shape, lambda i: (0,) * len(shape))


def _rows2():
    return pl.BlockSpec((2 * CHUNK, D_MODEL), lambda i: (i, 0))


def _weight_scratch(shapes):
    return ([pltpu.VMEM(s, BF16) for s in shapes]
            + [pltpu.VMEM((2, STAGE_ROWS, STAGE_COLS), F32), pltpu.VMEM((2, STAGE_COLS, STAGE_ROWS), F32),
               pltpu.SemaphoreType.DMA((2,))])


def _any():
    return pl.BlockSpec(memory_space=pl.ANY)


def _att_branch(x2, g_pre, w_in_t, w_br, layer, rel_bias, sinks, bucket, nb):
    t = x2.shape[0]
    n_blk = t // CHUNK
    smem = pl.BlockSpec(memory_space=pltpu.SMEM)
    return pl.pallas_call(
        functools.partial(_att_kernel, nb=nb, layer=layer),
        grid=(n_blk // 2,),
        in_specs=[smem, smem, _const((CHUNK, 2 * CHUNK))] + _x_specs(n_blk)
                 + [_const((1, D_MODEL)), _any(), _any()],
        out_specs=_rows2(),
        out_shape=jax.ShapeDtypeStruct((t, D_MODEL), F32),
        scratch_shapes=_weight_scratch([(D_MODEL, A_G), (D_MODEL, W_PAD_COLS), (1024, W_PAD_COLS)])
                       + [pltpu.VMEM((CHUNK, A_COLS), F32), pltpu.VMEM((CHUNK, A_COLS), F32),
                        pltpu.VMEM((CHUNK, D_MODEL), BF16),
                        pltpu.VMEM((CHUNK, 2 * LANE), F32),
                        pltpu.VMEM((2, ATT_HEADS, CHUNK, CHUNK), F32)],
        compiler_params=_cparams(),
        name="att",
    )(rel_bias, sinks, bucket, x2, x2, x2, g_pre.reshape(1, -1), w_in_t, w_br)


def _sg_branch(x2, acc, g_pre, w_in_t, w_br, layer, ln_g, ln_b, w_s, b_full):
    t = x2.shape[0]
    n_blk = t // CHUNK
    return pl.pallas_call(
        functools.partial(_sg_kernel, layer=layer),
        grid=(n_blk // 2,),
        in_specs=_x_specs(n_blk) + [_rows2(), _const((1, D_MODEL)), _any(), _any(),
                                    _const((1, SG_WIDTH)), _const((1, SG_WIDTH)),
                                    _const((SG_GROUPS, CHUNK, CHUNK)), _const((CHUNK, SG_WIDTH))],
        out_specs=_rows2(),
        out_shape=jax.ShapeDtypeStruct((t, D_MODEL), F32),
        scratch_shapes=_weight_scratch([(D_MODEL, S_MAIN_COLS), (D_MODEL, W_PAD_COLS), (SG_WIDTH, W_PAD_COLS)])
                       + [pltpu.VMEM((CHUNK, S_COLS), F32), pltpu.VMEM((CHUNK, S_COLS), F32),
                        pltpu.VMEM((CHUNK, D_MODEL), BF16),
                        pltpu.VMEM((SG_GROUPS, CHUNK, CHUNK), BF16)],
        compiler_params=_cparams(),
        name="sg",
    )(x2, x2, x2, acc, g_pre.reshape(1, -1), w_in_t, w_br, ln_g.reshape(1, -1), ln_b.reshape(1, -1),
      w_s, b_full)


def _ssd_branch(x2, acc, g_pre, w_in_t, w_br, w_out, layer, g_post, cwx, cbx, cwb, cbb, dtb, alog, drow,
                ng, nb):
    t = x2.shape[0]
    n_blk = t // CHUNK
    return pl.pallas_call(
        functools.partial(_ssd_kernel, nb=nb, layer=layer),
        grid=(n_blk // 2,),
        in_specs=_x_specs(n_blk) + [_rows2(), _rows2(), _const((1, D_MODEL)), _any(), _any(), _any(),
                                    _const((1, D_MODEL)),
                                    _const((SSM_CONV, 1, SSM_WIDTH)), _const((1, SSM_WIDTH)),
                                    _const((SSM_CONV, 1, SSM_BC)), _const((1, SSM_BC)),
                                    _const((1, LANE)), _const((LANE, 1)),
                                    _const((1, SSM_WIDTH)), _const((1, SSM_WIDTH))],
        out_specs=_rows2(),
        out_shape=jax.ShapeDtypeStruct((t, D_MODEL), F32),
        scratch_shapes=_weight_scratch([(D_MODEL, M_G), (D_MODEL, W_PAD_COLS), (SSM_WIDTH, W_PAD_COLS),
                                        (D_MODEL, W_PAD_COLS)])
                       + [pltpu.VMEM((CHUNK, M_COLS), F32), pltpu.VMEM((CHUNK, M_COLS), F32),
                        pltpu.VMEM((CHUNK, D_MODEL), BF16),
                        pltpu.VMEM((SSM_WIDTH // LANE, 8 + CHUNK, LANE), F32),
                        pltpu.VMEM((SSM_BC // LANE, 8 + CHUNK, LANE), F32),
                        pltpu.VMEM((SSM_STATE, SSM_WIDTH), F32),
                        pltpu.VMEM((CHUNK, SSM_WIDTH), F32),
                        pltpu.VMEM((CHUNK, SSM_WIDTH), F32),
                        pltpu.VMEM((CHUNK, SSM_WIDTH), BF16),
                        pltpu.VMEM((CHUNK, D_MODEL), BF16),
                        pltpu.VMEM((CHUNK, D_MODEL), F32)],
        compiler_params=_cparams(),
        name="ssd",
    )(x2, x2, x2, x2, acc, g_pre.reshape(1, -1), w_in_t, w_br, w_out, g_post.reshape(1, -1),
      cwx, cbx, cwb, cbb, dtb, alog, drow, ng)


def _rel_bucket_table():
    qi = jnp.arange(CHUNK, dtype=jnp.int32)[:, None]
    kj = jnp.arange(2 * CHUNK, dtype=jnp.int32)[None, :]
    dist = jnp.maximum(qi + CHUNK - kj, 0)
    max_exact = REL_BUCKETS // 2
    dist_f = jnp.maximum(dist, 1).astype(F32)
    large = max_exact + (jnp.log(dist_f / max_exact) / math.log(REL_MAX_DIST / max_exact)
                         * (REL_BUCKETS - max_exact)).astype(jnp.int32)
    large = jnp.minimum(large, REL_BUCKETS - 1)
    return jnp.where(dist < max_exact, dist, large)


def _pad_lanes(v):
    return jnp.pad(v, (0, LANE - v.shape[0])).reshape(1, LANE)


def kernel(x, w_in, norm_pre, norm_post, rel_bias, att_sinks, sg_ln_g, sg_ln_b, sg_w, sg_b, ssm_conv_w, ssm_conv_b, ssm_dt_bias, ssm_a_log, ssm_d, ssm_norm_g, w_br_att, w_br_sg, w_br_ssm, w_out):
    bsz, seq, d = x.shape
    depth = w_in.shape[0]
    assert d == D_MODEL and seq % (2 * CHUNK) == 0
    nb = seq // CHUNK
    x2 = x.reshape(bsz * seq, d)
    bucket = _rel_bucket_table()
    w_in_t = jnp.swapaxes(w_in, 1, 2)
    for l in range(depth):
        acc = _att_branch(x2, norm_pre[l], w_in_t, w_br_att, l, rel_bias, att_sinks[l], bucket, nb)
        b_full = jnp.repeat(jnp.transpose(sg_b[l]), CHUNK, axis=1)
        acc = _sg_branch(x2, acc, norm_pre[l], w_in_t, w_br_sg, l, sg_ln_g[l], sg_ln_b[l], sg_w[l], b_full)
        cw, cb = ssm_conv_w[l], ssm_conv_b[l]
        x2 = _ssd_branch(x2, acc, norm_pre[l], w_in_t, w_br_ssm, w_out, l, norm_post[l],
                         cw[:, None, :SSM_WIDTH], cb[:SSM_WIDTH].reshape(1, -1),
                         cw[:, None, SSM_WIDTH:], cb[SSM_WIDTH:].reshape(1, -1),
                         _pad_lanes(ssm_dt_bias[l]), _pad_lanes(ssm_a_log[l]).reshape(LANE, 1),
                         jnp.repeat(ssm_d[l], SSM_HEAD_DIM).reshape(1, -1),
                         ssm_norm_g[l].reshape(1, -1), nb)
    return x2.reshape(bsz, seq, d)
```

```python
import functools
import math

import jax
import jax.numpy as jnp
from jax import lax
from jax.experimental import pallas as pl
from jax.experimental.pallas import tpu as pltpu

F32 = jnp.float32
BF16 = jnp.bfloat16

D_MODEL = 1024
ATT_HEADS = 16
ATT_KV_HEADS = 2
ATT_HEAD_DIM = 64
ATT_GROUP = ATT_HEADS // ATT_KV_HEADS
CHUNK = 128
REL_BUCKETS = 32
REL_MAX_DIST = 128
SG_GROUPS = 8
SG_WIDTH = 1024
SSM_WIDTH = 2048
SSM_HEAD_DIM = 64
SSM_HEADS = 32
SSM_GROUPS = 4
SSM_STATE = 128
SSM_HPG = SSM_HEADS // SSM_GROUPS
SSM_CONV = 4
SSM_BC = 2 * SSM_GROUPS * SSM_STATE
EPS = 1e-6
NEG = -1e30
LOG2E = 1.4426950408889634
LANE = 128

A_Q, A_K, A_V, A_ZA, A_G, A_COLS = 0, 1024, 1152, 1280, 2304, 3328
S_U, S_VS, S_ZS, S_G, S_COLS = 0, 1024, 2048, 3072, 4096
M_ZM, M_XS, M_BC, M_DT, M_G, M_COLS = 0, 2048, 4096, 5120, 5248, 6272
VMEM_LIMIT = 56 * 1024 * 1024
W_PAD_COLS = D_MODEL + LANE
S_MAIN_COLS = S_G + LANE
W_SG_COL0, W_SSM_COL0, W_GATE_COL0 = 2304, 5376, 10528
SG_PER_STEP = 4


def _cparams():
    return pltpu.CompilerParams(dimension_semantics=("arbitrary",), vmem_limit_bytes=VMEM_LIMIT)


def _sigmoid(x):
    return 0.5 * jnp.tanh(0.5 * x) + 0.5


def _silu(x):
    h = 0.5 * x
    return h * jnp.tanh(h) + h


def _softplus(x):
    return jnp.maximum(x, 0.0) + jnp.log1p(jnp.exp(-jnp.abs(x)))


def _dot(a, b):
    return jnp.dot(a, b, preferred_element_type=F32)


def _dot_nt(a, b):
    return lax.dot_general(a, b, (((1,), (1,)), ((), ())), preferred_element_type=F32)


def _rms_norm_to(x_ref, g_ref, h_ref):
    x = x_ref[...]
    ms = jnp.mean(x * x, axis=-1, keepdims=True)
    h_ref[...] = (x * lax.rsqrt(ms + EPS) * g_ref[...]).astype(BF16)


def _proj_cols(h_ref, w_ref, dst_ref, c0, c1, d0):
    dst_ref[:, d0:d0 + c1 - c0] = _dot(h_ref[...], w_ref[:, c0:c1])


def _proj_plan(parts, width=256):
    plan, d0 = [], 0
    for w_ref, n_cols in parts:
        for c0 in range(0, n_cols, width):
            c1 = min(c0 + width, n_cols)
            plan.append((w_ref, c0, c1, d0 + c0))
        d0 += n_cols
    return plan


def _proj_thunks(h_ref, dst_ref, plan):
    return [functools.partial(_proj_cols, h_ref, w_ref, dst_ref, c0, c1, d0) for w_ref, c0, c1, d0 in plan]


def _rms_proj(x_ref, g_ref, dst_ref, h_ref, plan):
    _rms_norm_to(x_ref, g_ref, h_ref)
    for thunk in _proj_thunks(h_ref, dst_ref, plan):
        thunk()


STAGE_ROWS, STAGE_COLS = 1024, 512


def _weight_jobs(src_hbm, layer, k_rows, col0, n_cols, dst_ref):
    jobs = []
    for r0 in range(0, k_rows, STAGE_ROWS):
        for c in range(0, n_cols, STAGE_COLS):
            w = min(STAGE_COLS, n_cols - c)
            jobs.append((src_hbm.at[layer, r0:r0 + STAGE_ROWS, col0 + c:col0 + c + w], dst_ref, r0, c, w))
    return jobs


def _weight_jobs_t(src_hbm, layer, row0, n_rows, dst_ref):
    jobs = []
    for c in range(0, n_rows, STAGE_COLS):
        w = min(STAGE_COLS, n_rows - c)
        jobs.append((src_hbm.at[layer, row0 + c:row0 + c + w, :], dst_ref, None, c, w))
    return jobs


def _load_weights(jobs, stage, stage_t, sem):
    def dma(k):
        src, _, r0, _, w = jobs[k]
        dst = stage_t.at[k % 2, 0:w, :] if r0 is None else stage.at[k % 2, :, 0:w]
        return pltpu.make_async_copy(src, dst, sem.at[k % 2])

    dma(0).start()
    for k, (_, dst, r0, d0, w) in enumerate(jobs):
        if k + 1 < len(jobs):
            dma(k + 1).start()
        dma(k).wait()
        if r0 is None:
            dst[:, d0:d0 + w] = stage_t[k % 2, 0:w, :].T.astype(BF16)
        else:
            dst[r0:r0 + STAGE_ROWS, d0:d0 + w] = stage[k % 2, :, 0:w].astype(BF16)


class _Spread:
    def __init__(self, thunks, n_ticks):
        self.thunks, self.n_ticks, self.ticks, self.done = thunks, n_ticks, 0, 0

    def tick(self):
        self.ticks += 1
        while self.done < len(self.thunks) and self.done * self.n_ticks < self.ticks * len(self.thunks):
            self.thunks[self.done]()
            self.done += 1

    def flush(self):
        self.ticks = self.n_ticks
        self.tick()


def _col_chunks(n_cols, n_chunks, width=256):
    tiles = -(-n_cols // width)
    per = [tiles // n_chunks + (1 if k < tiles % n_chunks else 0) for k in range(n_chunks)]
    out, c = [], 0
    for n_tiles in per:
        out.append((c, min(c + n_tiles * width, n_cols)))
        c = min(c + n_tiles * width, n_cols)
    return out


def _split3(a):
    hi = a.astype(BF16)
    r1 = a - hi.astype(F32)
    mid = r1.astype(BF16)
    lo = (r1 - mid.astype(F32)).astype(BF16)
    return hi, mid, lo


def _att_block(src, kv_prev, bias_ref, sink_ref, variant, tick=lambda: None):
    qi = lax.broadcasted_iota(jnp.int32, (CHUNK, CHUNK), 0)
    ci = lax.broadcasted_iota(jnp.int32, (CHUNK, CHUNK), 1)
    own = ci <= qi
    low = lax.broadcasted_iota(jnp.int32, (2 * CHUNK, LANE), 1) < ATT_HEAD_DIM
    low_q = lax.broadcasted_iota(jnp.int32, (CHUNK, LANE), 1) < ATT_HEAD_DIM
    scale = ATT_HEAD_DIM ** -0.5
    kcat = jnp.concatenate([src[:, A_K:A_K + LANE], kv_prev[:, 0:LANE]], axis=0)
    vcat = jnp.concatenate([src[:, A_V:A_V + LANE], kv_prev[:, LANE:2 * LANE]], axis=0)
    k_roll = pltpu.roll(kcat, ATT_HEAD_DIM, 1)
    v_roll = pltpu.roll(vcat, ATT_HEAD_DIM, 1)
    n_pair = ATT_GROUP // 2
    pairs = []
    for kvh in range(ATT_KV_HEADS):
        k_lo, k_hi = (kcat, k_roll) if kvh == 0 else (k_roll, kcat)
        v_lo, v_hi = (vcat, v_roll) if kvh == 0 else (v_roll, vcat)
        kz = (jnp.where(low, k_lo, 0.0).astype(BF16), jnp.where(low, 0.0, k_hi).astype(BF16))
        vz = (jnp.where(low, v_lo, 0.0).astype(BF16), jnp.where(low, 0.0, v_hi).astype(BF16))
        pr0 = kvh * n_pair
        q4 = jnp.concatenate([src[:, A_Q + pr * LANE:A_Q + (pr + 1) * LANE]
                              for pr in range(pr0, pr0 + n_pair)], axis=0)
        q4 = (q4 * scale).astype(BF16)
        lhs_cols = []
        recips = [[None, None] for _ in range(n_pair)]
        for par in range(2):
            tick()
            s2 = _dot_nt(q4, kz[par])
            rows = []
            for k in range(n_pair):
                h = 2 * (pr0 + k) + par
                s2k = s2[k * CHUNK:(k + 1) * CHUNK]
                s = jnp.where(own, s2k[:, 0:CHUNK], s2k[:, CHUNK:2 * CHUNK]) + bias_ref[variant, h]
                sink = sink_ref[h]
                m = jnp.maximum(jnp.max(s, axis=-1, keepdims=True), sink)
                p = jnp.exp(s - m)
                den = jnp.sum(p, axis=-1, keepdims=True) + jnp.exp(sink - m)
                recips[k][par] = 1.0 / den
                rows.append(jnp.concatenate([jnp.where(own, p, 0.0), jnp.where(own, 0.0, p)],
                                            axis=1).astype(BF16))
            lhs_cols.append(jnp.concatenate(rows, axis=0))
        tick()
        o4 = _dot(jnp.concatenate(lhs_cols, axis=1), jnp.concatenate(vz, axis=0))
        for k in range(n_pair):
            r_pair = jnp.where(low_q, jnp.broadcast_to(recips[k][0], (CHUNK, LANE)),
                               jnp.broadcast_to(recips[k][1], (CHUNK, LANE)))
            pairs.append(o4[k * CHUNK:(k + 1) * CHUNK] * r_pair)
    y = jnp.concatenate(pairs, axis=-1) * _silu(src[:, A_ZA:A_ZA + 1024])
    tick()
    tick()
    return y


def _att_kernel(rb_ref, sink_ref, bkt_ref, x0_ref, xa_ref, xb_ref, gpre_ref, win_hbm, wbr_hbm,
                o_ref, w_ref, wg_ref, wbr_ref, stage, stage_t, sem, slot_a, slot_b, h_ref, kv_prev, bias_ref,
                *, nb, layer):
    i = pl.program_id(0)
    plan = _proj_plan([(w_ref, A_G), (wg_ref, D_MODEL)])
    n_ticks = 2 * (ATT_KV_HEADS + 1) + 2

    @pl.when(i == 0)
    def _():
        qi = lax.broadcasted_iota(jnp.int32, (CHUNK, CHUNK), 0)
        ci = lax.broadcasted_iota(jnp.int32, (CHUNK, CHUNK), 1)
        own = ci <= qi
        bkt = jnp.where(own, bkt_ref[:, CHUNK:2 * CHUNK], bkt_ref[:, 0:CHUNK])
        for h in range(ATT_HEADS):
            acc = jnp.zeros((CHUNK, CHUNK), F32)
            for b in range(REL_BUCKETS):
                acc = jnp.where(bkt == b, rb_ref[b, h], acc)
            bias_ref[1, h] = acc
            bias_ref[0, h] = jnp.where(own, acc, NEG)
        kv_prev[...] = jnp.zeros_like(kv_prev)
        _load_weights(_weight_jobs_t(win_hbm, layer, 0, A_G, w_ref)
                      + _weight_jobs_t(win_hbm, layer, W_GATE_COL0, D_MODEL, wg_ref)
                      + _weight_jobs(wbr_hbm, layer, 1024, 0, D_MODEL, wbr_ref), stage, stage_t, sem)
        _rms_proj(x0_ref, gpre_ref, slot_a, h_ref, plan)

    def mix(src, variant, rows, x_next_ref, dst):
        _rms_norm_to(x_next_ref, gpre_ref, h_ref)
        spread = _Spread(_proj_thunks(h_ref, dst, plan), n_ticks)
        y = _att_block(src, kv_prev, bias_ref, sink_ref, variant, tick=spread.tick)
        spread.flush()
        kv_prev[...] = src[:, A_K:A_K + 2 * LANE]
        o_ref[rows, :] = _sigmoid(src[:, A_G:A_G + D_MODEL]) * _dot(y.astype(BF16), wbr_ref[:, 0:D_MODEL])

    first = ((2 * i) % nb) == 0
    mix(slot_a, jnp.where(first, 0, 1), slice(0, CHUNK), xa_ref, slot_b)
    mix(slot_b, 1, slice(CHUNK, 2 * CHUNK), xb_ref, slot_a)


def _sg_block(src, lng_ref, lnb_ref, wt_ref, bfull_ref, tick=lambda: None):
    v = src[:, S_VS:S_VS + SG_WIDTH]
    mu = jnp.mean(v, axis=-1, keepdims=True)
    vc = v - mu
    var = jnp.mean(vc * vc, axis=-1, keepdims=True)
    vn = (vc * lax.rsqrt(var + EPS) * lng_ref[...] + lnb_ref[...]).astype(BF16)
    out = []
    for g in range(SG_GROUPS):
        tick()
        sl = slice(g * LANE, (g + 1) * LANE)
        mixed = _dot(wt_ref[g], vn[:, sl]) + bfull_ref[:, sl]
        out.append(src[:, S_U + g * LANE:S_U + (g + 1) * LANE] * mixed
                   * _silu(src[:, S_ZS + g * LANE:S_ZS + (g + 1) * LANE]))
    return jnp.concatenate(out, axis=-1)


def _sg_kernel(x0_ref, *refs, layer):
    x_next, refs = refs[:SG_PER_STEP], refs[SG_PER_STEP:]
    (acc_ref, gpre_ref, win_hbm, wbr_hbm, lng_ref, lnb_ref, ws_ref, bfull_ref, o_ref, w_ref, wg_ref, wbr_ref,
     stage, stage_t, sem, slot_a, slot_b, h_ref, wt_ref) = refs
    i = pl.program_id(0)
    plan = _proj_plan([(w_ref, S_G), (wg_ref, D_MODEL)])

    @pl.when(i == 0)
    def _():
        ti = lax.broadcasted_iota(jnp.int32, (CHUNK, CHUNK), 0)
        si = lax.broadcasted_iota(jnp.int32, (CHUNK, CHUNK), 1)
        for g in range(SG_GROUPS):
            wt_ref[g] = jnp.where(si <= ti, ws_ref[g], 0.0).astype(BF16)
        _load_weights(_weight_jobs_t(win_hbm, layer, W_SG_COL0, S_G, w_ref)
                      + _weight_jobs_t(win_hbm, layer, W_GATE_COL0 + D_MODEL, D_MODEL, wg_ref)
                      + _weight_jobs(wbr_hbm, layer, SG_WIDTH, 0, D_MODEL, wbr_ref), stage, stage_t, sem)
        _rms_proj(x0_ref, gpre_ref, slot_a, h_ref, plan)

    def mix(src, rows, x_next_ref, dst):
        _rms_norm_to(x_next_ref, gpre_ref, h_ref)
        spread = _Spread(_proj_thunks(h_ref, dst, plan), SG_GROUPS)
        y = _sg_block(src, lng_ref, lnb_ref, wt_ref, bfull_ref, tick=spread.tick)
        spread.flush()
        o_ref[rows, :] = (acc_ref[rows, :]
                          + _sigmoid(src[:, S_G:S_G + D_MODEL]) * _dot(y.astype(BF16), wbr_ref[:, 0:D_MODEL]))

    slots = (slot_a, slot_b)
    for k in range(SG_PER_STEP):
        mix(slots[k % 2], slice(k * CHUNK, (k + 1) * CHUNK), x_next[k], slots[(k + 1) % 2])


def _ssd_block(src, first, cwx_ref, cbx_ref, cwb_ref, cbb_ref, dtb_ref, alog_ref, drow_ref, ng_ref,
               ext_x, ext_b, h_ref, xc_ref, y_ref, tick=lambda: None):
    n_xt, n_bt = SSM_WIDTH // LANE, SSM_BC // LANE
    tail = slice(CHUNK, CHUNK + 8)

    if first is not None:
        @pl.when(first)
        def _():
            h_ref[...] = jnp.zeros_like(h_ref)

    def stage(ext, col0, n_tiles):
        for t in range(n_tiles):
            prev = ext[t, tail, :]
            ext[t, 0:8, :] = prev if first is None else jnp.where(first, 0.0, prev)
            ext[t, 8:8 + CHUNK, :] = src[:, col0 + t * LANE:col0 + (t + 1) * LANE]

    def conv_tile(ext, w_ref, b_ref, t):
        sl = slice(t * LANE, (t + 1) * LANE)
        acc = b_ref[:, sl]
        for k in range(SSM_CONV):
            acc = acc + w_ref[k, :, sl] * ext[t, pl.ds(8 - (SSM_CONV - 1) + k, CHUNK), :]
        return _silu(acc)

    stage(ext_x, M_XS, n_xt)
    stage(ext_b, M_BC, n_bt)
    for t in range(n_xt):
        xc_ref[:, t * LANE:(t + 1) * LANE] = conv_tile(ext_x, cwx_ref, cbx_ref, t)
    bcv = []
    for t in range(n_bt):
        bcv.append(conv_tile(ext_b, cwb_ref, cbb_ref, t))

    li = lax.broadcasted_iota(jnp.int32, (CHUNK, CHUNK), 0)
    si = lax.broadcasted_iota(jnp.int32, (CHUNK, CHUNK), 1)
    causal = si <= li
    x_dt_t = (src[:, M_DT:M_DT + LANE] + dtb_ref[...]).T[0:SSM_HEADS, :]
    dt_t = _softplus(x_dt_t)
    a_dt_t = dt_t * (-jnp.exp(alog_ref[0:SSM_HEADS, :]))
    upper = jnp.where(li <= si, 1.0, 0.0).astype(BF16)
    parts = _dot(jnp.concatenate(_split3(a_dt_t), axis=0), upper)
    a_cs_t = (parts[0:SSM_HEADS] + parts[SSM_HEADS:2 * SSM_HEADS]
              + parts[2 * SSM_HEADS:3 * SSM_HEADS])
    a2_t = a_cs_t * LOG2E
    a2_last = a2_t[:, CHUNK - 1:CHUNK]
    w_t = dt_t * jnp.exp2(a2_last - a2_t)
    cd_t = jnp.exp2(a2_last)
    a2 = jnp.concatenate([a2_t, jnp.zeros((CHUNK - SSM_HEADS, CHUNK), F32)], axis=0).T
    e2 = jnp.exp2(a2)
    low = lax.broadcasted_iota(jnp.int32, (CHUNK, LANE), 1) < SSM_HEAD_DIM
    low_row = low[0:1]
    keep_lo = jnp.where(low, 1.0, 0.0).astype(BF16)
    keep_hi = jnp.where(low, 0.0, 1.0).astype(BF16)
    gw = SSM_WIDTH // SSM_GROUPS

    for g in range(SSM_GROUPS):
        b_g, c_g = bcv[g], bcv[SSM_GROUPS + g]
        cb = jnp.where(causal, _dot_nt(c_g.astype(BF16), b_g.astype(BF16)), 0.0)
        b_gt = b_g.T
        y_off = _dot(c_g.astype(BF16), h_ref[:, g * gw:(g + 1) * gw].astype(BF16))
        for pr in range(g * SSM_HPG // 2, (g + 1) * SSM_HPG // 2):
            tick()
            sl = slice(pr * LANE, (pr + 1) * LANE)
            x_pair = xc_ref[:, sl].astype(BF16)
            x_bd = jnp.concatenate([x_pair * keep_lo, x_pair * keep_hi], axis=0)
            m_cols, bw_cols, e_cols = [], [], []
            for j in (2 * pr, 2 * pr + 1):
                a_col = jnp.broadcast_to(a2[:, j:j + 1], (CHUNK, CHUNK))
                decay = jnp.exp2(jnp.minimum(a_col - a2_t[j:j + 1, :], 0.0))
                m_cols.append((cb * decay * dt_t[j:j + 1, :]).astype(BF16))
                bw_cols.append((b_gt * w_t[j:j + 1, :]).astype(BF16))
                e_cols.append(jnp.broadcast_to(e2[:, j:j + 1], (CHUNK, LANE)))
            y_diag = _dot(jnp.concatenate(m_cols, axis=1), x_bd)
            up = _dot(jnp.concatenate(bw_cols, axis=1), x_bd)
            k = pr - g * SSM_HPG // 2
            y_ref[:, sl] = y_diag + jnp.where(low, e_cols[0], e_cols[1]) * y_off[:, k * LANE:(k + 1) * LANE]
            cd_pair = jnp.where(low_row, cd_t[2 * pr:2 * pr + 1, :], cd_t[2 * pr + 1:2 * pr + 2, :])
            h_ref[:, sl] = h_ref[:, sl] * cd_pair + up

    out = []
    for g in range(SSM_GROUPS):
        tick()
        sl = slice(g * gw, (g + 1) * gw)
        yg = (y_ref[:, sl] + drow_ref[:, sl] * xc_ref[:, sl]) * _silu(src[:, M_ZM + g * gw:M_ZM + (g + 1) * gw])
        yg = yg * lax.rsqrt(jnp.mean(yg * yg, axis=-1, keepdims=True) + EPS)
        out.append(yg * ng_ref[:, sl])
    return jnp.concatenate(out, axis=-1)


def _ssd_kernel(x0_ref, xa_ref, xb_ref, xres_ref, acc_ref, gpre_ref, win_hbm, wbr_hbm, wout_hbm,
                gpost_ref, cwx_ref, cbx_ref, cwb_ref, cbb_ref, dtb_ref, alog_ref, drow_ref, ng_ref,
                o_ref, w_ref, wg_ref, wbr_ref, wout_ref, stage, stage_t, sem, slot_a, slot_b, hn_ref, ext_x, ext_b,
                h_ref, xc_ref, y_ref, ybf_ref, mg_ref, out_ref, *, nb, layer):
    i = pl.program_id(0)
    plan = _proj_plan([(w_ref, M_G), (wg_ref, D_MODEL)])

    @pl.when(i == 0)
    def _():
        ext_x[...] = jnp.zeros_like(ext_x)
        ext_b[...] = jnp.zeros_like(ext_b)
        _load_weights(_weight_jobs_t(win_hbm, layer, W_SSM_COL0, M_G, w_ref)
                      + _weight_jobs_t(win_hbm, layer, W_GATE_COL0 + 2 * D_MODEL, D_MODEL, wg_ref)
                      + _weight_jobs(wbr_hbm, layer, SSM_WIDTH, 0, D_MODEL, wbr_ref)
                      + _weight_jobs(wout_hbm, layer, D_MODEL, 0, D_MODEL, wout_ref), stage, stage_t, sem)
        _rms_proj(x0_ref, gpre_ref, slot_a, hn_ref, plan)

    n_ticks = SSM_HEADS // 2 + SSM_GROUPS
    out_cols = _col_chunks(D_MODEL, D_MODEL // 256)

    def epilogue(src, rows):
        def branch(c0, c1):
            gate = _sigmoid(src[:, M_G + c0:M_G + c1])
            mg_ref[:, c0:c1] = (acc_ref[rows, c0:c1]
                                + gate * _dot(ybf_ref[...], wbr_ref[:, c0:c1])).astype(BF16)

        def outproj(c0, c1):
            out_ref[:, c0:c1] = _dot(mg_ref[...], wout_ref[:, c0:c1])

        def finish():
            out = out_ref[...]
            ms = jnp.mean(out * out, axis=-1, keepdims=True)
            o_ref[rows, :] = xres_ref[rows, :] + out * lax.rsqrt(ms + EPS) * gpost_ref[...]

        return ([functools.partial(branch, c0, c1) for c0, c1 in out_cols]
                + [functools.partial(outproj, c0, c1) for c0, c1 in out_cols] + [finish])

    def mix(src, first, x_next_ref, dst, pending):
        _rms_norm_to(x_next_ref, gpre_ref, hn_ref)
        spread = _Spread(pending + _proj_thunks(hn_ref, dst, plan), n_ticks)
        y = _ssd_block(src, first, cwx_ref, cbx_ref, cwb_ref, cbb_ref, dtb_ref, alog_ref, drow_ref, ng_ref,
                       ext_x, ext_b, h_ref, xc_ref, y_ref, tick=spread.tick)
        spread.flush()
        ybf_ref[...] = y.astype(BF16)

    first = ((2 * i) % nb) == 0
    mix(slot_a, first, xa_ref, slot_b, [])
    mix(slot_b, None, xb_ref, slot_a, epilogue(slot_a, slice(0, CHUNK)))
    for thunk in epilogue(slot_b, slice(CHUNK, 2 * CHUNK)):
        thunk()


def _x_specs(n_blk, per_step=2):
    nxt = lambda k: pl.BlockSpec((CHUNK, D_MODEL), lambda i: (jnp.minimum(per_step * i + k, n_blk - 1), 0))
    return [pl.BlockSpec((CHUNK, D_MODEL), lambda i: (0, 0))] + [nxt(k) for k in range(1, per_step + 1)]


def _const(shape):
    return pl.BlockSpec(shape, lambda i: (0,) * len(shape))


def _rows2(per_step=2):
    return pl.BlockSpec((per_step * CHUNK, D_MODEL), lambda i: (i, 0))


def _weight_scratch(shapes):
    return ([pltpu.VMEM(s, BF16) for s in shapes]
            + [pltpu.VMEM((2, STAGE_ROWS, STAGE_COLS), F32), pltpu.VMEM((2, STAGE_COLS, STAGE_ROWS), F32),
               pltpu.SemaphoreType.DMA((2,))])


def _any():
    return pl.BlockSpec(memory_space=pl.ANY)


def _att_branch(x2, g_pre, w_in_t, w_br, layer, rel_bias, sinks, bucket, nb):
    t = x2.shape[0]
    n_blk = t // CHUNK
    smem = pl.BlockSpec(memory_space=pltpu.SMEM)
    return pl.pallas_call(
        functools.partial(_att_kernel, nb=nb, layer=layer),
        grid=(n_blk // 2,),
        in_specs=[smem, smem, _const((CHUNK, 2 * CHUNK))] + _x_specs(n_blk)
                 + [_const((1, D_MODEL)), _any(), _any()],
        out_specs=_rows2(),
        out_shape=jax.ShapeDtypeStruct((t, D_MODEL), F32),
        scratch_shapes=_weight_scratch([(D_MODEL, A_G), (D_MODEL, W_PAD_COLS), (1024, W_PAD_COLS)])
                       + [pltpu.VMEM((CHUNK, A_COLS), F32), pltpu.VMEM((CHUNK, A_COLS), F32),
                        pltpu.VMEM((CHUNK, D_MODEL), BF16),
                        pltpu.VMEM((CHUNK, 2 * LANE), F32),
                        pltpu.VMEM((2, ATT_HEADS, CHUNK, CHUNK), F32)],
        compiler_params=_cparams(),
        name="att",
    )(rel_bias, sinks, bucket, x2, x2, x2, g_pre.reshape(1, -1), w_in_t, w_br)


def _sg_branch(x2, acc, g_pre, w_in_t, w_br, layer, ln_g, ln_b, w_s, b_full):
    t = x2.shape[0]
    n_blk = t // CHUNK
    return pl.pallas_call(
        functools.partial(_sg_kernel, layer=layer),
        grid=(n_blk // SG_PER_STEP,),
        in_specs=_x_specs(n_blk, SG_PER_STEP) + [_rows2(SG_PER_STEP), _const((1, D_MODEL)), _any(), _any(),
                                    _const((1, SG_WIDTH)), _const((1, SG_WIDTH)),
                                    _const((SG_GROUPS, CHUNK, CHUNK)), _const((CHUNK, SG_WIDTH))],
        out_specs=_rows2(SG_PER_STEP),
        out_shape=jax.ShapeDtypeStruct((t, D_MODEL), F32),
        scratch_shapes=_weight_scratch([(D_MODEL, S_MAIN_COLS), (D_MODEL, W_PAD_COLS), (SG_WIDTH, W_PAD_COLS)])
                       + [pltpu.VMEM((CHUNK, S_COLS), F32), pltpu.VMEM((CHUNK, S_COLS), F32),
                        pltpu.VMEM((CHUNK, D_MODEL), BF16),
                        pltpu.VMEM((SG_GROUPS, CHUNK, CHUNK), BF16)],
        compiler_params=_cparams(),
        name="sg",
    )(*([x2] * (1 + SG_PER_STEP)), acc, g_pre.reshape(1, -1), w_in_t, w_br, ln_g.reshape(1, -1), ln_b.reshape(1, -1),
      w_s, b_full)


def _ssd_branch(x2, acc, g_pre, w_in_t, w_br, w_out, layer, g_post, cwx, cbx, cwb, cbb, dtb, alog, drow,
                ng, nb):
    t = x2.shape[0]
    n_blk = t // CHUNK
    return pl.pallas_call(
        functools.partial(_ssd_kernel, nb=nb, layer=layer),
        grid=(n_blk // 2,),
        in_specs=_x_specs(n_blk) + [_rows2(), _rows2(), _const((1, D_MODEL)), _any(), _any(), _any(),
                                    _const((1, D_MODEL)),
                                    _const((SSM_CONV, 1, SSM_WIDTH)), _const((1, SSM_WIDTH)),
                                    _const((SSM_CONV, 1, SSM_BC)), _const((1, SSM_BC)),
                                    _const((1, LANE)), _const((LANE, 1)),
                                    _const((1, SSM_WIDTH)), _const((1, SSM_WIDTH))],
        out_specs=_rows2(),
        out_shape=jax.ShapeDtypeStruct((t, D_MODEL), F32),
        scratch_shapes=_weight_scratch([(D_MODEL, M_G), (D_MODEL, W_PAD_COLS), (SSM_WIDTH, W_PAD_COLS),
                                        (D_MODEL, W_PAD_COLS)])
                       + [pltpu.VMEM((CHUNK, M_COLS), F32), pltpu.VMEM((CHUNK, M_COLS), F32),
                        pltpu.VMEM((CHUNK, D_MODEL), BF16),
                        pltpu.VMEM((SSM_WIDTH // LANE, 8 + CHUNK, LANE), F32),
                        pltpu.VMEM((SSM_BC // LANE, 8 + CHUNK, LANE), F32),
                        pltpu.VMEM((SSM_STATE, SSM_WIDTH), F32),
                        pltpu.VMEM((CHUNK, SSM_WIDTH), F32),
                        pltpu.VMEM((CHUNK, SSM_WIDTH), F32),
                        pltpu.VMEM((CHUNK, SSM_WIDTH), BF16),
                        pltpu.VMEM((CHUNK, D_MODEL), BF16),
                        pltpu.VMEM((CHUNK, D_MODEL), F32)],
        compiler_params=_cparams(),
        name="ssd",
    )(x2, x2, x2, x2, acc, g_pre.reshape(1, -1), w_in_t, w_br, w_out, g_post.reshape(1, -1),
      cwx, cbx, cwb, cbb, dtb, alog, drow, ng)


def _rel_bucket_table():
    qi = jnp.arange(CHUNK, dtype=jnp.int32)[:, None]
    kj = jnp.arange(2 * CHUNK, dtype=jnp.int32)[None, :]
    dist = jnp.maximum(qi + CHUNK - kj, 0)
    max_exact = REL_BUCKETS // 2
    dist_f = jnp.maximum(dist, 1).astype(F32)
    large = max_exact + (jnp.log(dist_f / max_exact) / math.log(REL_MAX_DIST / max_exact)
                         * (REL_BUCKETS - max_exact)).astype(jnp.int32)
    large = jnp.minimum(large, REL_BUCKETS - 1)
    return jnp.where(dist < max_exact, dist, large)


def _pad_lanes(v):
    return jnp.pad(v, (0, LANE - v.shape[0])).reshape(1, LANE)


def kernel(x, w_in, norm_pre, norm_post, rel_bias, att_sinks, sg_ln_g, sg_ln_b, sg_w, sg_b, ssm_conv_w, ssm_conv_b, ssm_dt_bias, ssm_a_log, ssm_d, ssm_norm_g, w_br_att, w_br_sg, w_br_ssm, w_out):
    bsz, seq, d = x.shape
    depth = w_in.shape[0]
    assert d == D_MODEL and seq % (2 * CHUNK) == 0 and (bsz * seq) % (SG_PER_STEP * CHUNK) == 0
    nb = seq // CHUNK
    x2 = x.reshape(bsz * seq, d)
    bucket = _rel_bucket_table()
    w_in_t = jnp.swapaxes(w_in, 1, 2)
    for l in range(depth):
        acc = _att_branch(x2, norm_pre[l], w_in_t, w_br_att, l, rel_bias, att_sinks[l], bucket, nb)
        b_full = jnp.repeat(jnp.transpose(sg_b[l]), CHUNK, axis=1)
        acc = _sg_branch(x2, acc, norm_pre[l], w_in_t, w_br_sg, l, sg_ln_g[l], sg_ln_b[l], sg_w[l], b_full)
        cw, cb = ssm_conv_w[l], ssm_conv_b[l]
        x2 = _ssd_branch(x2, acc, norm_pre[l], w_in_t, w_br_ssm, w_out, l, norm_post[l],
                         cw[:, None, :SSM_WIDTH], cb[:SSM_WIDTH].reshape(1, -1),
                         cw[:, None, SSM_WIDTH:], cb[SSM_WIDTH:].reshape(1, -1),
                         _pad_lanes(ssm_dt_bias[l]), _pad_lanes(ssm_a_log[l]).reshape(LANE, 1),
                         jnp.repeat(ssm_d[l], SSM_HEAD_DIM).reshape(1, -1),
                         ssm_norm_g[l].reshape(1, -1), nb)
    return x2.reshape(bsz, seq, d)
```

```python
import functools
import math

import jax
import jax.numpy as jnp
from jax import lax
from jax.experimental import pallas as pl
from jax.experimental.pallas import tpu as pltpu

F32 = jnp.float32
BF16 = jnp.bfloat16

D_MODEL = 1024
ATT_HEADS = 16
ATT_KV_HEADS = 2
ATT_HEAD_DIM = 64
ATT_GROUP = ATT_HEADS // ATT_KV_HEADS
CHUNK = 128
REL_BUCKETS = 32
REL_MAX_DIST = 128
SG_GROUPS = 8
SG_WIDTH = 1024
SSM_WIDTH = 2048
SSM_HEAD_DIM = 64
SSM_HEADS = 32
SSM_GROUPS = 4
SSM_STATE = 128
SSM_HPG = SSM_HEADS // SSM_GROUPS
SSM_CONV = 4
SSM_BC = 2 * SSM_GROUPS * SSM_STATE
EPS = 1e-6
NEG = -1e30
LOG2E = 1.4426950408889634
LANE = 128

A_Q, A_K, A_V, A_ZA, A_G, A_COLS = 0, 1024, 1152, 1280, 2304, 3328
S_U, S_VS, S_ZS, S_G, S_COLS = 0, 1024, 2048, 3072, 4096
M_ZM, M_XS, M_BC, M_DT, M_G, M_COLS = 0, 2048, 4096, 5120, 5248, 6272
VMEM_LIMIT = 56 * 1024 * 1024
W_PAD_COLS = D_MODEL + LANE
S_MAIN_COLS = S_G + LANE
W_SG_COL0, W_SSM_COL0, W_GATE_COL0 = 2304, 5376, 10528
ATT_PER_STEP = SG_PER_STEP = SSD_PER_STEP = 4


def _cparams():
    return pltpu.CompilerParams(dimension_semantics=("arbitrary",), vmem_limit_bytes=VMEM_LIMIT)


def _sigmoid(x):
    return 0.5 * jnp.tanh(0.5 * x) + 0.5


def _silu(x):
    h = 0.5 * x
    return h * jnp.tanh(h) + h


def _softplus(x):
    return jnp.maximum(x, 0.0) + jnp.log1p(jnp.exp(-jnp.abs(x)))


def _dot(a, b):
    return jnp.dot(a, b, preferred_element_type=F32)


def _dot_nt(a, b):
    return lax.dot_general(a, b, (((1,), (1,)), ((), ())), preferred_element_type=F32)


def _rms_norm_to(x_ref, g_ref, h_ref):
    x = x_ref[...]
    ms = jnp.mean(x * x, axis=-1, keepdims=True)
    h_ref[...] = (x * lax.rsqrt(ms + EPS) * g_ref[...]).astype(BF16)


def _proj_cols(h_ref, w_ref, dst_ref, c0, c1, d0):
    dst_ref[:, d0:d0 + c1 - c0] = _dot(h_ref[...], w_ref[:, c0:c1])


def _proj_plan(parts, width=256):
    plan, d0 = [], 0
    for w_ref, n_cols in parts:
        for c0 in range(0, n_cols, width):
            c1 = min(c0 + width, n_cols)
            plan.append((w_ref, c0, c1, d0 + c0))
        d0 += n_cols
    return plan


def _proj_thunks(h_ref, dst_ref, plan):
    return [functools.partial(_proj_cols, h_ref, w_ref, dst_ref, c0, c1, d0) for w_ref, c0, c1, d0 in plan]


def _rms_proj(x_ref, g_ref, dst_ref, h_ref, plan):
    _rms_norm_to(x_ref, g_ref, h_ref)
    for thunk in _proj_thunks(h_ref, dst_ref, plan):
        thunk()


STAGE_ROWS, STAGE_COLS = 1024, 256


def _weight_jobs(src_hbm, layer, k_rows, col0, n_cols, dst_ref):
    jobs = []
    for r0 in range(0, k_rows, STAGE_ROWS):
        for c in range(0, n_cols, STAGE_COLS):
            w = min(STAGE_COLS, n_cols - c)
            jobs.append((src_hbm.at[layer, r0:r0 + STAGE_ROWS, col0 + c:col0 + c + w], dst_ref, r0, c, w))
    return jobs


def _weight_jobs_t(src_hbm, layer, row0, n_rows, dst_ref):
    jobs = []
    for c in range(0, n_rows, STAGE_COLS):
        w = min(STAGE_COLS, n_rows - c)
        jobs.append((src_hbm.at[layer, row0 + c:row0 + c + w, :], dst_ref, None, c, w))
    return jobs


def _load_weights(jobs, stage, stage_t, sem):
    def dma(k):
        src, _, r0, _, w = jobs[k]
        dst = stage_t.at[k % 2, 0:w, :] if r0 is None else stage.at[k % 2, :, 0:w]
        return pltpu.make_async_copy(src, dst, sem.at[k % 2])

    dma(0).start()
    for k, (_, dst, r0, d0, w) in enumerate(jobs):
        if k + 1 < len(jobs):
            dma(k + 1).start()
        dma(k).wait()
        if r0 is None:
            dst[:, d0:d0 + w] = stage_t[k % 2, 0:w, :].T.astype(BF16)
        else:
            dst[r0:r0 + STAGE_ROWS, d0:d0 + w] = stage[k % 2, :, 0:w].astype(BF16)


class _Spread:
    def __init__(self, thunks, n_ticks):
        self.thunks, self.n_ticks, self.ticks, self.done = thunks, n_ticks, 0, 0

    def tick(self):
        self.ticks += 1
        while self.done < len(self.thunks) and self.done * self.n_ticks < self.ticks * len(self.thunks):
            self.thunks[self.done]()
            self.done += 1

    def flush(self):
        self.ticks = self.n_ticks
        self.tick()


def _col_chunks(n_cols, n_chunks, width=256):
    tiles = -(-n_cols // width)
    per = [tiles // n_chunks + (1 if k < tiles % n_chunks else 0) for k in range(n_chunks)]
    out, c = [], 0
    for n_tiles in per:
        out.append((c, min(c + n_tiles * width, n_cols)))
        c = min(c + n_tiles * width, n_cols)
    return out


def _split3(a):
    hi = a.astype(BF16)
    r1 = a - hi.astype(F32)
    mid = r1.astype(BF16)
    lo = (r1 - mid.astype(F32)).astype(BF16)
    return hi, mid, lo


def _att_block(src, kv_prev, bias_ref, sink_ref, variant, tick=lambda: None):
    qi = lax.broadcasted_iota(jnp.int32, (CHUNK, CHUNK), 0)
    ci = lax.broadcasted_iota(jnp.int32, (CHUNK, CHUNK), 1)
    own = ci <= qi
    low = lax.broadcasted_iota(jnp.int32, (2 * CHUNK, LANE), 1) < ATT_HEAD_DIM
    low_q = lax.broadcasted_iota(jnp.int32, (CHUNK, LANE), 1) < ATT_HEAD_DIM
    scale = ATT_HEAD_DIM ** -0.5
    kcat = jnp.concatenate([src[:, A_K:A_K + LANE], kv_prev[:, 0:LANE]], axis=0)
    vcat = jnp.concatenate([src[:, A_V:A_V + LANE], kv_prev[:, LANE:2 * LANE]], axis=0)
    k_roll = pltpu.roll(kcat, ATT_HEAD_DIM, 1)
    v_roll = pltpu.roll(vcat, ATT_HEAD_DIM, 1)
    n_pair = ATT_GROUP // 2
    pairs = []
    for kvh in range(ATT_KV_HEADS):
        k_lo, k_hi = (kcat, k_roll) if kvh == 0 else (k_roll, kcat)
        v_lo, v_hi = (vcat, v_roll) if kvh == 0 else (v_roll, vcat)
        kz = (jnp.where(low, k_lo, 0.0).astype(BF16), jnp.where(low, 0.0, k_hi).astype(BF16))
        vz = (jnp.where(low, v_lo, 0.0).astype(BF16), jnp.where(low, 0.0, v_hi).astype(BF16))
        pr0 = kvh * n_pair
        q4 = jnp.concatenate([src[:, A_Q + pr * LANE:A_Q + (pr + 1) * LANE]
                              for pr in range(pr0, pr0 + n_pair)], axis=0)
        q4 = (q4 * scale).astype(BF16)
        lhs_cols = []
        recips = [[None, None] for _ in range(n_pair)]
        for par in range(2):
            tick()
            s2 = _dot_nt(q4, kz[par])
            rows = []
            for k in range(n_pair):
                h = 2 * (pr0 + k) + par
                s2k = s2[k * CHUNK:(k + 1) * CHUNK]
                s = jnp.where(own, s2k[:, 0:CHUNK], s2k[:, CHUNK:2 * CHUNK]) + bias_ref[variant, h]
                sink = sink_ref[h]
                m = jnp.maximum(jnp.max(s, axis=-1, keepdims=True), sink)
                p = jnp.exp(s - m)
                den = jnp.sum(p, axis=-1, keepdims=True) + jnp.exp(sink - m)
                recips[k][par] = 1.0 / den
                rows.append(jnp.concatenate([jnp.where(own, p, 0.0), jnp.where(own, 0.0, p)],
                                            axis=1).astype(BF16))
            lhs_cols.append(jnp.concatenate(rows, axis=0))
        tick()
        o4 = _dot(jnp.concatenate(lhs_cols, axis=1), jnp.concatenate(vz, axis=0))
        for k in range(n_pair):
            r_pair = jnp.where(low_q, jnp.broadcast_to(recips[k][0], (CHUNK, LANE)),
                               jnp.broadcast_to(recips[k][1], (CHUNK, LANE)))
            pairs.append(o4[k * CHUNK:(k + 1) * CHUNK] * r_pair)
    y = jnp.concatenate(pairs, axis=-1) * _silu(src[:, A_ZA:A_ZA + 1024])
    tick()
    tick()
    return y


def _att_kernel(rb_ref, sink_ref, bkt_ref, x0_ref, *refs, nb, layer):
    x_next, refs = refs[:ATT_PER_STEP], refs[ATT_PER_STEP:]
    (gpre_ref, win_hbm, wbr_hbm, o_ref, w_ref, wg_ref, wbr_ref, stage, stage_t, sem, slot_a, slot_b, h_ref,
     kv_prev, bias_ref, ybf_ref) = refs
    i = pl.program_id(0)
    plan = _proj_plan([(w_ref, A_G), (wg_ref, D_MODEL)])
    n_ticks = 2 * (ATT_KV_HEADS + 1) + 2

    @pl.when(i == 0)
    def _():
        qi = lax.broadcasted_iota(jnp.int32, (CHUNK, CHUNK), 0)
        ci = lax.broadcasted_iota(jnp.int32, (CHUNK, CHUNK), 1)
        own = ci <= qi
        bkt = jnp.where(own, bkt_ref[:, CHUNK:2 * CHUNK], bkt_ref[:, 0:CHUNK])
        for h in range(ATT_HEADS):
            acc = jnp.zeros((CHUNK, CHUNK), F32)
            for b in range(REL_BUCKETS):
                acc = jnp.where(bkt == b, rb_ref[b, h], acc)
            bias_ref[1, h] = acc
            bias_ref[0, h] = jnp.where(own, acc, NEG)
        kv_prev[...] = jnp.zeros_like(kv_prev)
        _load_weights(_weight_jobs_t(win_hbm, layer, 0, A_G, w_ref)
                      + _weight_jobs_t(win_hbm, layer, W_GATE_COL0, D_MODEL, wg_ref)
                      + _weight_jobs(wbr_hbm, layer, 1024, 0, D_MODEL, wbr_ref), stage, stage_t, sem)
        _rms_proj(x0_ref, gpre_ref, slot_a, h_ref, plan)

    def epilogue(src, rows):
        def branch(c0, c1):
            o_ref[rows, c0:c1] = (_sigmoid(src[:, A_G + c0:A_G + c1])
                                  * _dot(ybf_ref[...], wbr_ref[:, c0:c1]))
        return [functools.partial(branch, c0, c1) for c0, c1 in _col_chunks(D_MODEL, D_MODEL // 256)]

    def mix(src, variant, x_next_ref, dst, pending):
        _rms_norm_to(x_next_ref, gpre_ref, h_ref)
        spread = _Spread(pending + _proj_thunks(h_ref, dst, plan), n_ticks)
        y = _att_block(src, kv_prev, bias_ref, sink_ref, variant, tick=spread.tick)
        spread.flush()
        kv_prev[...] = src[:, A_K:A_K + 2 * LANE]
        ybf_ref[...] = y.astype(BF16)

    first = ((ATT_PER_STEP * i) % nb) == 0
    slots = (slot_a, slot_b)
    pending = []
    for k in range(ATT_PER_STEP):
        mix(slots[k % 2], jnp.where(first, 0, 1) if k == 0 else 1, x_next[k], slots[(k + 1) % 2], pending)
        pending = epilogue(slots[k % 2], slice(k * CHUNK, (k + 1) * CHUNK))
    for thunk in pending:
        thunk()


def _sg_block(src, lng_ref, lnb_ref, wt_ref, bfull_ref, tick=lambda: None):
    v = src[:, S_VS:S_VS + SG_WIDTH]
    mu = jnp.mean(v, axis=-1, keepdims=True)
    vc = v - mu
    var = jnp.mean(vc * vc, axis=-1, keepdims=True)
    vn = (vc * lax.rsqrt(var + EPS) * lng_ref[...] + lnb_ref[...]).astype(BF16)
    out = []
    for g in range(SG_GROUPS):
        tick()
        sl = slice(g * LANE, (g + 1) * LANE)
        mixed = _dot(wt_ref[g], vn[:, sl]) + bfull_ref[:, sl]
        out.append(src[:, S_U + g * LANE:S_U + (g + 1) * LANE] * mixed
                   * _silu(src[:, S_ZS + g * LANE:S_ZS + (g + 1) * LANE]))
    return jnp.concatenate(out, axis=-1)


def _sg_kernel(x0_ref, *refs, layer):
    x_next, refs = refs[:SG_PER_STEP], refs[SG_PER_STEP:]
    (acc_ref, gpre_ref, win_hbm, wbr_hbm, lng_ref, lnb_ref, ws_ref, bfull_ref, o_ref, w_ref, wg_ref, wbr_ref,
     stage, stage_t, sem, slot_a, slot_b, h_ref, wt_ref, ybf_ref) = refs
    i = pl.program_id(0)
    plan = _proj_plan([(w_ref, S_G), (wg_ref, D_MODEL)])

    @pl.when(i == 0)
    def _():
        ti = lax.broadcasted_iota(jnp.int32, (CHUNK, CHUNK), 0)
        si = lax.broadcasted_iota(jnp.int32, (CHUNK, CHUNK), 1)
        for g in range(SG_GROUPS):
            wt_ref[g] = jnp.where(si <= ti, ws_ref[g], 0.0).astype(BF16)
        _load_weights(_weight_jobs_t(win_hbm, layer, W_SG_COL0, S_G, w_ref)
                      + _weight_jobs_t(win_hbm, layer, W_GATE_COL0 + D_MODEL, D_MODEL, wg_ref)
                      + _weight_jobs(wbr_hbm, layer, SG_WIDTH, 0, D_MODEL, wbr_ref), stage, stage_t, sem)
        _rms_proj(x0_ref, gpre_ref, slot_a, h_ref, plan)

    def epilogue(src, rows):
        def branch(c0, c1):
            o_ref[rows, c0:c1] = (acc_ref[rows, c0:c1] + _sigmoid(src[:, S_G + c0:S_G + c1])
                                  * _dot(ybf_ref[...], wbr_ref[:, c0:c1]))
        return [functools.partial(branch, c0, c1) for c0, c1 in _col_chunks(D_MODEL, D_MODEL // 256)]

    def mix(src, x_next_ref, dst, pending):
        _rms_norm_to(x_next_ref, gpre_ref, h_ref)
        spread = _Spread(pending + _proj_thunks(h_ref, dst, plan), SG_GROUPS)
        y = _sg_block(src, lng_ref, lnb_ref, wt_ref, bfull_ref, tick=spread.tick)
        spread.flush()
        ybf_ref[...] = y.astype(BF16)

    slots = (slot_a, slot_b)
    pending = []
    for k in range(SG_PER_STEP):
        mix(slots[k % 2], x_next[k], slots[(k + 1) % 2], pending)
        pending = epilogue(slots[k % 2], slice(k * CHUNK, (k + 1) * CHUNK))
    for thunk in pending:
        thunk()


def _ssd_block(src, first, cwx_ref, cbx_ref, cwb_ref, cbb_ref, dtb_ref, alog_ref, drow_ref, ng_ref,
               ext_x, ext_b, h_ref, xc_ref, y_ref, tick=lambda: None):
    n_xt, n_bt = SSM_WIDTH // LANE, SSM_BC // LANE
    tail = slice(CHUNK, CHUNK + 8)

    if first is not None:
        @pl.when(first)
        def _():
            h_ref[...] = jnp.zeros_like(h_ref)

    def stage(ext, col0, n_tiles):
        for t in range(n_tiles):
            prev = ext[t, tail, :]
            ext[t, 0:8, :] = prev if first is None else jnp.where(first, 0.0, prev)
            ext[t, 8:8 + CHUNK, :] = src[:, col0 + t * LANE:col0 + (t + 1) * LANE]

    def conv_tile(ext, w_ref, b_ref, t):
        sl = slice(t * LANE, (t + 1) * LANE)
        acc = b_ref[:, sl]
        for k in range(SSM_CONV):
            acc = acc + w_ref[k, :, sl] * ext[t, pl.ds(8 - (SSM_CONV - 1) + k, CHUNK), :]
        return _silu(acc)

    stage(ext_x, M_XS, n_xt)
    stage(ext_b, M_BC, n_bt)
    for t in range(n_xt):
        xc_ref[:, t * LANE:(t + 1) * LANE] = conv_tile(ext_x, cwx_ref, cbx_ref, t)
    bcv = []
    for t in range(n_bt):
        bcv.append(conv_tile(ext_b, cwb_ref, cbb_ref, t))

    li = lax.broadcasted_iota(jnp.int32, (CHUNK, CHUNK), 0)
    si = lax.broadcasted_iota(jnp.int32, (CHUNK, CHUNK), 1)
    causal = si <= li
    x_dt_t = (src[:, M_DT:M_DT + LANE] + dtb_ref[...]).T[0:SSM_HEADS, :]
    dt_t = _softplus(x_dt_t)
    a_dt_t = dt_t * (-jnp.exp(alog_ref[0:SSM_HEADS, :]))
    upper = jnp.where(li <= si, 1.0, 0.0).astype(BF16)
    parts = _dot(jnp.concatenate(_split3(a_dt_t), axis=0), upper)
    a_cs_t = (parts[0:SSM_HEADS] + parts[SSM_HEADS:2 * SSM_HEADS]
              + parts[2 * SSM_HEADS:3 * SSM_HEADS])
    a2_t = a_cs_t * LOG2E
    a2_last = a2_t[:, CHUNK - 1:CHUNK]
    w_t = dt_t * jnp.exp2(a2_last - a2_t)
    cd_t = jnp.exp2(a2_last)
    a2 = jnp.concatenate([a2_t, jnp.zeros((CHUNK - SSM_HEADS, CHUNK), F32)], axis=0).T
    e2 = jnp.exp2(a2)
    low = lax.broadcasted_iota(jnp.int32, (CHUNK, LANE), 1) < SSM_HEAD_DIM
    low_row = low[0:1]
    keep_lo = jnp.where(low, 1.0, 0.0).astype(BF16)
    keep_hi = jnp.where(low, 0.0, 1.0).astype(BF16)
    gw = SSM_WIDTH // SSM_GROUPS

    for g in range(SSM_GROUPS):
        b_g, c_g = bcv[g], bcv[SSM_GROUPS + g]
        cb = jnp.where(causal, _dot_nt(c_g.astype(BF16), b_g.astype(BF16)), 0.0)
        b_gt = b_g.T
        y_off = _dot(c_g.astype(BF16), h_ref[:, g * gw:(g + 1) * gw].astype(BF16))
        for pr in range(g * SSM_HPG // 2, (g + 1) * SSM_HPG // 2):
            tick()
            sl = slice(pr * LANE, (pr + 1) * LANE)
            x_pair = xc_ref[:, sl].astype(BF16)
            x_bd = jnp.concatenate([x_pair * keep_lo, x_pair * keep_hi], axis=0)
            m_cols, bw_cols, e_cols = [], [], []
            for j in (2 * pr, 2 * pr + 1):
                a_col = jnp.broadcast_to(a2[:, j:j + 1], (CHUNK, CHUNK))
                decay = jnp.exp2(jnp.minimum(a_col - a2_t[j:j + 1, :], 0.0))
                m_cols.append((cb * decay * dt_t[j:j + 1, :]).astype(BF16))
                bw_cols.append((b_gt * w_t[j:j + 1, :]).astype(BF16))
                e_cols.append(jnp.broadcast_to(e2[:, j:j + 1], (CHUNK, LANE)))
            y_diag = _dot(jnp.concatenate(m_cols, axis=1), x_bd)
            up = _dot(jnp.concatenate(bw_cols, axis=1), x_bd)
            k = pr - g * SSM_HPG // 2
            y_ref[:, sl] = y_diag + jnp.where(low, e_cols[0], e_cols[1]) * y_off[:, k * LANE:(k + 1) * LANE]
            cd_pair = jnp.where(low_row, cd_t[2 * pr:2 * pr + 1, :], cd_t[2 * pr + 1:2 * pr + 2, :])
            h_ref[:, sl] = h_ref[:, sl] * cd_pair + up

    out = []
    for g in range(SSM_GROUPS):
        tick()
        sl = slice(g * gw, (g + 1) * gw)
        yg = (y_ref[:, sl] + drow_ref[:, sl] * xc_ref[:, sl]) * _silu(src[:, M_ZM + g * gw:M_ZM + (g + 1) * gw])
        yg = yg * lax.rsqrt(jnp.mean(yg * yg, axis=-1, keepdims=True) + EPS)
        out.append(yg * ng_ref[:, sl])
    return jnp.concatenate(out, axis=-1)


def _ssd_kernel(x0_ref, *refs, nb, layer):
    x_next, refs = refs[:SSD_PER_STEP], refs[SSD_PER_STEP:]
    (xres_ref, acc_ref, gpre_ref, win_hbm, wbr_hbm, wout_hbm, gpost_ref, cwx_ref, cbx_ref, cwb_ref, cbb_ref,
     dtb_ref, alog_ref, drow_ref, ng_ref, o_ref, w_ref, wg_ref, wbr_ref, wout_ref, stage, stage_t, sem,
     slot_a, slot_b, hn_ref, ext_x, ext_b, h_ref, xc_ref, y_ref, ybf_ref, mg_ref, out_ref) = refs
    i = pl.program_id(0)
    plan = _proj_plan([(w_ref, M_G), (wg_ref, D_MODEL)])

    @pl.when(i == 0)
    def _():
        ext_x[...] = jnp.zeros_like(ext_x)
        ext_b[...] = jnp.zeros_like(ext_b)
        _load_weights(_weight_jobs_t(win_hbm, layer, W_SSM_COL0, M_G, w_ref)
                      + _weight_jobs_t(win_hbm, layer, W_GATE_COL0 + 2 * D_MODEL, D_MODEL, wg_ref)
                      + _weight_jobs(wbr_hbm, layer, SSM_WIDTH, 0, D_MODEL, wbr_ref)
                      + _weight_jobs(wout_hbm, layer, D_MODEL, 0, D_MODEL, wout_ref), stage, stage_t, sem)
        _rms_proj(x0_ref, gpre_ref, slot_a, hn_ref, plan)

    n_ticks = SSM_HEADS // 2 + SSM_GROUPS
    out_cols = _col_chunks(D_MODEL, D_MODEL // 256)

    def epilogue(src, rows):
        def branch(c0, c1):
            gate = _sigmoid(src[:, M_G + c0:M_G + c1])
            mg_ref[:, c0:c1] = (acc_ref[rows, c0:c1]
                                + gate * _dot(ybf_ref[...], wbr_ref[:, c0:c1])).astype(BF16)

        def outproj(c0, c1):
            out_ref[:, c0:c1] = _dot(mg_ref[...], wout_ref[:, c0:c1])

        def finish():
            out = out_ref[...]
            ms = jnp.mean(out * out, axis=-1, keepdims=True)
            o_ref[rows, :] = xres_ref[rows, :] + out * lax.rsqrt(ms + EPS) * gpost_ref[...]

        return ([functools.partial(branch, c0, c1) for c0, c1 in out_cols]
                + [functools.partial(outproj, c0, c1) for c0, c1 in out_cols] + [finish])

    def mix(src, first, x_next_ref, dst, pending):
        _rms_norm_to(x_next_ref, gpre_ref, hn_ref)
        spread = _Spread(pending + _proj_thunks(hn_ref, dst, plan), n_ticks)
        y = _ssd_block(src, first, cwx_ref, cbx_ref, cwb_ref, cbb_ref, dtb_ref, alog_ref, drow_ref, ng_ref,
                       ext_x, ext_b, h_ref, xc_ref, y_ref, tick=spread.tick)
        spread.flush()
        ybf_ref[...] = y.astype(BF16)

    first = ((SSD_PER_STEP * i) % nb) == 0
    slots = (slot_a, slot_b)
    pending = []
    for k in range(SSD_PER_STEP):
        mix(slots[k % 2], first if k == 0 else None, x_next[k], slots[(k + 1) % 2], pending)
        pending = epilogue(slots[k % 2], slice(k * CHUNK, (k + 1) * CHUNK))
    for thunk in pending:
        thunk()


def _x_specs(n_blk, per_step=2):
    nxt = lambda k: pl.BlockSpec((CHUNK, D_MODEL), lambda i: (jnp.minimum(per_step * i + k, n_blk - 1), 0))
    return [pl.BlockSpec((CHUNK, D_MODEL), lambda i: (0, 0))] + [nxt(k) for k in range(1, per_step + 1)]


def _const(shape):
    return pl.BlockSpec(shape, lambda i: (0,) * len(shape))


def _rows2(per_step=2):
    return pl.BlockSpec((per_step * CHUNK, D_MODEL), lambda i: (i, 0))


def _weight_scratch(shapes):
    return ([pltpu.VMEM(s, BF16) for s in shapes]
            + [pltpu.VMEM((2, STAGE_ROWS, STAGE_COLS), F32), pltpu.VMEM((2, STAGE_COLS, STAGE_ROWS), F32),
               pltpu.SemaphoreType.DMA((2,))])


def _any():
    return pl.BlockSpec(memory_space=pl.ANY)


def _att_branch(x2, g_pre, w_in_t, w_br, layer, rel_bias, sinks, bucket, nb):
    t = x2.shape[0]
    n_blk = t // CHUNK
    smem = pl.BlockSpec(memory_space=pltpu.SMEM)
    return pl.pallas_call(
        functools.partial(_att_kernel, nb=nb, layer=layer),
        grid=(n_blk // ATT_PER_STEP,),
        in_specs=[smem, smem, _const((CHUNK, 2 * CHUNK))] + _x_specs(n_blk, ATT_PER_STEP)
                 + [_const((1, D_MODEL)), _any(), _any()],
        out_specs=_rows2(ATT_PER_STEP),
        out_shape=jax.ShapeDtypeStruct((t, D_MODEL), F32),
        scratch_shapes=_weight_scratch([(D_MODEL, A_G), (D_MODEL, W_PAD_COLS), (1024, W_PAD_COLS)])
                       + [pltpu.VMEM((CHUNK, A_COLS), F32), pltpu.VMEM((CHUNK, A_COLS), F32),
                        pltpu.VMEM((CHUNK, D_MODEL), BF16),
                        pltpu.VMEM((CHUNK, 2 * LANE), F32),
                        pltpu.VMEM((2, ATT_HEADS, CHUNK, CHUNK), F32),
                        pltpu.VMEM((CHUNK, D_MODEL), BF16)],
        compiler_params=_cparams(),
        name="att",
    )(rel_bias, sinks, bucket, *([x2] * (1 + ATT_PER_STEP)), g_pre.reshape(1, -1), w_in_t, w_br)


def _sg_branch(x2, acc, g_pre, w_in_t, w_br, layer, ln_g, ln_b, w_s, b_full):
    t = x2.shape[0]
    n_blk = t // CHUNK
    return pl.pallas_call(
        functools.partial(_sg_kernel, layer=layer),
        grid=(n_blk // SG_PER_STEP,),
        in_specs=_x_specs(n_blk, SG_PER_STEP) + [_rows2(SG_PER_STEP), _const((1, D_MODEL)), _any(), _any(),
                                    _const((1, SG_WIDTH)), _const((1, SG_WIDTH)),
                                    _const((SG_GROUPS, CHUNK, CHUNK)), _const((CHUNK, SG_WIDTH))],
        out_specs=_rows2(SG_PER_STEP),
        out_shape=jax.ShapeDtypeStruct((t, D_MODEL), F32),
        scratch_shapes=_weight_scratch([(D_MODEL, S_MAIN_COLS), (D_MODEL, W_PAD_COLS), (SG_WIDTH, W_PAD_COLS)])
                       + [pltpu.VMEM((CHUNK, S_COLS), F32), pltpu.VMEM((CHUNK, S_COLS), F32),
                        pltpu.VMEM((CHUNK, D_MODEL), BF16),
                        pltpu.VMEM((SG_GROUPS, CHUNK, CHUNK), BF16),
                        pltpu.VMEM((CHUNK, SG_WIDTH), BF16)],
        compiler_params=_cparams(),
        name="sg",
    )(*([x2] * (1 + SG_PER_STEP)), acc, g_pre.reshape(1, -1), w_in_t, w_br, ln_g.reshape(1, -1), ln_b.reshape(1, -1),
      w_s, b_full)


def _ssd_branch(x2, acc, g_pre, w_in_t, w_br, w_out, layer, g_post, cwx, cbx, cwb, cbb, dtb, alog, drow,
                ng, nb):
    t = x2.shape[0]
    n_blk = t // CHUNK
    return pl.pallas_call(
        functools.partial(_ssd_kernel, nb=nb, layer=layer),
        grid=(n_blk // SSD_PER_STEP,),
        in_specs=_x_specs(n_blk, SSD_PER_STEP) + [_rows2(SSD_PER_STEP), _rows2(SSD_PER_STEP), _const((1, D_MODEL)),
                                                  _any(), _any(), _any(),
                                    _const((1, D_MODEL)),
                                    _const((SSM_CONV, 1, SSM_WIDTH)), _const((1, SSM_WIDTH)),
                                    _const((SSM_CONV, 1, SSM_BC)), _const((1, SSM_BC)),
                                    _const((1, LANE)), _const((LANE, 1)),
                                    _const((1, SSM_WIDTH)), _const((1, SSM_WIDTH))],
        out_specs=_rows2(SSD_PER_STEP),
        out_shape=jax.ShapeDtypeStruct((t, D_MODEL), F32),
        scratch_shapes=_weight_scratch([(D_MODEL, M_G), (D_MODEL, W_PAD_COLS), (SSM_WIDTH, W_PAD_COLS),
                                        (D_MODEL, W_PAD_COLS)])
                       + [pltpu.VMEM((CHUNK, M_COLS), F32), pltpu.VMEM((CHUNK, M_COLS), F32),
                        pltpu.VMEM((CHUNK, D_MODEL), BF16),
                        pltpu.VMEM((SSM_WIDTH // LANE, 8 + CHUNK, LANE), F32),
                        pltpu.VMEM((SSM_BC // LANE, 8 + CHUNK, LANE), F32),
                        pltpu.VMEM((SSM_STATE, SSM_WIDTH), F32),
                        pltpu.VMEM((CHUNK, SSM_WIDTH), F32),
                        pltpu.VMEM((CHUNK, SSM_WIDTH), F32),
                        pltpu.VMEM((CHUNK, SSM_WIDTH), BF16),
                        pltpu.VMEM((CHUNK, D_MODEL), BF16),
                        pltpu.VMEM((CHUNK, D_MODEL), F32)],
        compiler_params=_cparams(),
        name="ssd",
    )(*([x2] * (2 + SSD_PER_STEP)), acc, g_pre.reshape(1, -1), w_in_t, w_br, w_out, g_post.reshape(1, -1),
      cwx, cbx, cwb, cbb, dtb, alog, drow, ng)


def _rel_bucket_table():
    qi = jnp.arange(CHUNK, dtype=jnp.int32)[:, None]
    kj = jnp.arange(2 * CHUNK, dtype=jnp.int32)[None, :]
    dist = jnp.maximum(qi + CHUNK - kj, 0)
    max_exact = REL_BUCKETS // 2
    dist_f = jnp.maximum(dist, 1).astype(F32)
    large = max_exact + (jnp.log(dist_f / max_exact) / math.log(REL_MAX_DIST / max_exact)
                         * (REL_BUCKETS - max_exact)).astype(jnp.int32)
    large = jnp.minimum(large, REL_BUCKETS - 1)
    return jnp.where(dist < max_exact, dist, large)


def _pad_lanes(v):
    return jnp.pad(v, (0, LANE - v.shape[0])).reshape(1, LANE)


def kernel(x, w_in, norm_pre, norm_post, rel_bias, att_sinks, sg_ln_g, sg_ln_b, sg_w, sg_b, ssm_conv_w, ssm_conv_b, ssm_dt_bias, ssm_a_log, ssm_d, ssm_norm_g, w_br_att, w_br_sg, w_br_ssm, w_out):
    bsz, seq, d = x.shape
    depth = w_in.shape[0]
    assert d == D_MODEL and seq % (2 * CHUNK) == 0 and seq % (SG_PER_STEP * CHUNK) == 0
    nb = seq // CHUNK
    x2 = x.reshape(bsz * seq, d)
    bucket = _rel_bucket_table()
    w_in_t = jnp.swapaxes(w_in, 1, 2)
    for l in range(depth):
        acc = _att_branch(x2, norm_pre[l], w_in_t, w_br_att, l, rel_bias, att_sinks[l], bucket, nb)
        b_full = jnp.repeat(jnp.transpose(sg_b[l]), CHUNK, axis=1)
        acc = _sg_branch(x2, acc, norm_pre[l], w_in_t, w_br_sg, l, sg_ln_g[l], sg_ln_b[l], sg_w[l], b_full)
        cw, cb = ssm_conv_w[l], ssm_conv_b[l]
        x2 = _ssd_branch(x2, acc, norm_pre[l], w_in_t, w_br_ssm, w_out, l, norm_post[l],
                         cw[:, None, :SSM_WIDTH], cb[:SSM_WIDTH].reshape(1, -1),
                         cw[:, None, SSM_WIDTH:], cb[SSM_WIDTH:].reshape(1, -1),
                         _pad_lanes(ssm_dt_bias[l]), _pad_lanes(ssm_a_log[l]).reshape(LANE, 1),
                         jnp.repeat(ssm_d[l], SSM_HEAD_DIM).reshape(1, -1),
                         ssm_norm_g[l].reshape(1, -1), nb)
    return x2.reshape(bsz, seq, d)
```

```python
import functools
import math

import jax
import jax.numpy as jnp
from jax import lax
from jax.experimental import pallas as pl
from jax.experimental.pallas import tpu as pltpu

F32 = jnp.float32
BF16 = jnp.bfloat16

D_MODEL = 1024
ATT_HEADS = 16
ATT_KV_HEADS = 2
ATT_HEAD_DIM = 64
ATT_WIDTH = ATT_HEADS * ATT_HEAD_DIM
ATT_GROUP = ATT_HEADS // ATT_KV_HEADS
CHUNK = 128
REL_BUCKETS = 32
REL_MAX_DIST = 128
SG_GROUPS = 8
SG_WIDTH = 1024
SSM_WIDTH = 2048
SSM_HEAD_DIM = 64
SSM_HEADS = 32
SSM_GROUPS = 4
SSM_STATE = 128
SSM_HPG = SSM_HEADS // SSM_GROUPS
SSM_CONV = 4
SSM_BC = 2 * SSM_GROUPS * SSM_STATE
EPS = 1e-6
NEG = -1e30
LOG2E = 1.4426950408889634
LANE = 128

A_Q, A_K, A_V, A_ZA, A_G, A_COLS = 0, 1024, 1152, 1280, 2304, 3328
S_U, S_VS, S_ZS, S_G, S_COLS = 0, 1024, 2048, 3072, 4096
M_ZM, M_XS, M_BC, M_DT, M_G, M_COLS = 0, 2048, 4096, 5120, 5248, 6272
VMEM_LIMIT = 56 * 1024 * 1024
W_PAD_COLS = D_MODEL + LANE
S_MAIN_COLS = S_G + LANE
W_SG_COL0, W_SSM_COL0, W_GATE_COL0 = 2304, 5376, 10528
ATT_PER_STEP = SG_PER_STEP = 4


def _cparams():
    return pltpu.CompilerParams(dimension_semantics=("arbitrary",), vmem_limit_bytes=VMEM_LIMIT)


def _sigmoid(x):
    return 0.5 * jnp.tanh(0.5 * x) + 0.5


def _silu(x):
    h = 0.5 * x
    return h * jnp.tanh(h) + h


def _softplus(x):
    return jnp.maximum(x, 0.0) + jnp.log1p(jnp.exp(-jnp.abs(x)))


def _dot(a, b):
    return jnp.dot(a, b, preferred_element_type=F32)


def _dot_nt(a, b):
    return lax.dot_general(a, b, (((1,), (1,)), ((), ())), preferred_element_type=F32)


def _rms_norm_to(x_ref, g_ref, h_ref):
    x = x_ref[...]
    ms = jnp.mean(x * x, axis=-1, keepdims=True)
    h_ref[...] = (x * lax.rsqrt(ms + EPS) * g_ref[...]).astype(BF16)


def _proj_cols(h_ref, w_ref, dst_ref, c0, c1, d0):
    dst_ref[:, d0:d0 + c1 - c0] = _dot(h_ref[...], w_ref[:, c0:c1])


def _proj_plan(parts, width=256):
    plan, d0 = [], 0
    for w_ref, n_cols in parts:
        for c0 in range(0, n_cols, width):
            c1 = min(c0 + width, n_cols)
            plan.append((w_ref, c0, c1, d0 + c0))
        d0 += n_cols
    return plan


def _proj_thunks(h_ref, dst_ref, plan):
    return [functools.partial(_proj_cols, h_ref, w_ref, dst_ref, c0, c1, d0) for w_ref, c0, c1, d0 in plan]


def _rms_proj(x_ref, g_ref, dst_ref, h_ref, plan):
    _rms_norm_to(x_ref, g_ref, h_ref)
    for thunk in _proj_thunks(h_ref, dst_ref, plan):
        thunk()


STAGE_ROWS, STAGE_COLS = 1024, 512


def _weight_jobs(src_hbm, layer, k_rows, col0, n_cols, dst_ref):
    jobs = []
    for r0 in range(0, k_rows, STAGE_ROWS):
        for c in range(0, n_cols, STAGE_COLS):
            w = min(STAGE_COLS, n_cols - c)
            jobs.append((src_hbm.at[layer, r0:r0 + STAGE_ROWS, col0 + c:col0 + c + w], dst_ref, r0, c, w))
    return jobs


def _weight_jobs_t(src_hbm, layer, row0, n_rows, dst_ref):
    jobs = []
    for c in range(0, n_rows, STAGE_COLS):
        w = min(STAGE_COLS, n_rows - c)
        jobs.append((src_hbm.at[layer, row0 + c:row0 + c + w, :], dst_ref, None, c, w))
    return jobs


def _load_weights(jobs, stage, stage_t, sem):
    def dma(k):
        src, _, r0, _, w = jobs[k]
        dst = stage_t.at[k % 2, 0:w, :] if r0 is None else stage.at[k % 2, :, 0:w]
        return pltpu.make_async_copy(src, dst, sem.at[k % 2])

    dma(0).start()
    for k, (_, dst, r0, d0, w) in enumerate(jobs):
        if k + 1 < len(jobs):
            dma(k + 1).start()
        dma(k).wait()
        if r0 is None:
            dst[:, d0:d0 + w] = stage_t[k % 2, 0:w, :].T.astype(BF16)
        else:
            dst[r0:r0 + STAGE_ROWS, d0:d0 + w] = stage[k % 2, :, 0:w].astype(BF16)


class _Spread:
    def __init__(self, thunks, n_ticks):
        self.thunks, self.n_ticks, self.ticks, self.done = thunks, n_ticks, 0, 0

    def tick(self):
        self.ticks += 1
        while self.done < len(self.thunks) and self.done * self.n_ticks < self.ticks * len(self.thunks):
            self.thunks[self.done]()
            self.done += 1

    def flush(self):
        self.ticks = self.n_ticks
        self.tick()


def _col_chunks(n_cols, n_chunks, width=256):
    tiles = -(-n_cols // width)
    per = [tiles // n_chunks + (1 if k < tiles % n_chunks else 0) for k in range(n_chunks)]
    out, c = [], 0
    for n_tiles in per:
        out.append((c, min(c + n_tiles * width, n_cols)))
        c = min(c + n_tiles * width, n_cols)
    return out


def _split3(a):
    hi = a.astype(BF16)
    r1 = a - hi.astype(F32)
    mid = r1.astype(BF16)
    lo = (r1 - mid.astype(F32)).astype(BF16)
    return hi, mid, lo


def _att_block(src, kv_prev, bias_ref, sink_ref, variant, tick=lambda: None):
    qi = lax.broadcasted_iota(jnp.int32, (CHUNK, CHUNK), 0)
    ci = lax.broadcasted_iota(jnp.int32, (CHUNK, CHUNK), 1)
    own = ci <= qi
    low = lax.broadcasted_iota(jnp.int32, (2 * CHUNK, LANE), 1) < ATT_HEAD_DIM
    low_q = lax.broadcasted_iota(jnp.int32, (CHUNK, LANE), 1) < ATT_HEAD_DIM
    scale = ATT_HEAD_DIM ** -0.5
    kcat = jnp.concatenate([src[:, A_K:A_K + LANE], kv_prev[:, 0:LANE]], axis=0)
    vcat = jnp.concatenate([src[:, A_V:A_V + LANE], kv_prev[:, LANE:2 * LANE]], axis=0)
    k_roll = pltpu.roll(kcat, ATT_HEAD_DIM, 1)
    v_roll = pltpu.roll(vcat, ATT_HEAD_DIM, 1)
    n_pair = ATT_GROUP // 2
    pairs = []
    for kvh in range(ATT_KV_HEADS):
        k_lo, k_hi = (kcat, k_roll) if kvh == 0 else (k_roll, kcat)
        v_lo, v_hi = (vcat, v_roll) if kvh == 0 else (v_roll, vcat)
        kz = (jnp.where(low, k_lo, 0.0).astype(BF16), jnp.where(low, 0.0, k_hi).astype(BF16))
        vz = (jnp.where(low, v_lo, 0.0).astype(BF16), jnp.where(low, 0.0, v_hi).astype(BF16))
        pr0 = kvh * n_pair
        q4 = jnp.concatenate([src[:, A_Q + pr * LANE:A_Q + (pr + 1) * LANE]
                              for pr in range(pr0, pr0 + n_pair)], axis=0)
        q4 = (q4 * scale).astype(BF16)
        lhs_cols = []
        recips = [[None, None] for _ in range(n_pair)]
        for par in range(2):
            tick()
            s2 = _dot_nt(q4, kz[par])
            rows = []
            for k in range(n_pair):
                h = 2 * (pr0 + k) + par
                s2k = s2[k * CHUNK:(k + 1) * CHUNK]
                s = jnp.where(own, s2k[:, 0:CHUNK], s2k[:, CHUNK:2 * CHUNK]) + bias_ref[variant, h]
                sink = sink_ref[h]
                m = jnp.maximum(jnp.max(s, axis=-1, keepdims=True), sink)
                p = jnp.exp(s - m)
                den = jnp.sum(p, axis=-1, keepdims=True) + jnp.exp(sink - m)
                recips[k][par] = 1.0 / den
                rows.append(jnp.concatenate([jnp.where(own, p, 0.0), jnp.where(own, 0.0, p)],
                                            axis=1).astype(BF16))
            lhs_cols.append(jnp.concatenate(rows, axis=0))
        tick()
        o4 = _dot(jnp.concatenate(lhs_cols, axis=1), jnp.concatenate(vz, axis=0))
        for k in range(n_pair):
            r_pair = jnp.where(low_q, jnp.broadcast_to(recips[k][0], (CHUNK, LANE)),
                               jnp.broadcast_to(recips[k][1], (CHUNK, LANE)))
            pairs.append(o4[k * CHUNK:(k + 1) * CHUNK] * r_pair)
    y = jnp.concatenate(pairs, axis=-1) * _silu(src[:, A_ZA:A_ZA + ATT_WIDTH])
    tick()
    tick()
    return y


def _att_kernel(rb_ref, sink_ref, bkt_ref, x0_ref, *refs, nb, layer):
    x_next, refs = refs[:ATT_PER_STEP], refs[ATT_PER_STEP:]
    (gpre_ref, win_hbm, wbr_hbm, o_ref, w_ref, wg_ref, wbr_ref, stage, stage_t, sem, slot_a, slot_b, h_ref,
     kv_prev, bias_ref, ybf_ref) = refs
    i = pl.program_id(0)
    plan = _proj_plan([(w_ref, A_G), (wg_ref, D_MODEL)])
    n_ticks = 2 * (ATT_KV_HEADS + 1) + 2

    @pl.when(i == 0)
    def _():
        qi = lax.broadcasted_iota(jnp.int32, (CHUNK, CHUNK), 0)
        ci = lax.broadcasted_iota(jnp.int32, (CHUNK, CHUNK), 1)
        own = ci <= qi
        bkt = jnp.where(own, bkt_ref[:, CHUNK:2 * CHUNK], bkt_ref[:, 0:CHUNK])
        for h in range(ATT_HEADS):
            acc = jnp.zeros((CHUNK, CHUNK), F32)
            for b in range(REL_BUCKETS):
                acc = jnp.where(bkt == b, rb_ref[b, h], acc)
            bias_ref[1, h] = acc
            bias_ref[0, h] = jnp.where(own, acc, NEG)
        kv_prev[...] = jnp.zeros_like(kv_prev)
        _load_weights(_weight_jobs_t(win_hbm, layer, 0, A_G, w_ref)
                      + _weight_jobs_t(win_hbm, layer, W_GATE_COL0, D_MODEL, wg_ref)
                      + _weight_jobs(wbr_hbm, layer, ATT_WIDTH, 0, D_MODEL, wbr_ref), stage, stage_t, sem)
        _rms_proj(x0_ref, gpre_ref, slot_a, h_ref, plan)

    def epilogue(src, rows):
        def branch(c0, c1):
            o_ref[rows, c0:c1] = (_sigmoid(src[:, A_G + c0:A_G + c1])
                                  * _dot(ybf_ref[...], wbr_ref[:, c0:c1]))
        return [functools.partial(branch, c0, c1) for c0, c1 in _col_chunks(D_MODEL, D_MODEL // 256)]

    def mix(src, variant, x_next_ref, dst, pending):
        _rms_norm_to(x_next_ref, gpre_ref, h_ref)
        spread = _Spread(pending + _proj_thunks(h_ref, dst, plan), n_ticks)
        y = _att_block(src, kv_prev, bias_ref, sink_ref, variant, tick=spread.tick)
        spread.flush()
        kv_prev[...] = src[:, A_K:A_K + 2 * LANE]
        ybf_ref[...] = y.astype(BF16)

    first = ((ATT_PER_STEP * i) % nb) == 0
    slots = (slot_a, slot_b)
    pending = []
    for k in range(ATT_PER_STEP):
        mix(slots[k % 2], jnp.where(first, 0, 1) if k == 0 else 1, x_next[k], slots[(k + 1) % 2], pending)
        pending = epilogue(slots[k % 2], slice(k * CHUNK, (k + 1) * CHUNK))
    for thunk in pending:
        thunk()


def _sg_block(src, lng_ref, lnb_ref, wt_ref, bfull_ref, tick=lambda: None):
    v = src[:, S_VS:S_VS + SG_WIDTH]
    mu = jnp.mean(v, axis=-1, keepdims=True)
    vc = v - mu
    var = jnp.mean(vc * vc, axis=-1, keepdims=True)
    vn = (vc * lax.rsqrt(var + EPS) * lng_ref[...] + lnb_ref[...]).astype(BF16)
    out = []
    for g in range(SG_GROUPS):
        tick()
        sl = slice(g * LANE, (g + 1) * LANE)
        mixed = _dot(wt_ref[g], vn[:, sl]) + bfull_ref[:, sl]
        out.append(src[:, S_U + g * LANE:S_U + (g + 1) * LANE] * mixed
                   * _silu(src[:, S_ZS + g * LANE:S_ZS + (g + 1) * LANE]))
    return jnp.concatenate(out, axis=-1)


def _sg_kernel(x0_ref, *refs, layer):
    x_next, refs = refs[:SG_PER_STEP], refs[SG_PER_STEP:]
    (acc_ref, gpre_ref, win_hbm, wbr_hbm, lng_ref, lnb_ref, ws_ref, bfull_ref, o_ref, w_ref, wg_ref, wbr_ref,
     stage, stage_t, sem, slot_a, slot_b, h_ref, wt_ref, ybf_ref) = refs
    i = pl.program_id(0)
    plan = _proj_plan([(w_ref, S_G), (wg_ref, D_MODEL)])

    @pl.when(i == 0)
    def _():
        ti = lax.broadcasted_iota(jnp.int32, (CHUNK, CHUNK), 0)
        si = lax.broadcasted_iota(jnp.int32, (CHUNK, CHUNK), 1)
        for g in range(SG_GROUPS):
            wt_ref[g] = jnp.where(si <= ti, ws_ref[g], 0.0).astype(BF16)
        _load_weights(_weight_jobs_t(win_hbm, layer, W_SG_COL0, S_G, w_ref)
                      + _weight_jobs_t(win_hbm, layer, W_GATE_COL0 + D_MODEL, D_MODEL, wg_ref)
                      + _weight_jobs(wbr_hbm, layer, SG_WIDTH, 0, D_MODEL, wbr_ref), stage, stage_t, sem)
        _rms_proj(x0_ref, gpre_ref, slot_a, h_ref, plan)

    def epilogue(src, rows):
        def branch(c0, c1):
            o_ref[rows, c0:c1] = (acc_ref[rows, c0:c1] + _sigmoid(src[:, S_G + c0:S_G + c1])
                                  * _dot(ybf_ref[...], wbr_ref[:, c0:c1]))
        return [functools.partial(branch, c0, c1) for c0, c1 in _col_chunks(D_MODEL, D_MODEL // 256)]

    def mix(src, x_next_ref, dst, pending):
        _rms_norm_to(x_next_ref, gpre_ref, h_ref)
        spread = _Spread(pending + _proj_thunks(h_ref, dst, plan), SG_GROUPS)
        y = _sg_block(src, lng_ref, lnb_ref, wt_ref, bfull_ref, tick=spread.tick)
        spread.flush()
        ybf_ref[...] = y.astype(BF16)

    slots = (slot_a, slot_b)
    pending = []
    for k in range(SG_PER_STEP):
        mix(slots[k % 2], x_next[k], slots[(k + 1) % 2], pending)
        pending = epilogue(slots[k % 2], slice(k * CHUNK, (k + 1) * CHUNK))
    for thunk in pending:
        thunk()


def _ssd_block(src, first, cwx_ref, cbx_ref, cwb_ref, cbb_ref, dtb_ref, alog_ref, drow_ref, ng_ref,
               ext_x, ext_b, h_ref, xc_ref, y_ref, tick=lambda: None, tick_conv=lambda: None):
    n_xt, n_bt = SSM_WIDTH // LANE, SSM_BC // LANE
    tail = slice(CHUNK, CHUNK + 8)

    if first is not None:
        @pl.when(first)
        def _():
            h_ref[...] = jnp.zeros_like(h_ref)

    def stage(ext, col0, n_tiles):
        for t in range(n_tiles):
            prev = ext[t, tail, :]
            ext[t, 0:8, :] = prev if first is None else jnp.where(first, 0.0, prev)
            ext[t, 8:8 + CHUNK, :] = src[:, col0 + t * LANE:col0 + (t + 1) * LANE]

    def conv_tile(ext, w_ref, b_ref, t):
        sl = slice(t * LANE, (t + 1) * LANE)
        acc = b_ref[:, sl]
        for k in range(SSM_CONV):
            acc = acc + w_ref[k, :, sl] * ext[t, pl.ds(8 - (SSM_CONV - 1) + k, CHUNK), :]
        return _silu(acc)

    stage(ext_x, M_XS, n_xt)
    stage(ext_b, M_BC, n_bt)
    for t in range(n_xt):
        tick_conv()
        xc_ref[:, t * LANE:(t + 1) * LANE] = conv_tile(ext_x, cwx_ref, cbx_ref, t)
    bcv = []
    for t in range(n_bt):
        tick_conv()
        bcv.append(conv_tile(ext_b, cwb_ref, cbb_ref, t))

    li = lax.broadcasted_iota(jnp.int32, (CHUNK, CHUNK), 0)
    si = lax.broadcasted_iota(jnp.int32, (CHUNK, CHUNK), 1)
    causal = si <= li
    x_dt_t = (src[:, M_DT:M_DT + LANE] + dtb_ref[...]).T[0:SSM_HEADS, :]
    dt_t = _softplus(x_dt_t)
    a_dt_t = dt_t * (-jnp.exp(alog_ref[0:SSM_HEADS, :]))
    upper = jnp.where(li <= si, 1.0, 0.0).astype(BF16)
    parts = _dot(jnp.concatenate(_split3(a_dt_t), axis=0), upper)
    a_cs_t = (parts[0:SSM_HEADS] + parts[SSM_HEADS:2 * SSM_HEADS]
              + parts[2 * SSM_HEADS:3 * SSM_HEADS])
    a2_t = a_cs_t * LOG2E
    a2_last = a2_t[:, CHUNK - 1:CHUNK]
    w_t = dt_t * jnp.exp2(a2_last - a2_t)
    cd_t = jnp.exp2(a2_last)
    a2 = jnp.concatenate([a2_t, jnp.zeros((CHUNK - SSM_HEADS, CHUNK), F32)], axis=0).T
    e2 = jnp.exp2(a2)
    low = lax.broadcasted_iota(jnp.int32, (CHUNK, LANE), 1) < SSM_HEAD_DIM
    low_row = low[0:1]
    keep_lo = jnp.where(low, 1.0, 0.0).astype(BF16)
    keep_hi = jnp.where(low, 0.0, 1.0).astype(BF16)
    gw = SSM_WIDTH // SSM_GROUPS

    for g in range(SSM_GROUPS):
        b_g, c_g = bcv[g], bcv[SSM_GROUPS + g]
        cb = jnp.where(causal, _dot_nt(c_g.astype(BF16), b_g.astype(BF16)), 0.0)
        b_gt = b_g.T
        y_off = _dot(c_g.astype(BF16), h_ref[:, g * gw:(g + 1) * gw].astype(BF16))
        for pr in range(g * SSM_HPG // 2, (g + 1) * SSM_HPG // 2):
            tick()
            sl = slice(pr * LANE, (pr + 1) * LANE)
            x_pair = xc_ref[:, sl].astype(BF16)
            x_bd = jnp.concatenate([x_pair * keep_lo, x_pair * keep_hi], axis=0)
            m_cols, bw_cols, e_cols = [], [], []
            for j in (2 * pr, 2 * pr + 1):
                if j % 2:
                    tick()
                a_col = jnp.broadcast_to(a2[:, j:j + 1], (CHUNK, CHUNK))
                decay = jnp.exp2(jnp.minimum(a_col - a2_t[j:j + 1, :], 0.0))
                m_cols.append((cb * decay * dt_t[j:j + 1, :]).astype(BF16))
                bw_cols.append((b_gt * w_t[j:j + 1, :]).astype(BF16))
                e_cols.append(jnp.broadcast_to(e2[:, j:j + 1], (CHUNK, LANE)))
            y_diag = _dot(jnp.concatenate(m_cols, axis=1), x_bd)
            up = _dot(jnp.concatenate(bw_cols, axis=1), x_bd)
            k = pr - g * SSM_HPG // 2
            y_ref[:, sl] = y_diag + jnp.where(low, e_cols[0], e_cols[1]) * y_off[:, k * LANE:(k + 1) * LANE]
            cd_pair = jnp.where(low_row, cd_t[2 * pr:2 * pr + 1, :], cd_t[2 * pr + 1:2 * pr + 2, :])
            h_ref[:, sl] = h_ref[:, sl] * cd_pair + up

    out = []
    for g in range(SSM_GROUPS):
        tick()
        sl = slice(g * gw, (g + 1) * gw)
        yg = (y_ref[:, sl] + drow_ref[:, sl] * xc_ref[:, sl]) * _silu(src[:, M_ZM + g * gw:M_ZM + (g + 1) * gw])
        yg = yg * lax.rsqrt(jnp.mean(yg * yg, axis=-1, keepdims=True) + EPS)
        out.append(yg * ng_ref[:, sl])
    return jnp.concatenate(out, axis=-1)


def _ssd_kernel(x0_ref, xa_ref, xb_ref, xres_ref, acc_ref, gpre_ref, win_hbm, wbr_hbm, wout_hbm,
                gpost_ref, cwx_ref, cbx_ref, cwb_ref, cbb_ref, dtb_ref, alog_ref, drow_ref, ng_ref,
                o_ref, w_ref, wg_ref, wbr_ref, wout_ref, stage, stage_t, sem, slot_a, slot_b, hn_ref, ext_x, ext_b,
                h_ref, xc_ref, y_ref, ybf_ref, mg_ref, out_ref, *, nb, layer):
    i = pl.program_id(0)
    plan = _proj_plan([(w_ref, M_G), (wg_ref, D_MODEL)])

    @pl.when(i == 0)
    def _():
        ext_x[...] = jnp.zeros_like(ext_x)
        ext_b[...] = jnp.zeros_like(ext_b)
        _load_weights(_weight_jobs_t(win_hbm, layer, W_SSM_COL0, M_G, w_ref)
                      + _weight_jobs_t(win_hbm, layer, W_GATE_COL0 + 2 * D_MODEL, D_MODEL, wg_ref)
                      + _weight_jobs(wbr_hbm, layer, SSM_WIDTH, 0, D_MODEL, wbr_ref)
                      + _weight_jobs(wout_hbm, layer, D_MODEL, 0, D_MODEL, wout_ref), stage, stage_t, sem)
        _rms_proj(x0_ref, gpre_ref, slot_a, hn_ref, plan)

    n_ticks = SSM_HEADS + SSM_GROUPS
    out_cols = _col_chunks(D_MODEL, D_MODEL // 256)

    def epilogue(src, rows):
        def branch(c0, c1):
            gate = _sigmoid(src[:, M_G + c0:M_G + c1])
            mg_ref[:, c0:c1] = (acc_ref[rows, c0:c1]
                                + gate * _dot(ybf_ref[...], wbr_ref[:, c0:c1])).astype(BF16)

        def outproj(c0, c1):
            out_ref[:, c0:c1] = _dot(mg_ref[...], wout_ref[:, c0:c1])

        def finish():
            out = out_ref[...]
            ms = jnp.mean(out * out, axis=-1, keepdims=True)
            o_ref[rows, :] = xres_ref[rows, :] + out * lax.rsqrt(ms + EPS) * gpost_ref[...]

        return ([functools.partial(branch, c0, c1) for c0, c1 in out_cols]
                + [functools.partial(outproj, c0, c1) for c0, c1 in out_cols] + [finish])

    def mix(src, first, x_next_ref, dst, pending):
        _rms_norm_to(x_next_ref, gpre_ref, hn_ref)
        spread = _Spread(_proj_thunks(hn_ref, dst, plan), n_ticks)
        spread_conv = _Spread(pending, (SSM_WIDTH + SSM_BC) // LANE)
        y = _ssd_block(src, first, cwx_ref, cbx_ref, cwb_ref, cbb_ref, dtb_ref, alog_ref, drow_ref, ng_ref,
                       ext_x, ext_b, h_ref, xc_ref, y_ref, tick=spread.tick, tick_conv=spread_conv.tick)
        spread.flush()
        ybf_ref[...] = y.astype(BF16)

    first = ((2 * i) % nb) == 0
    mix(slot_a, first, xa_ref, slot_b, [])
    mix(slot_b, None, xb_ref, slot_a, epilogue(slot_a, slice(0, CHUNK)))
    for thunk in epilogue(slot_b, slice(CHUNK, 2 * CHUNK)):
        thunk()


def _x_specs(n_blk, per_step=2):
    nxt = lambda k: pl.BlockSpec((CHUNK, D_MODEL), lambda i: (jnp.minimum(per_step * i + k, n_blk - 1), 0))
    return [pl.BlockSpec((CHUNK, D_MODEL), lambda i: (0, 0))] + [nxt(k) for k in range(1, per_step + 1)]


def _const(shape):
    return pl.BlockSpec(shape, lambda i: (0,) * len(shape))


def _row_blocks(per_step=2):
    return pl.BlockSpec((per_step * CHUNK, D_MODEL), lambda i: (i, 0))


def _weight_scratch(shapes):
    return ([pltpu.VMEM(s, BF16) for s in shapes]
            + [pltpu.VMEM((2, STAGE_ROWS, STAGE_COLS), F32), pltpu.VMEM((2, STAGE_COLS, STAGE_ROWS), F32),
               pltpu.SemaphoreType.DMA((2,))])


def _any():
    return pl.BlockSpec(memory_space=pl.ANY)


def _att_branch(x2, g_pre, w_in_t, w_br, layer, rel_bias, sinks, bucket, nb):
    t = x2.shape[0]
    n_blk = t // CHUNK
    smem = pl.BlockSpec(memory_space=pltpu.SMEM)
    return pl.pallas_call(
        functools.partial(_att_kernel, nb=nb, layer=layer),
        grid=(n_blk // ATT_PER_STEP,),
        in_specs=[smem, smem, _const((CHUNK, 2 * CHUNK))] + _x_specs(n_blk, ATT_PER_STEP)
                 + [_const((1, D_MODEL)), _any(), _any()],
        out_specs=_row_blocks(ATT_PER_STEP),
        out_shape=jax.ShapeDtypeStruct((t, D_MODEL), F32),
        scratch_shapes=_weight_scratch([(D_MODEL, A_G), (D_MODEL, W_PAD_COLS), (ATT_WIDTH, W_PAD_COLS)])
                       + [pltpu.VMEM((CHUNK, A_COLS), F32), pltpu.VMEM((CHUNK, A_COLS), F32),
                        pltpu.VMEM((CHUNK, D_MODEL), BF16),
                        pltpu.VMEM((CHUNK, 2 * LANE), F32),
                        pltpu.VMEM((2, ATT_HEADS, CHUNK, CHUNK), F32),
                        pltpu.VMEM((CHUNK, D_MODEL), BF16)],
        compiler_params=_cparams(),
        name="att",
    )(rel_bias, sinks, bucket, *([x2] * (1 + ATT_PER_STEP)), g_pre.reshape(1, -1), w_in_t, w_br)


def _sg_branch(x2, acc, g_pre, w_in_t, w_br, layer, ln_g, ln_b, w_s, b_full):
    t = x2.shape[0]
    n_blk = t // CHUNK
    return pl.pallas_call(
        functools.partial(_sg_kernel, layer=layer),
        grid=(n_blk // SG_PER_STEP,),
        in_specs=_x_specs(n_blk, SG_PER_STEP) + [_row_blocks(SG_PER_STEP), _const((1, D_MODEL)), _any(), _any(),
                                    _const((1, SG_WIDTH)), _const((1, SG_WIDTH)),
                                    _const((SG_GROUPS, CHUNK, CHUNK)), _const((CHUNK, SG_WIDTH))],
        out_specs=_row_blocks(SG_PER_STEP),
        out_shape=jax.ShapeDtypeStruct((t, D_MODEL), F32),
        scratch_shapes=_weight_scratch([(D_MODEL, S_MAIN_COLS), (D_MODEL, W_PAD_COLS), (SG_WIDTH, W_PAD_COLS)])
                       + [pltpu.VMEM((CHUNK, S_COLS), F32), pltpu.VMEM((CHUNK, S_COLS), F32),
                        pltpu.VMEM((CHUNK, D_MODEL), BF16),
                        pltpu.VMEM((SG_GROUPS, CHUNK, CHUNK), BF16),
                        pltpu.VMEM((CHUNK, SG_WIDTH), BF16)],
        compiler_params=_cparams(),
        name="sg",
    )(*([x2] * (1 + SG_PER_STEP)), acc, g_pre.reshape(1, -1), w_in_t, w_br, ln_g.reshape(1, -1), ln_b.reshape(1, -1),
      w_s, b_full)


def _ssd_branch(x2, acc, g_pre, w_in_t, w_br, w_out, layer, g_post, cwx, cbx, cwb, cbb, dtb, alog, drow,
                ng, nb):
    t = x2.shape[0]
    n_blk = t // CHUNK
    return pl.pallas_call(
        functools.partial(_ssd_kernel, nb=nb, layer=layer),
        grid=(n_blk // 2,),
        in_specs=_x_specs(n_blk) + [_row_blocks(), _row_blocks(), _const((1, D_MODEL)), _any(), _any(), _any(),
                                    _const((1, D_MODEL)),
                                    _const((SSM_CONV, 1, SSM_WIDTH)), _const((1, SSM_WIDTH)),
                                    _const((SSM_CONV, 1, SSM_BC)), _const((1, SSM_BC)),
                                    _const((1, LANE)), _const((LANE, 1)),
                                    _const((1, SSM_WIDTH)), _const((1, SSM_WIDTH))],
        out_specs=_row_blocks(),
        out_shape=jax.ShapeDtypeStruct((t, D_MODEL), F32),
        scratch_shapes=_weight_scratch([(D_MODEL, M_G), (D_MODEL, W_PAD_COLS), (SSM_WIDTH, W_PAD_COLS),
                                        (D_MODEL, W_PAD_COLS)])
                       + [pltpu.VMEM((CHUNK, M_COLS), F32), pltpu.VMEM((CHUNK, M_COLS), F32),
                        pltpu.VMEM((CHUNK, D_MODEL), BF16),
                        pltpu.VMEM((SSM_WIDTH // LANE, 8 + CHUNK, LANE), F32),
                        pltpu.VMEM((SSM_BC // LANE, 8 + CHUNK, LANE), F32),
                        pltpu.VMEM((SSM_STATE, SSM_WIDTH), F32),
                        pltpu.VMEM((CHUNK, SSM_WIDTH), F32),
                        pltpu.VMEM((CHUNK, SSM_WIDTH), F32),
                        pltpu.VMEM((CHUNK, SSM_WIDTH), BF16),
                        pltpu.VMEM((CHUNK, D_MODEL), BF16),
                        pltpu.VMEM((CHUNK, D_MODEL), F32)],
        compiler_params=_cparams(),
        name="ssd",
    )(x2, x2, x2, x2, acc, g_pre.reshape(1, -1), w_in_t, w_br, w_out, g_post.reshape(1, -1),
      cwx, cbx, cwb, cbb, dtb, alog, drow, ng)


def _rel_bucket_table():
    qi = jnp.arange(CHUNK, dtype=jnp.int32)[:, None]
    kj = jnp.arange(2 * CHUNK, dtype=jnp.int32)[None, :]
    dist = jnp.maximum(qi + CHUNK - kj, 0)
    max_exact = REL_BUCKETS // 2
    dist_f = jnp.maximum(dist, 1).astype(F32)
    large = max_exact + (jnp.log(dist_f / max_exact) / math.log(REL_MAX_DIST / max_exact)
                         * (REL_BUCKETS - max_exact)).astype(jnp.int32)
    large = jnp.minimum(large, REL_BUCKETS - 1)
    return jnp.where(dist < max_exact, dist, large)


def _pad_lanes(v):
    return jnp.pad(v, (0, LANE - v.shape[0])).reshape(1, LANE)


def kernel(x, w_in, norm_pre, norm_post, rel_bias, att_sinks, sg_ln_g, sg_ln_b, sg_w, sg_b, ssm_conv_w, ssm_conv_b, ssm_dt_bias, ssm_a_log, ssm_d, ssm_norm_g, w_br_att, w_br_sg, w_br_ssm, w_out):
    bsz, seq, d = x.shape
    depth = w_in.shape[0]
    assert d == D_MODEL and seq % (2 * CHUNK) == 0 and seq % (SG_PER_STEP * CHUNK) == 0
    nb = seq // CHUNK
    x2 = x.reshape(bsz * seq, d)
    bucket = _rel_bucket_table()
    w_in_t = jnp.swapaxes(w_in, 1, 2)
    for l in range(depth):
        acc = _att_branch(x2, norm_pre[l], w_in_t, w_br_att, l, rel_bias, att_sinks[l], bucket, nb)
        b_full = jnp.repeat(jnp.transpose(sg_b[l]), CHUNK, axis=1)
        acc = _sg_branch(x2, acc, norm_pre[l], w_in_t, w_br_sg, l, sg_ln_g[l], sg_ln_b[l], sg_w[l], b_full)
        cw, cb = ssm_conv_w[l], ssm_conv_b[l]
        x2 = _ssd_branch(x2, acc, norm_pre[l], w_in_t, w_br_ssm, w_out, l, norm_post[l],
                         cw[:, None, :SSM_WIDTH], cb[:SSM_WIDTH].reshape(1, -1),
                         cw[:, None, SSM_WIDTH:], cb[SSM_WIDTH:].reshape(1, -1),
                         _pad_lanes(ssm_dt_bias[l]), _pad_lanes(ssm_a_log[l]).reshape(LANE, 1),
                         jnp.repeat(ssm_d[l], SSM_HEAD_DIM).reshape(1, -1),
                         ssm_norm_g[l].reshape(1, -1), nb)
    return x2.reshape(bsz, seq, d)
```

```python
import functools
import math

import jax
import jax.numpy as jnp
from jax import lax
from jax.experimental import pallas as pl
from jax.experimental.pallas import tpu as pltpu

F32 = jnp.float32
BF16 = jnp.bfloat16

D_MODEL = 1024
ATT_HEADS = 16
ATT_KV_HEADS = 2
ATT_HEAD_DIM = 64
ATT_WIDTH = ATT_HEADS * ATT_HEAD_DIM
ATT_GROUP = ATT_HEADS // ATT_KV_HEADS
CHUNK = 128
REL_BUCKETS = 32
REL_MAX_DIST = 128
SG_GROUPS = 8
SG_WIDTH = 1024
SSM_WIDTH = 2048
SSM_HEAD_DIM = 64
SSM_HEADS = 32
SSM_GROUPS = 4
SSM_STATE = 128
SSM_HPG = SSM_HEADS // SSM_GROUPS
SSM_CONV = 4
SSM_BC = 2 * SSM_GROUPS * SSM_STATE
EPS = 1e-6
NEG = -1e30
LOG2E = 1.4426950408889634
LANE = 128

A_Q, A_K, A_V, A_ZA, A_G, A_COLS = 0, 1024, 1152, 1280, 2304, 3328
S_U, S_VS, S_ZS, S_G, S_COLS = 0, 1024, 2048, 3072, 4096
M_ZM, M_XS, M_BC, M_DT, M_G, M_COLS = 0, 2048, 4096, 5120, 5248, 6272
VMEM_LIMIT = 56 * 1024 * 1024
W_PAD_COLS = D_MODEL + LANE
S_MAIN_COLS = S_G + LANE
W_SG_COL0, W_SSM_COL0, W_GATE_COL0 = 2304, 5376, 10528
ATT_PER_STEP = SG_PER_STEP = 4


def _cparams():
    return pltpu.CompilerParams(dimension_semantics=("arbitrary",), vmem_limit_bytes=VMEM_LIMIT)


def _sigmoid(x):
    return 0.5 * jnp.tanh(0.5 * x) + 0.5


def _silu(x):
    h = 0.5 * x
    return h * jnp.tanh(h) + h


def _softplus(x):
    return jnp.maximum(x, 0.0) + jnp.log1p(jnp.exp(-jnp.abs(x)))


def _dot(a, b):
    return jnp.dot(a, b, preferred_element_type=F32)


def _dot_nt(a, b):
    return lax.dot_general(a, b, (((1,), (1,)), ((), ())), preferred_element_type=F32)


def _rms_norm_to(x_ref, g_ref, h_ref):
    x = x_ref[...]
    ms = jnp.mean(x * x, axis=-1, keepdims=True)
    h_ref[...] = (x * lax.rsqrt(ms + EPS) * g_ref[...]).astype(BF16)


def _proj_cols(h_ref, w_ref, dst_ref, c0, c1, d0):
    dst_ref[:, d0:d0 + c1 - c0] = _dot(h_ref[...], w_ref[:, c0:c1])


def _proj_plan(parts, width=256):
    plan, d0 = [], 0
    for w_ref, n_cols in parts:
        for c0 in range(0, n_cols, width):
            c1 = min(c0 + width, n_cols)
            plan.append((w_ref, c0, c1, d0 + c0))
        d0 += n_cols
    return plan


def _proj_thunks(h_ref, dst_ref, plan):
    return [functools.partial(_proj_cols, h_ref, w_ref, dst_ref, c0, c1, d0) for w_ref, c0, c1, d0 in plan]


def _rms_proj(x_ref, g_ref, dst_ref, h_ref, plan):
    _rms_norm_to(x_ref, g_ref, h_ref)
    for thunk in _proj_thunks(h_ref, dst_ref, plan):
        thunk()


STAGE_ROWS, STAGE_COLS = 1024, 512


def _weight_jobs(src_hbm, layer, k_rows, col0, n_cols, dst_ref):
    jobs = []
    for r0 in range(0, k_rows, STAGE_ROWS):
        for c in range(0, n_cols, STAGE_COLS):
            w = min(STAGE_COLS, n_cols - c)
            jobs.append((src_hbm.at[layer, r0:r0 + STAGE_ROWS, col0 + c:col0 + c + w], dst_ref, r0, c, w))
    return jobs


def _weight_jobs_t(src_hbm, layer, row0, n_rows, dst_ref):
    jobs = []
    for c in range(0, n_rows, STAGE_COLS):
        w = min(STAGE_COLS, n_rows - c)
        jobs.append((src_hbm.at[layer, row0 + c:row0 + c + w, :], dst_ref, None, c, w))
    return jobs


def _load_weights(jobs, stage, stage_t, sem):
    def dma(k):
        src, _, r0, _, w = jobs[k]
        dst = stage_t.at[k % 2, 0:w, :] if r0 is None else stage.at[k % 2, :, 0:w]
        return pltpu.make_async_copy(src, dst, sem.at[k % 2])

    dma(0).start()
    for k, (_, dst, r0, d0, w) in enumerate(jobs):
        if k + 1 < len(jobs):
            dma(k + 1).start()
        dma(k).wait()
        if r0 is None:
            dst[:, d0:d0 + w] = stage_t[k % 2, 0:w, :].T.astype(BF16)
        else:
            dst[r0:r0 + STAGE_ROWS, d0:d0 + w] = stage[k % 2, :, 0:w].astype(BF16)


class _Spread:
    def __init__(self, thunks, n_ticks):
        self.thunks, self.n_ticks, self.ticks, self.done = thunks, n_ticks, 0, 0

    def tick(self):
        self.ticks += 1
        while self.done < len(self.thunks) and self.done * self.n_ticks < self.ticks * len(self.thunks):
            self.thunks[self.done]()
            self.done += 1

    def flush(self):
        self.ticks = self.n_ticks
        self.tick()


def _col_chunks(n_cols, n_chunks, width=256):
    tiles = -(-n_cols // width)
    per = [tiles // n_chunks + (1 if k < tiles % n_chunks else 0) for k in range(n_chunks)]
    out, c = [], 0
    for n_tiles in per:
        out.append((c, min(c + n_tiles * width, n_cols)))
        c = min(c + n_tiles * width, n_cols)
    return out


def _split3(a):
    hi = a.astype(BF16)
    r1 = a - hi.astype(F32)
    mid = r1.astype(BF16)
    lo = (r1 - mid.astype(F32)).astype(BF16)
    return hi, mid, lo


def _att_block(src, kv_prev, bias_ref, sink_ref, variant, tick=lambda: None):
    qi = lax.broadcasted_iota(jnp.int32, (CHUNK, CHUNK), 0)
    ci = lax.broadcasted_iota(jnp.int32, (CHUNK, CHUNK), 1)
    own = ci <= qi
    low = lax.broadcasted_iota(jnp.int32, (2 * CHUNK, LANE), 1) < ATT_HEAD_DIM
    low_q = lax.broadcasted_iota(jnp.int32, (CHUNK, LANE), 1) < ATT_HEAD_DIM
    scale = ATT_HEAD_DIM ** -0.5
    kcat = jnp.concatenate([src[:, A_K:A_K + LANE], kv_prev[:, 0:LANE]], axis=0)
    vcat = jnp.concatenate([src[:, A_V:A_V + LANE], kv_prev[:, LANE:2 * LANE]], axis=0)
    k_roll = pltpu.roll(kcat, ATT_HEAD_DIM, 1)
    v_roll = pltpu.roll(vcat, ATT_HEAD_DIM, 1)
    n_pair = ATT_GROUP // 2
    pairs = []
    for kvh in range(ATT_KV_HEADS):
        k_lo, k_hi = (kcat, k_roll) if kvh == 0 else (k_roll, kcat)
        v_lo, v_hi = (vcat, v_roll) if kvh == 0 else (v_roll, vcat)
        kz = (jnp.where(low, k_lo, 0.0).astype(BF16), jnp.where(low, 0.0, k_hi).astype(BF16))
        vz = (jnp.where(low, v_lo, 0.0).astype(BF16), jnp.where(low, 0.0, v_hi).astype(BF16))
        pr0 = kvh * n_pair
        q4 = jnp.concatenate([src[:, A_Q + pr * LANE:A_Q + (pr + 1) * LANE]
                              for pr in range(pr0, pr0 + n_pair)], axis=0)
        q4 = (q4 * scale).astype(BF16)
        lhs_cols = []
        recips = [[None, None] for _ in range(n_pair)]
        for par in range(2):
            tick()
            s2 = _dot_nt(q4, kz[par])
            rows = []
            for k in range(n_pair):
                h = 2 * (pr0 + k) + par
                s2k = s2[k * CHUNK:(k + 1) * CHUNK]
                s = jnp.where(own, s2k[:, 0:CHUNK], s2k[:, CHUNK:2 * CHUNK]) + bias_ref[variant, h]
                sink = sink_ref[h]
                m = jnp.maximum(jnp.max(s, axis=-1, keepdims=True), sink)
                p = jnp.exp(s - m)
                den = jnp.sum(p, axis=-1, keepdims=True) + jnp.exp(sink - m)
                recips[k][par] = 1.0 / den
                rows.append(jnp.concatenate([jnp.where(own, p, 0.0), jnp.where(own, 0.0, p)],
                                            axis=1).astype(BF16))
            lhs_cols.append(jnp.concatenate(rows, axis=0))
        tick()
        o4 = _dot(jnp.concatenate(lhs_cols, axis=1), jnp.concatenate(vz, axis=0))
        for k in range(n_pair):
            r_pair = jnp.where(low_q, jnp.broadcast_to(recips[k][0], (CHUNK, LANE)),
                               jnp.broadcast_to(recips[k][1], (CHUNK, LANE)))
            pairs.append(o4[k * CHUNK:(k + 1) * CHUNK] * r_pair)
    y = jnp.concatenate(pairs, axis=-1) * _silu(src[:, A_ZA:A_ZA + ATT_WIDTH])
    tick()
    tick()
    return y


def _att_kernel(rb_ref, sink_ref, bkt_ref, x0_ref, *refs, nb, layer):
    x_next, refs = refs[:ATT_PER_STEP], refs[ATT_PER_STEP:]
    (gpre_ref, win_hbm, wbr_hbm, o_ref, w_ref, wg_ref, wbr_ref, stage, stage_t, sem, slot_a, slot_b, h_ref,
     kv_prev, bias_ref, ybf_ref) = refs
    i = pl.program_id(0)
    plan = _proj_plan([(w_ref, A_G), (wg_ref, D_MODEL)])
    n_ticks = 2 * (ATT_KV_HEADS + 1) + 2

    @pl.when(i == 0)
    def _():
        qi = lax.broadcasted_iota(jnp.int32, (CHUNK, CHUNK), 0)
        ci = lax.broadcasted_iota(jnp.int32, (CHUNK, CHUNK), 1)
        own = ci <= qi
        bkt = jnp.where(own, bkt_ref[:, CHUNK:2 * CHUNK], bkt_ref[:, 0:CHUNK])
        for h in range(ATT_HEADS):
            acc = jnp.zeros((CHUNK, CHUNK), F32)
            for b in range(REL_BUCKETS):
                acc = jnp.where(bkt == b, rb_ref[b, h], acc)
            bias_ref[1, h] = acc
            bias_ref[0, h] = jnp.where(own, acc, NEG)
        kv_prev[...] = jnp.zeros_like(kv_prev)
        _load_weights(_weight_jobs_t(win_hbm, layer, 0, A_G, w_ref)
                      + _weight_jobs_t(win_hbm, layer, W_GATE_COL0, D_MODEL, wg_ref)
                      + _weight_jobs(wbr_hbm, layer, ATT_WIDTH, 0, D_MODEL, wbr_ref), stage, stage_t, sem)
        _rms_proj(x0_ref, gpre_ref, slot_a, h_ref, plan)

    def epilogue(src, rows):
        def branch(c0, c1):
            o_ref[rows, c0:c1] = (_sigmoid(src[:, A_G + c0:A_G + c1])
                                  * _dot(ybf_ref[...], wbr_ref[:, c0:c1]))
        return [functools.partial(branch, c0, c1) for c0, c1 in _col_chunks(D_MODEL, D_MODEL // 256)]

    def mix(src, variant, x_next_ref, dst, pending):
        _rms_norm_to(x_next_ref, gpre_ref, h_ref)
        spread = _Spread(pending + _proj_thunks(h_ref, dst, plan), n_ticks)
        y = _att_block(src, kv_prev, bias_ref, sink_ref, variant, tick=spread.tick)
        spread.flush()
        kv_prev[...] = src[:, A_K:A_K + 2 * LANE]
        ybf_ref[...] = y.astype(BF16)

    first = ((ATT_PER_STEP * i) % nb) == 0
    slots = (slot_a, slot_b)
    pending = []
    for k in range(ATT_PER_STEP):
        mix(slots[k % 2], jnp.where(first, 0, 1) if k == 0 else 1, x_next[k], slots[(k + 1) % 2], pending)
        pending = epilogue(slots[k % 2], slice(k * CHUNK, (k + 1) * CHUNK))
    for thunk in pending:
        thunk()


def _sg_block(src, lng_ref, lnb_ref, wt_ref, bfull_ref, tick=lambda: None):
    v = src[:, S_VS:S_VS + SG_WIDTH]
    mu = jnp.mean(v, axis=-1, keepdims=True)
    vc = v - mu
    var = jnp.mean(vc * vc, axis=-1, keepdims=True)
    vn = (vc * lax.rsqrt(var + EPS) * lng_ref[...] + lnb_ref[...]).astype(BF16)
    out = []
    zeros = jnp.zeros((CHUNK, LANE), BF16)
    for p in range(SG_GROUPS // 2):
        sl2 = slice(2 * p * LANE, (2 * p + 2) * LANE)
        v_bd = jnp.concatenate([jnp.concatenate([vn[:, 2 * p * LANE:(2 * p + 1) * LANE], zeros], axis=1),
                                jnp.concatenate([zeros, vn[:, (2 * p + 1) * LANE:(2 * p + 2) * LANE]], axis=1)],
                               axis=0)
        tick()
        mixed = _dot(wt_ref[p], v_bd) + bfull_ref[:, sl2]
        tick()
        out.append(src[:, S_U + 2 * p * LANE:S_U + (2 * p + 2) * LANE] * mixed
                   * _silu(src[:, S_ZS + 2 * p * LANE:S_ZS + (2 * p + 2) * LANE]))
    return jnp.concatenate(out, axis=-1)


def _sg_kernel(x0_ref, *refs, layer):
    x_next, refs = refs[:SG_PER_STEP], refs[SG_PER_STEP:]
    (acc_ref, gpre_ref, win_hbm, wbr_hbm, lng_ref, lnb_ref, ws_ref, bfull_ref, o_ref, w_ref, wg_ref, wbr_ref,
     stage, stage_t, sem, slot_a, slot_b, h_ref, wt_ref, ybf_ref) = refs
    i = pl.program_id(0)
    plan = _proj_plan([(w_ref, S_G), (wg_ref, D_MODEL)])

    @pl.when(i == 0)
    def _():
        ti = lax.broadcasted_iota(jnp.int32, (CHUNK, CHUNK), 0)
        si = lax.broadcasted_iota(jnp.int32, (CHUNK, CHUNK), 1)
        for p in range(SG_GROUPS // 2):
            wt_ref[p] = jnp.concatenate([jnp.where(si <= ti, ws_ref[2 * p], 0.0),
                                         jnp.where(si <= ti, ws_ref[2 * p + 1], 0.0)], axis=1).astype(BF16)
        _load_weights(_weight_jobs_t(win_hbm, layer, W_SG_COL0, S_G, w_ref)
                      + _weight_jobs_t(win_hbm, layer, W_GATE_COL0 + D_MODEL, D_MODEL, wg_ref)
                      + _weight_jobs(wbr_hbm, layer, SG_WIDTH, 0, D_MODEL, wbr_ref), stage, stage_t, sem)
        _rms_proj(x0_ref, gpre_ref, slot_a, h_ref, plan)

    def epilogue(src, rows):
        def branch(c0, c1):
            o_ref[rows, c0:c1] = (acc_ref[rows, c0:c1] + _sigmoid(src[:, S_G + c0:S_G + c1])
                                  * _dot(ybf_ref[...], wbr_ref[:, c0:c1]))
        return [functools.partial(branch, c0, c1) for c0, c1 in _col_chunks(D_MODEL, D_MODEL // 256)]

    def mix(src, x_next_ref, dst, pending):
        _rms_norm_to(x_next_ref, gpre_ref, h_ref)
        spread = _Spread(pending + _proj_thunks(h_ref, dst, plan), SG_GROUPS)
        y = _sg_block(src, lng_ref, lnb_ref, wt_ref, bfull_ref, tick=spread.tick)
        spread.flush()
        ybf_ref[...] = y.astype(BF16)

    slots = (slot_a, slot_b)
    pending = []
    for k in range(SG_PER_STEP):
        mix(slots[k % 2], x_next[k], slots[(k + 1) % 2], pending)
        pending = epilogue(slots[k % 2], slice(k * CHUNK, (k + 1) * CHUNK))
    for thunk in pending:
        thunk()


def _ssd_block(src, first, cwx_ref, cbx_ref, cwb_ref, cbb_ref, dtb_ref, alog_ref, drow_ref, ng_ref,
               ext_x, ext_b, h_ref, xc_ref, y_ref, tick=lambda: None, tick_conv=lambda: None):
    n_xt, n_bt = SSM_WIDTH // LANE, SSM_BC // LANE
    tail = slice(CHUNK, CHUNK + 8)

    if first is not None:
        @pl.when(first)
        def _():
            h_ref[...] = jnp.zeros_like(h_ref)

    def stage(ext, col0, n_tiles):
        for t in range(n_tiles):
            prev = ext[t, tail, :]
            ext[t, 0:8, :] = prev if first is None else jnp.where(first, 0.0, prev)
            ext[t, 8:8 + CHUNK, :] = src[:, col0 + t * LANE:col0 + (t + 1) * LANE]

    def conv_tile(ext, w_ref, b_ref, t):
        sl = slice(t * LANE, (t + 1) * LANE)
        acc = b_ref[:, sl]
        for k in range(SSM_CONV):
            acc = acc + w_ref[k, :, sl] * ext[t, pl.ds(8 - (SSM_CONV - 1) + k, CHUNK), :]
        return _silu(acc)

    stage(ext_x, M_XS, n_xt)
    stage(ext_b, M_BC, n_bt)
    for t in range(n_xt):
        tick_conv()
        xc_ref[:, t * LANE:(t + 1) * LANE] = conv_tile(ext_x, cwx_ref, cbx_ref, t)
    bcv = []
    for t in range(n_bt):
        tick_conv()
        bcv.append(conv_tile(ext_b, cwb_ref, cbb_ref, t))

    li = lax.broadcasted_iota(jnp.int32, (CHUNK, CHUNK), 0)
    si = lax.broadcasted_iota(jnp.int32, (CHUNK, CHUNK), 1)
    causal = si <= li
    x_dt_t = (src[:, M_DT:M_DT + LANE] + dtb_ref[...]).T[0:SSM_HEADS, :]
    dt_t = _softplus(x_dt_t)
    a_dt_t = dt_t * (-jnp.exp(alog_ref[0:SSM_HEADS, :]))
    upper = jnp.where(li <= si, 1.0, 0.0).astype(BF16)
    parts = _dot(jnp.concatenate(_split3(a_dt_t), axis=0), upper)
    a_cs_t = (parts[0:SSM_HEADS] + parts[SSM_HEADS:2 * SSM_HEADS]
              + parts[2 * SSM_HEADS:3 * SSM_HEADS])
    a2_t = a_cs_t * LOG2E
    a2_last = a2_t[:, CHUNK - 1:CHUNK]
    w_t = dt_t * jnp.exp2(a2_last - a2_t)
    cd_t = jnp.exp2(a2_last)
    a2 = jnp.concatenate([a2_t, jnp.zeros((CHUNK - SSM_HEADS, CHUNK), F32)], axis=0).T
    e2 = jnp.exp2(a2)
    low = lax.broadcasted_iota(jnp.int32, (CHUNK, LANE), 1) < SSM_HEAD_DIM
    low_row = low[0:1]
    keep_lo = jnp.where(low, 1.0, 0.0).astype(BF16)
    keep_hi = jnp.where(low, 0.0, 1.0).astype(BF16)
    gw = SSM_WIDTH // SSM_GROUPS

    for g in range(SSM_GROUPS):
        b_g, c_g = bcv[g], bcv[SSM_GROUPS + g]
        cb = jnp.where(causal, _dot_nt(c_g.astype(BF16), b_g.astype(BF16)), 0.0)
        b_gt = b_g.T
        y_off = _dot(c_g.astype(BF16), h_ref[:, g * gw:(g + 1) * gw].astype(BF16))
        for pr in range(g * SSM_HPG // 2, (g + 1) * SSM_HPG // 2):
            tick()
            sl = slice(pr * LANE, (pr + 1) * LANE)
            x_pair = xc_ref[:, sl].astype(BF16)
            x_bd = jnp.concatenate([x_pair * keep_lo, x_pair * keep_hi], axis=0)
            m_cols, bw_cols, e_cols = [], [], []
            for j in (2 * pr, 2 * pr + 1):
                if j % 2:
                    tick()
                a_col = jnp.broadcast_to(a2[:, j:j + 1], (CHUNK, CHUNK))
                decay = jnp.exp2(jnp.minimum(a_col - a2_t[j:j + 1, :], 0.0))
                m_cols.append((cb * decay * dt_t[j:j + 1, :]).astype(BF16))
                bw_cols.append((b_gt * w_t[j:j + 1, :]).astype(BF16))
                e_cols.append(jnp.broadcast_to(e2[:, j:j + 1], (CHUNK, LANE)))
            y_diag = _dot(jnp.concatenate(m_cols, axis=1), x_bd)
            up = _dot(jnp.concatenate(bw_cols, axis=1), x_bd)
            k = pr - g * SSM_HPG // 2
            y_ref[:, sl] = y_diag + jnp.where(low, e_cols[0], e_cols[1]) * y_off[:, k * LANE:(k + 1) * LANE]
            cd_pair = jnp.where(low_row, cd_t[2 * pr:2 * pr + 1, :], cd_t[2 * pr + 1:2 * pr + 2, :])
            h_ref[:, sl] = h_ref[:, sl] * cd_pair + up

    out = []
    for g in range(SSM_GROUPS):
        tick()
        sl = slice(g * gw, (g + 1) * gw)
        yg = (y_ref[:, sl] + drow_ref[:, sl] * xc_ref[:, sl]) * _silu(src[:, M_ZM + g * gw:M_ZM + (g + 1) * gw])
        yg = yg * lax.rsqrt(jnp.mean(yg * yg, axis=-1, keepdims=True) + EPS)
        out.append(yg * ng_ref[:, sl])
    return jnp.concatenate(out, axis=-1)


def _ssd_kernel(x0_ref, xa_ref, xb_ref, xres_ref, acc_ref, gpre_ref, win_hbm, wbr_hbm, wout_hbm,
                gpost_ref, cwx_ref, cbx_ref, cwb_ref, cbb_ref, dtb_ref, alog_ref, drow_ref, ng_ref,
                o_ref, w_ref, wg_ref, wbr_ref, wout_ref, stage, stage_t, sem, slot_a, slot_b, hn_ref, ext_x, ext_b,
                h_ref, xc_ref, y_ref, ybf_ref, mg_ref, out_ref, *, nb, layer):
    i = pl.program_id(0)
    plan = _proj_plan([(w_ref, M_G), (wg_ref, D_MODEL)])

    @pl.when(i == 0)
    def _():
        ext_x[...] = jnp.zeros_like(ext_x)
        ext_b[...] = jnp.zeros_like(ext_b)
        _load_weights(_weight_jobs_t(win_hbm, layer, W_SSM_COL0, M_G, w_ref)
                      + _weight_jobs_t(win_hbm, layer, W_GATE_COL0 + 2 * D_MODEL, D_MODEL, wg_ref)
                      + _weight_jobs(wbr_hbm, layer, SSM_WIDTH, 0, D_MODEL, wbr_ref)
                      + _weight_jobs(wout_hbm, layer, D_MODEL, 0, D_MODEL, wout_ref), stage, stage_t, sem)
        _rms_proj(x0_ref, gpre_ref, slot_a, hn_ref, plan)

    n_ticks = SSM_HEADS + SSM_GROUPS
    out_cols = _col_chunks(D_MODEL, D_MODEL // 256)

    def epilogue(src, rows):
        def branch(c0, c1):
            gate = _sigmoid(src[:, M_G + c0:M_G + c1])
            mg_ref[:, c0:c1] = (acc_ref[rows, c0:c1]
                                + gate * _dot(ybf_ref[...], wbr_ref[:, c0:c1])).astype(BF16)

        def outproj(c0, c1):
            out_ref[:, c0:c1] = _dot(mg_ref[...], wout_ref[:, c0:c1])

        def finish():
            out = out_ref[...]
            ms = jnp.mean(out * out, axis=-1, keepdims=True)
            o_ref[rows, :] = xres_ref[rows, :] + out * lax.rsqrt(ms + EPS) * gpost_ref[...]

        return ([functools.partial(branch, c0, c1) for c0, c1 in out_cols]
                + [functools.partial(outproj, c0, c1) for c0, c1 in out_cols] + [finish])

    def mix(src, first, x_next_ref, dst, pending):
        _rms_norm_to(x_next_ref, gpre_ref, hn_ref)
        spread = _Spread(_proj_thunks(hn_ref, dst, plan), n_ticks)
        spread_conv = _Spread(pending, (SSM_WIDTH + SSM_BC) // LANE)
        y = _ssd_block(src, first, cwx_ref, cbx_ref, cwb_ref, cbb_ref, dtb_ref, alog_ref, drow_ref, ng_ref,
                       ext_x, ext_b, h_ref, xc_ref, y_ref, tick=spread.tick, tick_conv=spread_conv.tick)
        spread.flush()
        ybf_ref[...] = y.astype(BF16)

    first = ((2 * i) % nb) == 0
    mix(slot_a, first, xa_ref, slot_b, [])
    mix(slot_b, None, xb_ref, slot_a, epilogue(slot_a, slice(0, CHUNK)))
    for thunk in epilogue(slot_b, slice(CHUNK, 2 * CHUNK)):
        thunk()


def _x_specs(n_blk, per_step=2):
    nxt = lambda k: pl.BlockSpec((CHUNK, D_MODEL), lambda i: (jnp.minimum(per_step * i + k, n_blk - 1), 0))
    return [pl.BlockSpec((CHUNK, D_MODEL), lambda i: (0, 0))] + [nxt(k) for k in range(1, per_step + 1)]


def _const(shape):
    return pl.BlockSpec(shape, lambda i: (0,) * len(shape))


def _row_blocks(per_step=2):
    return pl.BlockSpec((per_step * CHUNK, D_MODEL), lambda i: (i, 0))


def _weight_scratch(shapes):
    return ([pltpu.VMEM(s, BF16) for s in shapes]
            + [pltpu.VMEM((2, STAGE_ROWS, STAGE_COLS), F32), pltpu.VMEM((2, STAGE_COLS, STAGE_ROWS), F32),
               pltpu.SemaphoreType.DMA((2,))])


def _any():
    return pl.BlockSpec(memory_space=pl.ANY)


def _att_branch(x2, g_pre, w_in_t, w_br, layer, rel_bias, sinks, bucket, nb):
    t = x2.shape[0]
    n_blk = t // CHUNK
    smem = pl.BlockSpec(memory_space=pltpu.SMEM)
    return pl.pallas_call(
        functools.partial(_att_kernel, nb=nb, layer=layer),
        grid=(n_blk // ATT_PER_STEP,),
        in_specs=[smem, smem, _const((CHUNK, 2 * CHUNK))] + _x_specs(n_blk, ATT_PER_STEP)
                 + [_const((1, D_MODEL)), _any(), _any()],
        out_specs=_row_blocks(ATT_PER_STEP),
        out_shape=jax.ShapeDtypeStruct((t, D_MODEL), F32),
        scratch_shapes=_weight_scratch([(D_MODEL, A_G), (D_MODEL, W_PAD_COLS), (ATT_WIDTH, W_PAD_COLS)])
                       + [pltpu.VMEM((CHUNK, A_COLS), F32), pltpu.VMEM((CHUNK, A_COLS), F32),
                        pltpu.VMEM((CHUNK, D_MODEL), BF16),
                        pltpu.VMEM((CHUNK, 2 * LANE), F32),
                        pltpu.VMEM((2, ATT_HEADS, CHUNK, CHUNK), F32),
                        pltpu.VMEM((CHUNK, D_MODEL), BF16)],
        compiler_params=_cparams(),
        name="att",
    )(rel_bias, sinks, bucket, *([x2] * (1 + ATT_PER_STEP)), g_pre.reshape(1, -1), w_in_t, w_br)


def _sg_branch(x2, acc, g_pre, w_in_t, w_br, layer, ln_g, ln_b, w_s, b_full):
    t = x2.shape[0]
    n_blk = t // CHUNK
    return pl.pallas_call(
        functools.partial(_sg_kernel, layer=layer),
        grid=(n_blk // SG_PER_STEP,),
        in_specs=_x_specs(n_blk, SG_PER_STEP) + [_row_blocks(SG_PER_STEP), _const((1, D_MODEL)), _any(), _any(),
                                    _const((1, SG_WIDTH)), _const((1, SG_WIDTH)),
                                    _const((SG_GROUPS, CHUNK, CHUNK)), _const((CHUNK, SG_WIDTH))],
        out_specs=_row_blocks(SG_PER_STEP),
        out_shape=jax.ShapeDtypeStruct((t, D_MODEL), F32),
        scratch_shapes=_weight_scratch([(D_MODEL, S_MAIN_COLS), (D_MODEL, W_PAD_COLS), (SG_WIDTH, W_PAD_COLS)])
                       + [pltpu.VMEM((CHUNK, S_COLS), F32), pltpu.VMEM((CHUNK, S_COLS), F32),
                        pltpu.VMEM((CHUNK, D_MODEL), BF16),
                        pltpu.VMEM((SG_GROUPS // 2, CHUNK, 2 * CHUNK), BF16),
                        pltpu.VMEM((CHUNK, SG_WIDTH), BF16)],
        compiler_params=_cparams(),
        name="sg",
    )(*([x2] * (1 + SG_PER_STEP)), acc, g_pre.reshape(1, -1), w_in_t, w_br, ln_g.reshape(1, -1), ln_b.reshape(1, -1),
      w_s, b_full)


def _ssd_branch(x2, acc, g_pre, w_in_t, w_br, w_out, layer, g_post, cwx, cbx, cwb, cbb, dtb, alog, drow,
                ng, nb):
    t = x2.shape[0]
    n_blk = t // CHUNK
    return pl.pallas_call(
        functools.partial(_ssd_kernel, nb=nb, layer=layer),
        grid=(n_blk // 2,),
        in_specs=_x_specs(n_blk) + [_row_blocks(), _row_blocks(), _const((1, D_MODEL)), _any(), _any(), _any(),
                                    _const((1, D_MODEL)),
                                    _const((SSM_CONV, 1, SSM_WIDTH)), _const((1, SSM_WIDTH)),
                                    _const((SSM_CONV, 1, SSM_BC)), _const((1, SSM_BC)),
                                    _const((1, LANE)), _const((LANE, 1)),
                                    _const((1, SSM_WIDTH)), _const((1, SSM_WIDTH))],
        out_specs=_row_blocks(),
        out_shape=jax.ShapeDtypeStruct((t, D_MODEL), F32),
        scratch_shapes=_weight_scratch([(D_MODEL, M_G), (D_MODEL, W_PAD_COLS), (SSM_WIDTH, W_PAD_COLS),
                                        (D_MODEL, W_PAD_COLS)])
                       + [pltpu.VMEM((CHUNK, M_COLS), F32), pltpu.VMEM((CHUNK, M_COLS), F32),
                        pltpu.VMEM((CHUNK, D_MODEL), BF16),
                        pltpu.VMEM((SSM_WIDTH // LANE, 8 + CHUNK, LANE), F32),
                        pltpu.VMEM((SSM_BC // LANE, 8 + CHUNK, LANE), F32),
                        pltpu.VMEM((SSM_STATE, SSM_WIDTH), F32),
                        pltpu.VMEM((CHUNK, SSM_WIDTH), F32),
                        pltpu.VMEM((CHUNK, SSM_WIDTH), F32),
                        pltpu.VMEM((CHUNK, SSM_WIDTH), BF16),
                        pltpu.VMEM((CHUNK, D_MODEL), BF16),
                        pltpu.VMEM((CHUNK, D_MODEL), F32)],
        compiler_params=_cparams(),
        name="ssd",
    )(x2, x2, x2, x2, acc, g_pre.reshape(1, -1), w_in_t, w_br, w_out, g_post.reshape(1, -1),
      cwx, cbx, cwb, cbb, dtb, alog, drow, ng)


def _rel_bucket_table():
    qi = jnp.arange(CHUNK, dtype=jnp.int32)[:, None]
    kj = jnp.arange(2 * CHUNK, dtype=jnp.int32)[None, :]
    dist = jnp.maximum(qi + CHUNK - kj, 0)
    max_exact = REL_BUCKETS // 2
    dist_f = jnp.maximum(dist, 1).astype(F32)
    large = max_exact + (jnp.log(dist_f / max_exact) / math.log(REL_MAX_DIST / max_exact)
                         * (REL_BUCKETS - max_exact)).astype(jnp.int32)
    large = jnp.minimum(large, REL_BUCKETS - 1)
    return jnp.where(dist < max_exact, dist, large)


def _pad_lanes(v):
    return jnp.pad(v, (0, LANE - v.shape[0])).reshape(1, LANE)


def kernel(x, w_in, norm_pre, norm_post, rel_bias, att_sinks, sg_ln_g, sg_ln_b, sg_w, sg_b, ssm_conv_w, ssm_conv_b, ssm_dt_bias, ssm_a_log, ssm_d, ssm_norm_g, w_br_att, w_br_sg, w_br_ssm, w_out):
    bsz, seq, d = x.shape
    depth = w_in.shape[0]
    assert d == D_MODEL and seq % (2 * CHUNK) == 0 and seq % (SG_PER_STEP * CHUNK) == 0
    nb = seq // CHUNK
    x2 = x.reshape(bsz * seq, d)
    bucket = _rel_bucket_table()
    w_in_t = jnp.swapaxes(w_in, 1, 2)
    for l in range(depth):
        acc = _att_branch(x2, norm_pre[l], w_in_t, w_br_att, l, rel_bias, att_sinks[l], bucket, nb)
        b_full = jnp.repeat(jnp.transpose(sg_b[l]), CHUNK, axis=1)
        acc = _sg_branch(x2, acc, norm_pre[l], w_in_t, w_br_sg, l, sg_ln_g[l], sg_ln_b[l], sg_w[l], b_full)
        cw, cb = ssm_conv_w[l], ssm_conv_b[l]
        x2 = _ssd_branch(x2, acc, norm_pre[l], w_in_t, w_br_ssm, w_out, l, norm_post[l],
                         cw[:, None, :SSM_WIDTH], cb[:SSM_WIDTH].reshape(1, -1),
                         cw[:, None, SSM_WIDTH:], cb[SSM_WIDTH:].reshape(1, -1),
                         _pad_lanes(ssm_dt_bias[l]), _pad_lanes(ssm_a_log[l]).reshape(LANE, 1),
                         jnp.repeat(ssm_d[l], SSM_HEAD_DIM).reshape(1, -1),
                         ssm_norm_g[l].reshape(1, -1), nb)
    return x2.reshape(bsz, seq, d)
```

```python
import functools
import math

import jax
import jax.numpy as jnp
from jax import lax
from jax.experimental import pallas as pl
from jax.experimental.pallas import tpu as pltpu

F32 = jnp.float32
BF16 = jnp.bfloat16

D_MODEL = 1024
ATT_HEADS = 16
ATT_KV_HEADS = 2
ATT_HEAD_DIM = 64
ATT_WIDTH = ATT_HEADS * ATT_HEAD_DIM
ATT_GROUP = ATT_HEADS // ATT_KV_HEADS
CHUNK = 128
REL_BUCKETS = 32
REL_MAX_DIST = 128
SG_GROUPS = 8
SG_WIDTH = 1024
SSM_WIDTH = 2048
SSM_HEAD_DIM = 64
SSM_HEADS = 32
SSM_GROUPS = 4
SSM_STATE = 128
SSM_HPG = SSM_HEADS // SSM_GROUPS
SSM_CONV = 4
SSM_BC = 2 * SSM_GROUPS * SSM_STATE
EPS = 1e-6
NEG = -1e30
LOG2E = 1.4426950408889634
LANE = 128

A_Q, A_K, A_V, A_ZA, A_G, A_COLS = 0, 1024, 1152, 1280, 2304, 3328
S_U, S_VS, S_ZS, S_G, S_COLS = 0, 1024, 2048, 3072, 4096
M_ZM, M_XS, M_BC, M_DT, M_G, M_COLS = 0, 2048, 4096, 5120, 5248, 6272
VMEM_LIMIT = 56 * 1024 * 1024
W_PAD_COLS = D_MODEL + LANE
S_MAIN_COLS = S_G + LANE
W_SG_COL0, W_SSM_COL0, W_GATE_COL0 = 2304, 5376, 10528
ATT_PER_STEP = SG_PER_STEP = 4


def _cparams():
    return pltpu.CompilerParams(dimension_semantics=("arbitrary",), vmem_limit_bytes=VMEM_LIMIT)


def _sigmoid(x):
    return 0.5 * jnp.tanh(0.5 * x) + 0.5


def _silu(x):
    h = 0.5 * x
    return h * jnp.tanh(h) + h


def _softplus(x):
    return jnp.maximum(x, 0.0) + jnp.log1p(jnp.exp(-jnp.abs(x)))


def _dot(a, b):
    return jnp.dot(a, b, preferred_element_type=F32)


def _dot_nt(a, b):
    return lax.dot_general(a, b, (((1,), (1,)), ((), ())), preferred_element_type=F32)


def _rms_norm_to(x_ref, g_ref, h_ref):
    x = x_ref[...]
    ms = jnp.mean(x * x, axis=-1, keepdims=True)
    h_ref[...] = (x * lax.rsqrt(ms + EPS) * g_ref[...]).astype(BF16)


def _proj_cols(h_ref, w_ref, dst_ref, c0, c1, d0):
    dst_ref[:, d0:d0 + c1 - c0] = _dot(h_ref[...], w_ref[:, c0:c1])


def _proj_plan(parts, width=256):
    plan, d0 = [], 0
    for w_ref, n_cols in parts:
        for c0 in range(0, n_cols, width):
            c1 = min(c0 + width, n_cols)
            plan.append((w_ref, c0, c1, d0 + c0))
        d0 += n_cols
    return plan


def _proj_thunks(h_ref, dst_ref, plan):
    return [functools.partial(_proj_cols, h_ref, w_ref, dst_ref, c0, c1, d0) for w_ref, c0, c1, d0 in plan]


def _rms_proj(x_ref, g_ref, dst_ref, h_ref, plan):
    _rms_norm_to(x_ref, g_ref, h_ref)
    for thunk in _proj_thunks(h_ref, dst_ref, plan):
        thunk()


STAGE_ROWS, STAGE_COLS = 1024, 512


def _weight_jobs(src_hbm, layer, k_rows, col0, n_cols, dst_ref):
    jobs = []
    for r0 in range(0, k_rows, STAGE_ROWS):
        for c in range(0, n_cols, STAGE_COLS):
            w = min(STAGE_COLS, n_cols - c)
            jobs.append((src_hbm.at[layer, r0:r0 + STAGE_ROWS, col0 + c:col0 + c + w], dst_ref, r0, c, w))
    return jobs


def _weight_jobs_t(src_hbm, layer, row0, n_rows, dst_ref):
    jobs = []
    for c in range(0, n_rows, STAGE_COLS):
        w = min(STAGE_COLS, n_rows - c)
        jobs.append((src_hbm.at[layer, row0 + c:row0 + c + w, :], dst_ref, None, c, w))
    return jobs


def _load_weights(jobs, stage, stage_t, sem):
    def dma(k):
        src, _, r0, _, w = jobs[k]
        dst = stage_t.at[k % 2, 0:w, :] if r0 is None else stage.at[k % 2, :, 0:w]
        return pltpu.make_async_copy(src, dst, sem.at[k % 2])

    dma(0).start()
    for k, (_, dst, r0, d0, w) in enumerate(jobs):
        if k + 1 < len(jobs):
            dma(k + 1).start()
        dma(k).wait()
        if r0 is None:
            dst[:, d0:d0 + w] = stage_t[k % 2, 0:w, :].T.astype(BF16)
        else:
            dst[r0:r0 + STAGE_ROWS, d0:d0 + w] = stage[k % 2, :, 0:w].astype(BF16)


class _Spread:
    def __init__(self, thunks, n_ticks):
        self.thunks, self.n_ticks, self.ticks, self.done = thunks, n_ticks, 0, 0

    def tick(self):
        self.ticks += 1
        while self.done < len(self.thunks) and self.done * self.n_ticks < self.ticks * len(self.thunks):
            self.thunks[self.done]()
            self.done += 1

    def flush(self):
        self.ticks = self.n_ticks
        self.tick()


def _col_chunks(n_cols, n_chunks, width=256):
    tiles = -(-n_cols // width)
    per = [tiles // n_chunks + (1 if k < tiles % n_chunks else 0) for k in range(n_chunks)]
    out, c = [], 0
    for n_tiles in per:
        out.append((c, min(c + n_tiles * width, n_cols)))
        c = min(c + n_tiles * width, n_cols)
    return out


def _split3(a):
    hi = a.astype(BF16)
    r1 = a - hi.astype(F32)
    mid = r1.astype(BF16)
    lo = (r1 - mid.astype(F32)).astype(BF16)
    return hi, mid, lo


def _att_block(src, kv_prev, bias_ref, sink_ref, variant, tick=lambda: None):
    qi = lax.broadcasted_iota(jnp.int32, (CHUNK, CHUNK), 0)
    ci = lax.broadcasted_iota(jnp.int32, (CHUNK, CHUNK), 1)
    own = ci <= qi
    low = lax.broadcasted_iota(jnp.int32, (2 * CHUNK, LANE), 1) < ATT_HEAD_DIM
    low_q = lax.broadcasted_iota(jnp.int32, (CHUNK, LANE), 1) < ATT_HEAD_DIM
    scale = ATT_HEAD_DIM ** -0.5
    kcat = jnp.concatenate([src[:, A_K:A_K + LANE], kv_prev[:, 0:LANE]], axis=0)
    vcat = jnp.concatenate([src[:, A_V:A_V + LANE], kv_prev[:, LANE:2 * LANE]], axis=0)
    k_roll = pltpu.roll(kcat, ATT_HEAD_DIM, 1)
    v_roll = pltpu.roll(vcat, ATT_HEAD_DIM, 1)
    n_pair = ATT_GROUP // 2
    pairs = []
    for kvh in range(ATT_KV_HEADS):
        k_lo, k_hi = (kcat, k_roll) if kvh == 0 else (k_roll, kcat)
        v_lo, v_hi = (vcat, v_roll) if kvh == 0 else (v_roll, vcat)
        kz = (jnp.where(low, k_lo, 0.0).astype(BF16), jnp.where(low, 0.0, k_hi).astype(BF16))
        vz = (jnp.where(low, v_lo, 0.0).astype(BF16), jnp.where(low, 0.0, v_hi).astype(BF16))
        pr0 = kvh * n_pair
        q4 = jnp.concatenate([src[:, A_Q + pr * LANE:A_Q + (pr + 1) * LANE]
                              for pr in range(pr0, pr0 + n_pair)], axis=0)
        q4 = (q4 * scale).astype(BF16)
        lhs_cols = []
        recips = [[None, None] for _ in range(n_pair)]
        for par in range(2):
            tick()
            s2 = _dot_nt(q4, kz[par])
            rows = []
            for k in range(n_pair):
                h = 2 * (pr0 + k) + par
                s2k = s2[k * CHUNK:(k + 1) * CHUNK]
                s = jnp.where(own, s2k[:, 0:CHUNK], s2k[:, CHUNK:2 * CHUNK]) + bias_ref[variant, h]
                sink = sink_ref[h]
                m = jnp.maximum(jnp.max(s, axis=-1, keepdims=True), sink)
                p = jnp.exp(s - m)
                den = jnp.sum(p, axis=-1, keepdims=True) + jnp.exp(sink - m)
                recips[k][par] = 1.0 / den
                rows.append(jnp.concatenate([jnp.where(own, p, 0.0), jnp.where(own, 0.0, p)],
                                            axis=1).astype(BF16))
            lhs_cols.append(jnp.concatenate(rows, axis=0))
        tick()
        o4 = _dot(jnp.concatenate(lhs_cols, axis=1), jnp.concatenate(vz, axis=0))
        for k in range(n_pair):
            r_pair = jnp.where(low_q, jnp.broadcast_to(recips[k][0], (CHUNK, LANE)),
                               jnp.broadcast_to(recips[k][1], (CHUNK, LANE)))
            pairs.append(o4[k * CHUNK:(k + 1) * CHUNK] * r_pair)
    y = jnp.concatenate(pairs, axis=-1) * _silu(src[:, A_ZA:A_ZA + ATT_WIDTH])
    tick()
    tick()
    return y


def _att_kernel(rb_ref, sink_ref, bkt_ref, x0_ref, *refs, nb, layer):
    x_next, refs = refs[:ATT_PER_STEP], refs[ATT_PER_STEP:]
    (gpre_ref, win_hbm, wbr_hbm, o_ref, w_ref, wg_ref, wbr_ref, stage, stage_t, sem, slot_a, slot_b, h_ref,
     kv_prev, bias_ref, ybf_ref) = refs
    i = pl.program_id(0)
    plan = _proj_plan([(w_ref, A_G), (wg_ref, D_MODEL)])
    n_ticks = 2 * (ATT_KV_HEADS + 1) + 2

    @pl.when(i == 0)
    def _():
        qi = lax.broadcasted_iota(jnp.int32, (CHUNK, CHUNK), 0)
        ci = lax.broadcasted_iota(jnp.int32, (CHUNK, CHUNK), 1)
        own = ci <= qi
        bkt = jnp.where(own, bkt_ref[:, CHUNK:2 * CHUNK], bkt_ref[:, 0:CHUNK])
        for h in range(ATT_HEADS):
            acc = jnp.zeros((CHUNK, CHUNK), F32)
            for b in range(REL_BUCKETS):
                acc = jnp.where(bkt == b, rb_ref[b, h], acc)
            bias_ref[1, h] = acc
            bias_ref[0, h] = jnp.where(own, acc, NEG)
        kv_prev[...] = jnp.zeros_like(kv_prev)
        _load_weights(_weight_jobs_t(win_hbm, layer, 0, A_G, w_ref)
                      + _weight_jobs_t(win_hbm, layer, W_GATE_COL0, D_MODEL, wg_ref)
                      + _weight_jobs(wbr_hbm, layer, ATT_WIDTH, 0, D_MODEL, wbr_ref), stage, stage_t, sem)
        _rms_proj(x0_ref, gpre_ref, slot_a, h_ref, plan)

    def epilogue(src, rows):
        def branch(c0, c1):
            o_ref[rows, c0:c1] = (_sigmoid(src[:, A_G + c0:A_G + c1])
                                  * _dot(ybf_ref[...], wbr_ref[:, c0:c1]))
        return [functools.partial(branch, c0, c1) for c0, c1 in _col_chunks(D_MODEL, D_MODEL // 256)]

    def mix(src, variant, x_next_ref, dst, pending):
        _rms_norm_to(x_next_ref, gpre_ref, h_ref)
        spread = _Spread(pending + _proj_thunks(h_ref, dst, plan), n_ticks)
        y = _att_block(src, kv_prev, bias_ref, sink_ref, variant, tick=spread.tick)
        spread.flush()
        kv_prev[...] = src[:, A_K:A_K + 2 * LANE]
        ybf_ref[...] = y.astype(BF16)

    first = ((ATT_PER_STEP * i) % nb) == 0
    slots = (slot_a, slot_b)
    pending = []
    for k in range(ATT_PER_STEP):
        mix(slots[k % 2], jnp.where(first, 0, 1) if k == 0 else 1, x_next[k], slots[(k + 1) % 2], pending)
        pending = epilogue(slots[k % 2], slice(k * CHUNK, (k + 1) * CHUNK))
    for thunk in pending:
        thunk()


def _sg_block(src, lng_ref, lnb_ref, wt_ref, bfull_ref, tick=lambda: None):
    v = src[:, S_VS:S_VS + SG_WIDTH]
    mu = jnp.mean(v, axis=-1, keepdims=True)
    vc = v - mu
    var = jnp.mean(vc * vc, axis=-1, keepdims=True)
    vn = (vc * lax.rsqrt(var + EPS) * lng_ref[...] + lnb_ref[...]).astype(BF16)
    out = []
    zeros = jnp.zeros((CHUNK, LANE), BF16)
    for p in range(SG_GROUPS // 2):
        sl2 = slice(2 * p * LANE, (2 * p + 2) * LANE)
        v_bd = jnp.concatenate([jnp.concatenate([vn[:, 2 * p * LANE:(2 * p + 1) * LANE], zeros], axis=1),
                                jnp.concatenate([zeros, vn[:, (2 * p + 1) * LANE:(2 * p + 2) * LANE]], axis=1)],
                               axis=0)
        tick()
        mixed = _dot(wt_ref[p], v_bd) + bfull_ref[:, sl2]
        tick()
        out.append(src[:, S_U + 2 * p * LANE:S_U + (2 * p + 2) * LANE] * mixed
                   * _silu(src[:, S_ZS + 2 * p * LANE:S_ZS + (2 * p + 2) * LANE]))
    return jnp.concatenate(out, axis=-1)


def _sg_kernel(x0_ref, *refs, layer):
    x_next, refs = refs[:SG_PER_STEP], refs[SG_PER_STEP:]
    (acc_ref, gpre_ref, win_hbm, wbr_hbm, lng_ref, lnb_ref, ws_ref, bfull_ref, o_ref, w_ref, wg_ref, wbr_ref,
     stage, stage_t, sem, slot_a, slot_b, h_ref, wt_ref, ybf_ref) = refs
    i = pl.program_id(0)
    plan = _proj_plan([(w_ref, S_G), (wg_ref, D_MODEL)])

    @pl.when(i == 0)
    def _():
        ti = lax.broadcasted_iota(jnp.int32, (CHUNK, CHUNK), 0)
        si = lax.broadcasted_iota(jnp.int32, (CHUNK, CHUNK), 1)
        for p in range(SG_GROUPS // 2):
            wt_ref[p] = jnp.concatenate([jnp.where(si <= ti, ws_ref[2 * p], 0.0),
                                         jnp.where(si <= ti, ws_ref[2 * p + 1], 0.0)], axis=1).astype(BF16)
        _load_weights(_weight_jobs_t(win_hbm, layer, W_SG_COL0, S_G, w_ref)
                      + _weight_jobs_t(win_hbm, layer, W_GATE_COL0 + D_MODEL, D_MODEL, wg_ref)
                      + _weight_jobs(wbr_hbm, layer, SG_WIDTH, 0, D_MODEL, wbr_ref), stage, stage_t, sem)
        _rms_proj(x0_ref, gpre_ref, slot_a, h_ref, plan)

    def epilogue(src, rows):
        def branch(c0, c1):
            o_ref[rows, c0:c1] = (acc_ref[rows, c0:c1] + _sigmoid(src[:, S_G + c0:S_G + c1])
                                  * _dot(ybf_ref[...], wbr_ref[:, c0:c1]))
        return [functools.partial(branch, c0, c1) for c0, c1 in _col_chunks(D_MODEL, D_MODEL // 256)]

    def mix(src, x_next_ref, dst, pending):
        _rms_norm_to(x_next_ref, gpre_ref, h_ref)
        spread = _Spread(pending + _proj_thunks(h_ref, dst, plan), SG_GROUPS)
        y = _sg_block(src, lng_ref, lnb_ref, wt_ref, bfull_ref, tick=spread.tick)
        spread.flush()
        ybf_ref[...] = y.astype(BF16)

    slots = (slot_a, slot_b)
    pending = []
    for k in range(SG_PER_STEP):
        mix(slots[k % 2], x_next[k], slots[(k + 1) % 2], pending)
        pending = epilogue(slots[k % 2], slice(k * CHUNK, (k + 1) * CHUNK))
    for thunk in pending:
        thunk()


def _ssd_block(src, first, cwx_ref, cbx_ref, cwb_ref, cbb_ref, dtb_ref, alog_ref, drow_ref, ng_ref,
               ext_x, ext_b, h_ref, xc_ref, y_ref, tick=lambda: None, tick_conv=lambda: None):
    n_xt, n_bt = SSM_WIDTH // LANE, SSM_BC // LANE
    tail = slice(CHUNK, CHUNK + 8)

    if first is not None:
        @pl.when(first)
        def _():
            h_ref[...] = jnp.zeros_like(h_ref)

    def stage(ext, col0, n_tiles):
        for t in range(n_tiles):
            prev = ext[t, tail, :]
            ext[t, 0:8, :] = prev if first is None else jnp.where(first, 0.0, prev)
            ext[t, 8:8 + CHUNK, :] = src[:, col0 + t * LANE:col0 + (t + 1) * LANE]

    def conv_tile(ext, w_ref, b_ref, t):
        sl = slice(t * LANE, (t + 1) * LANE)
        acc = b_ref[:, sl]
        for k in range(SSM_CONV):
            acc = acc + w_ref[k, :, sl] * ext[t, pl.ds(8 - (SSM_CONV - 1) + k, CHUNK), :]
        return _silu(acc)

    stage(ext_x, M_XS, n_xt)
    stage(ext_b, M_BC, n_bt)
    for t in range(n_xt):
        tick_conv()
        xc_ref[:, t * LANE:(t + 1) * LANE] = conv_tile(ext_x, cwx_ref, cbx_ref, t)
    bcv = []
    for t in range(n_bt):
        tick_conv()
        bcv.append(conv_tile(ext_b, cwb_ref, cbb_ref, t))

    li = lax.broadcasted_iota(jnp.int32, (CHUNK, CHUNK), 0)
    si = lax.broadcasted_iota(jnp.int32, (CHUNK, CHUNK), 1)
    causal = si <= li
    x_dt_t = (src[:, M_DT:M_DT + LANE] + dtb_ref[...]).T[0:SSM_HEADS, :]
    dt_t = _softplus(x_dt_t)
    a_dt_t = dt_t * (-jnp.exp(alog_ref[0:SSM_HEADS, :]))
    upper = jnp.where(li <= si, 1.0, 0.0).astype(BF16)
    parts = _dot(jnp.concatenate(_split3(a_dt_t), axis=0), upper)
    a_cs_t = (parts[0:SSM_HEADS] + parts[SSM_HEADS:2 * SSM_HEADS]
              + parts[2 * SSM_HEADS:3 * SSM_HEADS])
    a2_t = a_cs_t * LOG2E
    a2_last = a2_t[:, CHUNK - 1:CHUNK]
    w_t = dt_t * jnp.exp2(a2_last - a2_t)
    cd_t = jnp.exp2(a2_last)
    a2 = jnp.concatenate([a2_t, jnp.zeros((CHUNK - SSM_HEADS, CHUNK), F32)], axis=0).T
    e2 = jnp.exp2(a2)
    low = lax.broadcasted_iota(jnp.int32, (CHUNK, LANE), 1) < SSM_HEAD_DIM
    low_row = low[0:1]
    keep_lo = jnp.where(low, 1.0, 0.0).astype(BF16)
    keep_hi = jnp.where(low, 0.0, 1.0).astype(BF16)
    gw = SSM_WIDTH // SSM_GROUPS

    zeros = jnp.zeros((CHUNK, SSM_STATE), BF16)
    cbs = []
    for g in range(0, SSM_GROUPS, 2):
        c2 = jnp.concatenate([bcv[SSM_GROUPS + g], bcv[SSM_GROUPS + g + 1]], axis=1).astype(BF16)
        b_bd = jnp.concatenate([jnp.concatenate([bcv[g].astype(BF16), zeros], axis=1),
                                jnp.concatenate([zeros, bcv[g + 1].astype(BF16)], axis=1)], axis=0)
        cb2 = _dot_nt(c2, b_bd)
        cbs += [cb2[:, 0:CHUNK], cb2[:, CHUNK:2 * CHUNK]]

    for g in range(SSM_GROUPS):
        b_g, c_g = bcv[g], bcv[SSM_GROUPS + g]
        cb = jnp.where(causal, cbs[g], 0.0)
        b_gt = b_g.T
        y_off = _dot(c_g.astype(BF16), h_ref[:, g * gw:(g + 1) * gw].astype(BF16))
        for pr in range(g * SSM_HPG // 2, (g + 1) * SSM_HPG // 2):
            tick()
            sl = slice(pr * LANE, (pr + 1) * LANE)
            x_pair = xc_ref[:, sl].astype(BF16)
            x_bd = jnp.concatenate([x_pair * keep_lo, x_pair * keep_hi], axis=0)
            m_cols, bw_cols, e_cols = [], [], []
            for j in (2 * pr, 2 * pr + 1):
                if j % 2:
                    tick()
                a_col = jnp.broadcast_to(a2[:, j:j + 1], (CHUNK, CHUNK))
                decay = jnp.exp2(jnp.minimum(a_col - a2_t[j:j + 1, :], 0.0))
                m_cols.append((cb * decay * dt_t[j:j + 1, :]).astype(BF16))
                bw_cols.append((b_gt * w_t[j:j + 1, :]).astype(BF16))
                e_cols.append(jnp.broadcast_to(e2[:, j:j + 1], (CHUNK, LANE)))
            y_diag = _dot(jnp.concatenate(m_cols, axis=1), x_bd)
            up = _dot(jnp.concatenate(bw_cols, axis=1), x_bd)
            k = pr - g * SSM_HPG // 2
            y_ref[:, sl] = y_diag + jnp.where(low, e_cols[0], e_cols[1]) * y_off[:, k * LANE:(k + 1) * LANE]
            cd_pair = jnp.where(low_row, cd_t[2 * pr:2 * pr + 1, :], cd_t[2 * pr + 1:2 * pr + 2, :])
            h_ref[:, sl] = h_ref[:, sl] * cd_pair + up

    out = []
    for g in range(SSM_GROUPS):
        tick()
        sl = slice(g * gw, (g + 1) * gw)
        yg = (y_ref[:, sl] + drow_ref[:, sl] * xc_ref[:, sl]) * _silu(src[:, M_ZM + g * gw:M_ZM + (g + 1) * gw])
        yg = yg * lax.rsqrt(jnp.mean(yg * yg, axis=-1, keepdims=True) + EPS)
        out.append(yg * ng_ref[:, sl])
    return jnp.concatenate(out, axis=-1)


def _ssd_kernel(x0_ref, xa_ref, xb_ref, xres_ref, acc_ref, gpre_ref, win_hbm, wbr_hbm, wout_hbm,
                gpost_ref, cwx_ref, cbx_ref, cwb_ref, cbb_ref, dtb_ref, alog_ref, drow_ref, ng_ref,
                o_ref, w_ref, wg_ref, wbr_ref, wout_ref, stage, stage_t, sem, slot_a, slot_b, hn_ref, ext_x, ext_b,
                h_ref, xc_ref, y_ref, ybf_ref, mg_ref, out_ref, *, nb, layer):
    i = pl.program_id(0)
    plan = _proj_plan([(w_ref, M_G), (wg_ref, D_MODEL)])

    @pl.when(i == 0)
    def _():
        ext_x[...] = jnp.zeros_like(ext_x)
        ext_b[...] = jnp.zeros_like(ext_b)
        _load_weights(_weight_jobs_t(win_hbm, layer, W_SSM_COL0, M_G, w_ref)
                      + _weight_jobs_t(win_hbm, layer, W_GATE_COL0 + 2 * D_MODEL, D_MODEL, wg_ref)
                      + _weight_jobs(wbr_hbm, layer, SSM_WIDTH, 0, D_MODEL, wbr_ref)
                      + _weight_jobs(wout_hbm, layer, D_MODEL, 0, D_MODEL, wout_ref), stage, stage_t, sem)
        _rms_proj(x0_ref, gpre_ref, slot_a, hn_ref, plan)

    n_ticks = SSM_HEADS + SSM_GROUPS
    out_cols = _col_chunks(D_MODEL, D_MODEL // 256)

    def epilogue(src, rows):
        def branch(c0, c1):
            gate = _sigmoid(src[:, M_G + c0:M_G + c1])
            mg_ref[:, c0:c1] = (acc_ref[rows, c0:c1]
                                + gate * _dot(ybf_ref[...], wbr_ref[:, c0:c1])).astype(BF16)

        def outproj(c0, c1):
            out_ref[:, c0:c1] = _dot(mg_ref[...], wout_ref[:, c0:c1])

        def finish():
            out = out_ref[...]
            ms = jnp.mean(out * out, axis=-1, keepdims=True)
            o_ref[rows, :] = xres_ref[rows, :] + out * lax.rsqrt(ms + EPS) * gpost_ref[...]

        return ([functools.partial(branch, c0, c1) for c0, c1 in out_cols]
                + [functools.partial(outproj, c0, c1) for c0, c1 in out_cols] + [finish])

    def mix(src, first, x_next_ref, dst, pending):
        _rms_norm_to(x_next_ref, gpre_ref, hn_ref)
        spread = _Spread(_proj_thunks(hn_ref, dst, plan), n_ticks)
        spread_conv = _Spread(pending, (SSM_WIDTH + SSM_BC) // LANE)
        y = _ssd_block(src, first, cwx_ref, cbx_ref, cwb_ref, cbb_ref, dtb_ref, alog_ref, drow_ref, ng_ref,
                       ext_x, ext_b, h_ref, xc_ref, y_ref, tick=spread.tick, tick_conv=spread_conv.tick)
        spread.flush()
        ybf_ref[...] = y.astype(BF16)

    first = ((2 * i) % nb) == 0
    mix(slot_a, first, xa_ref, slot_b, [])
    mix(slot_b, None, xb_ref, slot_a, epilogue(slot_a, slice(0, CHUNK)))
    for thunk in epilogue(slot_b, slice(CHUNK, 2 * CHUNK)):
        thunk()


def _x_specs(n_blk, per_step=2):
    nxt = lambda k: pl.BlockSpec((CHUNK, D_MODEL), lambda i: (jnp.minimum(per_step * i + k, n_blk - 1), 0))
    return [pl.BlockSpec((CHUNK, D_MODEL), lambda i: (0, 0))] + [nxt(k) for k in range(1, per_step + 1)]


def _const(shape):
    return pl.BlockSpec(shape, lambda i: (0,) * len(shape))


def _row_blocks(per_step=2):
    return pl.BlockSpec((per_step * CHUNK, D_MODEL), lambda i: (i, 0))


def _weight_scratch(shapes):
    return ([pltpu.VMEM(s, BF16) for s in shapes]
            + [pltpu.VMEM((2, STAGE_ROWS, STAGE_COLS), F32), pltpu.VMEM((2, STAGE_COLS, STAGE_ROWS), F32),
               pltpu.SemaphoreType.DMA((2,))])


def _any():
    return pl.BlockSpec(memory_space=pl.ANY)


def _att_branch(x2, g_pre, w_in_t, w_br, layer, rel_bias, sinks, bucket, nb):
    t = x2.shape[0]
    n_blk = t // CHUNK
    smem = pl.BlockSpec(memory_space=pltpu.SMEM)
    return pl.pallas_call(
        functools.partial(_att_kernel, nb=nb, layer=layer),
        grid=(n_blk // ATT_PER_STEP,),
        in_specs=[smem, smem, _const((CHUNK, 2 * CHUNK))] + _x_specs(n_blk, ATT_PER_STEP)
                 + [_const((1, D_MODEL)), _any(), _any()],
        out_specs=_row_blocks(ATT_PER_STEP),
        out_shape=jax.ShapeDtypeStruct((t, D_MODEL), F32),
        scratch_shapes=_weight_scratch([(D_MODEL, A_G), (D_MODEL, W_PAD_COLS), (ATT_WIDTH, W_PAD_COLS)])
                       + [pltpu.VMEM((CHUNK, A_COLS), F32), pltpu.VMEM((CHUNK, A_COLS), F32),
                        pltpu.VMEM((CHUNK, D_MODEL), BF16),
                        pltpu.VMEM((CHUNK, 2 * LANE), F32),
                        pltpu.VMEM((2, ATT_HEADS, CHUNK, CHUNK), F32),
                        pltpu.VMEM((CHUNK, D_MODEL), BF16)],
        compiler_params=_cparams(),
        name="att",
    )(rel_bias, sinks, bucket, *([x2] * (1 + ATT_PER_STEP)), g_pre.reshape(1, -1), w_in_t, w_br)


def _sg_branch(x2, acc, g_pre, w_in_t, w_br, layer, ln_g, ln_b, w_s, b_full):
    t = x2.shape[0]
    n_blk = t // CHUNK
    return pl.pallas_call(
        functools.partial(_sg_kernel, layer=layer),
        grid=(n_blk // SG_PER_STEP,),
        in_specs=_x_specs(n_blk, SG_PER_STEP) + [_row_blocks(SG_PER_STEP), _const((1, D_MODEL)), _any(), _any(),
                                    _const((1, SG_WIDTH)), _const((1, SG_WIDTH)),
                                    _const((SG_GROUPS, CHUNK, CHUNK)), _const((CHUNK, SG_WIDTH))],
        out_specs=_row_blocks(SG_PER_STEP),
        out_shape=jax.ShapeDtypeStruct((t, D_MODEL), F32),
        scratch_shapes=_weight_scratch([(D_MODEL, S_MAIN_COLS), (D_MODEL, W_PAD_COLS), (SG_WIDTH, W_PAD_COLS)])
                       + [pltpu.VMEM((CHUNK, S_COLS), F32), pltpu.VMEM((CHUNK, S_COLS), F32),
                        pltpu.VMEM((CHUNK, D_MODEL), BF16),
                        pltpu.VMEM((SG_GROUPS // 2, CHUNK, 2 * CHUNK), BF16),
                        pltpu.VMEM((CHUNK, SG_WIDTH), BF16)],
        compiler_params=_cparams(),
        name="sg",
    )(*([x2] * (1 + SG_PER_STEP)), acc, g_pre.reshape(1, -1), w_in_t, w_br, ln_g.reshape(1, -1), ln_b.reshape(1, -1),
      w_s, b_full)


def _ssd_branch(x2, acc, g_pre, w_in_t, w_br, w_out, layer, g_post, cwx, cbx, cwb, cbb, dtb, alog, drow,
                ng, nb):
    t = x2.shape[0]
    n_blk = t // CHUNK
    return pl.pallas_call(
        functools.partial(_ssd_kernel, nb=nb, layer=layer),
        grid=(n_blk // 2,),
        in_specs=_x_specs(n_blk) + [_row_blocks(), _row_blocks(), _const((1, D_MODEL)), _any(), _any(), _any(),
                                    _const((1, D_MODEL)),
                                    _const((SSM_CONV, 1, SSM_WIDTH)), _const((1, SSM_WIDTH)),
                                    _const((SSM_CONV, 1, SSM_BC)), _const((1, SSM_BC)),
                                    _const((1, LANE)), _const((LANE, 1)),
                                    _const((1, SSM_WIDTH)), _const((1, SSM_WIDTH))],
        out_specs=_row_blocks(),
        out_shape=jax.ShapeDtypeStruct((t, D_MODEL), F32),
        scratch_shapes=_weight_scratch([(D_MODEL, M_G), (D_MODEL, W_PAD_COLS), (SSM_WIDTH, W_PAD_COLS),
                                        (D_MODEL, W_PAD_COLS)])
                       + [pltpu.VMEM((CHUNK, M_COLS), F32), pltpu.VMEM((CHUNK, M_COLS), F32),
                        pltpu.VMEM((CHUNK, D_MODEL), BF16),
                        pltpu.VMEM((SSM_WIDTH // LANE, 8 + CHUNK, LANE), F32),
                        pltpu.VMEM((SSM_BC // LANE, 8 + CHUNK, LANE), F32),
                        pltpu.VMEM((SSM_STATE, SSM_WIDTH), F32),
                        pltpu.VMEM((CHUNK, SSM_WIDTH), F32),
                        pltpu.VMEM((CHUNK, SSM_WIDTH), F32),
                        pltpu.VMEM((CHUNK, SSM_WIDTH), BF16),
                        pltpu.VMEM((CHUNK, D_MODEL), BF16),
                        pltpu.VMEM((CHUNK, D_MODEL), F32)],
        compiler_params=_cparams(),
        name="ssd",
    )(x2, x2, x2, x2, acc, g_pre.reshape(1, -1), w_in_t, w_br, w_out, g_post.reshape(1, -1),
      cwx, cbx, cwb, cbb, dtb, alog, drow, ng)


def _rel_bucket_table():
    qi = jnp.arange(CHUNK, dtype=jnp.int32)[:, None]
    kj = jnp.arange(2 * CHUNK, dtype=jnp.int32)[None, :]
    dist = jnp.maximum(qi + CHUNK - kj, 0)
    max_exact = REL_BUCKETS // 2
    dist_f = jnp.maximum(dist, 1).astype(F32)
    large = max_exact + (jnp.log(dist_f / max_exact) / math.log(REL_MAX_DIST / max_exact)
                         * (REL_BUCKETS - max_exact)).astype(jnp.int32)
    large = jnp.minimum(large, REL_BUCKETS - 1)
    return jnp.where(dist < max_exact, dist, large)


def _pad_lanes(v):
    return jnp.pad(v, (0, LANE - v.shape[0])).reshape(1, LANE)


def kernel(x, w_in, norm_pre, norm_post, rel_bias, att_sinks, sg_ln_g, sg_ln_b, sg_w, sg_b, ssm_conv_w, ssm_conv_b, ssm_dt_bias, ssm_a_log, ssm_d, ssm_norm_g, w_br_att, w_br_sg, w_br_ssm, w_out):
    bsz, seq, d = x.shape
    depth = w_in.shape[0]
    assert d == D_MODEL and seq % (2 * CHUNK) == 0 and seq % (SG_PER_STEP * CHUNK) == 0
    nb = seq // CHUNK
    x2 = x.reshape(bsz * seq, d)
    bucket = _rel_bucket_table()
    w_in_t = jnp.swapaxes(w_in, 1, 2)
    for l in range(depth):
        acc = _att_branch(x2, norm_pre[l], w_in_t, w_br_att, l, rel_bias, att_sinks[l], bucket, nb)
        b_full = jnp.repeat(jnp.transpose(sg_b[l]), CHUNK, axis=1)
        acc = _sg_branch(x2, acc, norm_pre[l], w_in_t, w_br_sg, l, sg_ln_g[l], sg_ln_b[l], sg_w[l], b_full)
        cw, cb = ssm_conv_w[l], ssm_conv_b[l]
        x2 = _ssd_branch(x2, acc, norm_pre[l], w_in_t, w_br_ssm, w_out, l, norm_post[l],
                         cw[:, None, :SSM_WIDTH], cb[:SSM_WIDTH].reshape(1, -1),
                         cw[:, None, SSM_WIDTH:], cb[SSM_WIDTH:].reshape(1, -1),
                         _pad_lanes(ssm_dt_bias[l]), _pad_lanes(ssm_a_log[l]).reshape(LANE, 1),
                         jnp.repeat(ssm_d[l], SSM_HEAD_DIM).reshape(1, -1),
                         ssm_norm_g[l].reshape(1, -1), nb)
    return x2.reshape(bsz, seq, d)
```

```python
import functools
import math

import jax
import jax.numpy as jnp
from jax import lax
from jax.experimental import pallas as pl
from jax.experimental.pallas import tpu as pltpu

F32 = jnp.float32
BF16 = jnp.bfloat16

D_MODEL = 1024
ATT_HEADS = 16
ATT_KV_HEADS = 2
ATT_HEAD_DIM = 64
ATT_WIDTH = ATT_HEADS * ATT_HEAD_DIM
ATT_GROUP = ATT_HEADS // ATT_KV_HEADS
CHUNK = 128
REL_BUCKETS = 32
REL_MAX_DIST = 128
SG_GROUPS = 8
SG_WIDTH = 1024
SSM_WIDTH = 2048
SSM_HEAD_DIM = 64
SSM_HEADS = 32
SSM_GROUPS = 4
SSM_STATE = 128
SSM_HPG = SSM_HEADS // SSM_GROUPS
SSM_CONV = 4
SSM_BC = 2 * SSM_GROUPS * SSM_STATE
EPS = 1e-6
NEG = -1e30
LOG2E = 1.4426950408889634
LANE = 128

A_Q, A_K, A_V, A_ZA, A_G, A_COLS = 0, 1024, 1152, 1280, 2304, 3328
S_U, S_VS, S_ZS, S_G, S_COLS = 0, 1024, 2048, 3072, 4096
M_ZM, M_XS, M_BC, M_DT, M_G, M_COLS = 0, 2048, 4096, 5120, 5248, 6272
VMEM_LIMIT = 56 * 1024 * 1024
W_PAD_COLS = D_MODEL + LANE
S_MAIN_COLS = S_G + LANE
W_SG_COL0, W_SSM_COL0, W_GATE_COL0 = 2304, 5376, 10528
ATT_PER_STEP = SG_PER_STEP = 4


def _cparams():
    return pltpu.CompilerParams(dimension_semantics=("arbitrary",), vmem_limit_bytes=VMEM_LIMIT)


def _sigmoid(x):
    return 0.5 * jnp.tanh(0.5 * x) + 0.5


def _silu(x):
    h = 0.5 * x
    return h * jnp.tanh(h) + h


def _softplus(x):
    return jnp.maximum(x, 0.0) + jnp.log1p(jnp.exp(-jnp.abs(x)))


def _dot(a, b):
    return jnp.dot(a, b, preferred_element_type=F32)


def _dot_nt(a, b):
    return lax.dot_general(a, b, (((1,), (1,)), ((), ())), preferred_element_type=F32)


def _rms_norm_to(x_ref, g_ref, h_ref):
    x = x_ref[...]
    ms = jnp.mean(x * x, axis=-1, keepdims=True)
    h_ref[...] = (x * lax.rsqrt(ms + EPS) * g_ref[...]).astype(BF16)


def _proj_cols(h_ref, w_ref, dst_ref, c0, c1, d0):
    dst_ref[:, d0:d0 + c1 - c0] = _dot(h_ref[...], w_ref[:, c0:c1])


def _proj_plan(parts, width=256):
    plan, d0 = [], 0
    for w_ref, n_cols in parts:
        for c0 in range(0, n_cols, width):
            c1 = min(c0 + width, n_cols)
            plan.append((w_ref, c0, c1, d0 + c0))
        d0 += n_cols
    return plan


def _proj_thunks(h_ref, dst_ref, plan):
    return [functools.partial(_proj_cols, h_ref, w_ref, dst_ref, c0, c1, d0) for w_ref, c0, c1, d0 in plan]


def _rms_proj(x_ref, g_ref, dst_ref, h_ref, plan):
    _rms_norm_to(x_ref, g_ref, h_ref)
    for thunk in _proj_thunks(h_ref, dst_ref, plan):
        thunk()


STAGE_ROWS, STAGE_COLS = 1024, 512


def _weight_jobs(src_hbm, layer, k_rows, col0, n_cols, dst_ref):
    jobs = []
    for r0 in range(0, k_rows, STAGE_ROWS):
        for c in range(0, n_cols, STAGE_COLS):
            w = min(STAGE_COLS, n_cols - c)
            jobs.append((src_hbm.at[layer, r0:r0 + STAGE_ROWS, col0 + c:col0 + c + w], dst_ref, r0, c, w))
    return jobs


def _weight_jobs_t(src_hbm, layer, row0, n_rows, dst_ref):
    jobs = []
    for c in range(0, n_rows, STAGE_COLS):
        w = min(STAGE_COLS, n_rows - c)
        jobs.append((src_hbm.at[layer, row0 + c:row0 + c + w, :], dst_ref, None, c, w))
    return jobs


def _load_weights(jobs, stage, stage_t, sem):
    def dma(k):
        src, _, r0, _, w = jobs[k]
        dst = stage_t.at[k % 2, 0:w, :] if r0 is None else stage.at[k % 2, :, 0:w]
        return pltpu.make_async_copy(src, dst, sem.at[k % 2])

    dma(0).start()
    for k, (_, dst, r0, d0, w) in enumerate(jobs):
        if k + 1 < len(jobs):
            dma(k + 1).start()
        dma(k).wait()
        if r0 is None:
            dst[:, d0:d0 + w] = stage_t[k % 2, 0:w, :].T.astype(BF16)
        else:
            dst[r0:r0 + STAGE_ROWS, d0:d0 + w] = stage[k % 2, :, 0:w].astype(BF16)


class _Spread:
    def __init__(self, thunks, n_ticks):
        self.thunks, self.n_ticks, self.ticks, self.done = thunks, n_ticks, 0, 0

    def tick(self):
        self.ticks += 1
        while self.done < len(self.thunks) and self.done * self.n_ticks < self.ticks * len(self.thunks):
            self.thunks[self.done]()
            self.done += 1

    def flush(self):
        self.ticks = self.n_ticks
        self.tick()


def _col_chunks(n_cols, n_chunks, width=256):
    tiles = -(-n_cols // width)
    per = [tiles // n_chunks + (1 if k < tiles % n_chunks else 0) for k in range(n_chunks)]
    out, c = [], 0
    for n_tiles in per:
        out.append((c, min(c + n_tiles * width, n_cols)))
        c = min(c + n_tiles * width, n_cols)
    return out


def _split3(a):
    hi = a.astype(BF16)
    r1 = a - hi.astype(F32)
    mid = r1.astype(BF16)
    lo = (r1 - mid.astype(F32)).astype(BF16)
    return hi, mid, lo


def _att_block(src, kv_prev, bias_ref, sink_ref, variant, tick=lambda: None):
    qi = lax.broadcasted_iota(jnp.int32, (CHUNK, CHUNK), 0)
    ci = lax.broadcasted_iota(jnp.int32, (CHUNK, CHUNK), 1)
    own = ci <= qi
    low = lax.broadcasted_iota(jnp.int32, (2 * CHUNK, LANE), 1) < ATT_HEAD_DIM
    low_q = lax.broadcasted_iota(jnp.int32, (CHUNK, LANE), 1) < ATT_HEAD_DIM
    scale = ATT_HEAD_DIM ** -0.5
    kcat = jnp.concatenate([src[:, A_K:A_K + LANE], kv_prev[:, 0:LANE]], axis=0)
    vcat = jnp.concatenate([src[:, A_V:A_V + LANE], kv_prev[:, LANE:2 * LANE]], axis=0)
    k_roll = pltpu.roll(kcat, ATT_HEAD_DIM, 1)
    v_roll = pltpu.roll(vcat, ATT_HEAD_DIM, 1)
    kcat_t = jnp.concatenate([kcat[0:CHUNK].T, kcat[CHUNK:2 * CHUNK].T], axis=1)
    kroll_t = jnp.concatenate([k_roll[0:CHUNK].T, k_roll[CHUNK:2 * CHUNK].T], axis=1)
    low_t = lax.broadcasted_iota(jnp.int32, (LANE, 2 * CHUNK), 0) < ATT_HEAD_DIM
    n_pair = ATT_GROUP // 2
    pairs = []
    for kvh in range(ATT_KV_HEADS):
        k_lo, k_hi = (kcat_t, kroll_t) if kvh == 0 else (kroll_t, kcat_t)
        v_lo, v_hi = (vcat, v_roll) if kvh == 0 else (v_roll, vcat)
        kz = (jnp.where(low_t, k_lo, 0.0).astype(BF16), jnp.where(low_t, 0.0, k_hi).astype(BF16))
        vz = (jnp.where(low, v_lo, 0.0).astype(BF16), jnp.where(low, 0.0, v_hi).astype(BF16))
        pr0 = kvh * n_pair
        q4 = jnp.concatenate([src[:, A_Q + pr * LANE:A_Q + (pr + 1) * LANE]
                              for pr in range(pr0, pr0 + n_pair)], axis=0)
        q4 = (q4 * scale).astype(BF16)
        lhs_cols = []
        recips = [[None, None] for _ in range(n_pair)]
        for par in range(2):
            tick()
            s2 = _dot(q4, kz[par])
            rows = []
            for k in range(n_pair):
                h = 2 * (pr0 + k) + par
                s2k = s2[k * CHUNK:(k + 1) * CHUNK]
                s = jnp.where(own, s2k[:, 0:CHUNK], s2k[:, CHUNK:2 * CHUNK]) + bias_ref[variant, h]
                sink = sink_ref[h]
                m = jnp.maximum(jnp.max(s, axis=-1, keepdims=True), sink)
                p = jnp.exp(s - m)
                den = jnp.sum(p, axis=-1, keepdims=True) + jnp.exp(sink - m)
                recips[k][par] = 1.0 / den
                rows.append(jnp.concatenate([jnp.where(own, p, 0.0), jnp.where(own, 0.0, p)],
                                            axis=1).astype(BF16))
            lhs_cols.append(jnp.concatenate(rows, axis=0))
        tick()
        o4 = _dot(jnp.concatenate(lhs_cols, axis=1), jnp.concatenate(vz, axis=0))
        for k in range(n_pair):
            r_pair = jnp.where(low_q, jnp.broadcast_to(recips[k][0], (CHUNK, LANE)),
                               jnp.broadcast_to(recips[k][1], (CHUNK, LANE)))
            pairs.append(o4[k * CHUNK:(k + 1) * CHUNK] * r_pair)
    y = jnp.concatenate(pairs, axis=-1) * _silu(src[:, A_ZA:A_ZA + ATT_WIDTH])
    tick()
    tick()
    return y


def _att_kernel(rb_ref, sink_ref, bkt_ref, x0_ref, *refs, nb, layer):
    x_next, refs = refs[:ATT_PER_STEP], refs[ATT_PER_STEP:]
    (gpre_ref, win_hbm, wbr_hbm, o_ref, w_ref, wg_ref, wbr_ref, stage, stage_t, sem, slot_a, slot_b, h_ref,
     kv_prev, bias_ref, ybf_ref) = refs
    i = pl.program_id(0)
    plan = _proj_plan([(w_ref, A_G), (wg_ref, D_MODEL)])
    n_ticks = 2 * (ATT_KV_HEADS + 1) + 2

    @pl.when(i == 0)
    def _():
        qi = lax.broadcasted_iota(jnp.int32, (CHUNK, CHUNK), 0)
        ci = lax.broadcasted_iota(jnp.int32, (CHUNK, CHUNK), 1)
        own = ci <= qi
        bkt = jnp.where(own, bkt_ref[:, CHUNK:2 * CHUNK], bkt_ref[:, 0:CHUNK])
        for h in range(ATT_HEADS):
            acc = jnp.zeros((CHUNK, CHUNK), F32)
            for b in range(REL_BUCKETS):
                acc = jnp.where(bkt == b, rb_ref[b, h], acc)
            bias_ref[1, h] = acc
            bias_ref[0, h] = jnp.where(own, acc, NEG)
        kv_prev[...] = jnp.zeros_like(kv_prev)
        _load_weights(_weight_jobs_t(win_hbm, layer, 0, A_G, w_ref)
                      + _weight_jobs_t(win_hbm, layer, W_GATE_COL0, D_MODEL, wg_ref)
                      + _weight_jobs(wbr_hbm, layer, ATT_WIDTH, 0, D_MODEL, wbr_ref), stage, stage_t, sem)
        _rms_proj(x0_ref, gpre_ref, slot_a, h_ref, plan)

    def epilogue(src, rows):
        def branch(c0, c1):
            o_ref[rows, c0:c1] = (_sigmoid(src[:, A_G + c0:A_G + c1])
                                  * _dot(ybf_ref[...], wbr_ref[:, c0:c1]))
        return [functools.partial(branch, c0, c1) for c0, c1 in _col_chunks(D_MODEL, D_MODEL // 256)]

    def mix(src, variant, x_next_ref, dst, pending):
        _rms_norm_to(x_next_ref, gpre_ref, h_ref)
        spread = _Spread(pending + _proj_thunks(h_ref, dst, plan), n_ticks)
        y = _att_block(src, kv_prev, bias_ref, sink_ref, variant, tick=spread.tick)
        spread.flush()
        kv_prev[...] = src[:, A_K:A_K + 2 * LANE]
        ybf_ref[...] = y.astype(BF16)

    first = ((ATT_PER_STEP * i) % nb) == 0
    slots = (slot_a, slot_b)
    pending = []
    for k in range(ATT_PER_STEP):
        mix(slots[k % 2], jnp.where(first, 0, 1) if k == 0 else 1, x_next[k], slots[(k + 1) % 2], pending)
        pending = epilogue(slots[k % 2], slice(k * CHUNK, (k + 1) * CHUNK))
    for thunk in pending:
        thunk()


def _sg_block(src, lng_ref, lnb_ref, wt_ref, bfull_ref, tick=lambda: None):
    v = src[:, S_VS:S_VS + SG_WIDTH]
    mu = jnp.mean(v, axis=-1, keepdims=True)
    vc = v - mu
    var = jnp.mean(vc * vc, axis=-1, keepdims=True)
    vn = (vc * lax.rsqrt(var + EPS) * lng_ref[...] + lnb_ref[...]).astype(BF16)
    out = []
    zeros = jnp.zeros((CHUNK, LANE), BF16)
    for p in range(SG_GROUPS // 2):
        sl2 = slice(2 * p * LANE, (2 * p + 2) * LANE)
        v_bd = jnp.concatenate([jnp.concatenate([vn[:, 2 * p * LANE:(2 * p + 1) * LANE], zeros], axis=1),
                                jnp.concatenate([zeros, vn[:, (2 * p + 1) * LANE:(2 * p + 2) * LANE]], axis=1)],
                               axis=0)
        tick()
        mixed = _dot(wt_ref[p], v_bd) + bfull_ref[:, sl2]
        tick()
        out.append(src[:, S_U + 2 * p * LANE:S_U + (2 * p + 2) * LANE] * mixed
                   * _silu(src[:, S_ZS + 2 * p * LANE:S_ZS + (2 * p + 2) * LANE]))
    return jnp.concatenate(out, axis=-1)


def _sg_kernel(x0_ref, *refs, layer):
    x_next, refs = refs[:SG_PER_STEP], refs[SG_PER_STEP:]
    (acc_ref, gpre_ref, win_hbm, wbr_hbm, lng_ref, lnb_ref, ws_ref, bfull_ref, o_ref, w_ref, wg_ref, wbr_ref,
     stage, stage_t, sem, slot_a, slot_b, h_ref, wt_ref, ybf_ref) = refs
    i = pl.program_id(0)
    plan = _proj_plan([(w_ref, S_G), (wg_ref, D_MODEL)])

    @pl.when(i == 0)
    def _():
        ti = lax.broadcasted_iota(jnp.int32, (CHUNK, CHUNK), 0)
        si = lax.broadcasted_iota(jnp.int32, (CHUNK, CHUNK), 1)
        for p in range(SG_GROUPS // 2):
            wt_ref[p] = jnp.concatenate([jnp.where(si <= ti, ws_ref[2 * p], 0.0),
                                         jnp.where(si <= ti, ws_ref[2 * p + 1], 0.0)], axis=1).astype(BF16)
        _load_weights(_weight_jobs_t(win_hbm, layer, W_SG_COL0, S_G, w_ref)
                      + _weight_jobs_t(win_hbm, layer, W_GATE_COL0 + D_MODEL, D_MODEL, wg_ref)
                      + _weight_jobs(wbr_hbm, layer, SG_WIDTH, 0, D_MODEL, wbr_ref), stage, stage_t, sem)
        _rms_proj(x0_ref, gpre_ref, slot_a, h_ref, plan)

    def epilogue(src, rows):
        def branch(c0, c1):
            o_ref[rows, c0:c1] = (acc_ref[rows, c0:c1] + _sigmoid(src[:, S_G + c0:S_G + c1])
                                  * _dot(ybf_ref[...], wbr_ref[:, c0:c1]))
        return [functools.partial(branch, c0, c1) for c0, c1 in _col_chunks(D_MODEL, D_MODEL // 256)]

    def mix(src, x_next_ref, dst, pending):
        _rms_norm_to(x_next_ref, gpre_ref, h_ref)
        spread = _Spread(pending + _proj_thunks(h_ref, dst, plan), SG_GROUPS)
        y = _sg_block(src, lng_ref, lnb_ref, wt_ref, bfull_ref, tick=spread.tick)
        spread.flush()
        ybf_ref[...] = y.astype(BF16)

    slots = (slot_a, slot_b)
    pending = []
    for k in range(SG_PER_STEP):
        mix(slots[k % 2], x_next[k], slots[(k + 1) % 2], pending)
        pending = epilogue(slots[k % 2], slice(k * CHUNK, (k + 1) * CHUNK))
    for thunk in pending:
        thunk()


def _ssd_block(src, first, cwx_ref, cbx_ref, cwb_ref, cbb_ref, dtb_ref, alog_ref, drow_ref, ng_ref,
               ext_x, ext_b, h_ref, xc_ref, y_ref, tick=lambda: None, tick_conv=lambda: None):
    n_xt, n_bt = SSM_WIDTH // LANE, SSM_BC // LANE
    tail = slice(CHUNK, CHUNK + 8)

    if first is not None:
        @pl.when(first)
        def _():
            h_ref[...] = jnp.zeros_like(h_ref)

    def stage(ext, col0, n_tiles):
        for t in range(n_tiles):
            prev = ext[t, tail, :]
            ext[t, 0:8, :] = prev if first is None else jnp.where(first, 0.0, prev)
            ext[t, 8:8 + CHUNK, :] = src[:, col0 + t * LANE:col0 + (t + 1) * LANE]

    def conv_tile(ext, w_ref, b_ref, t):
        sl = slice(t * LANE, (t + 1) * LANE)
        acc = b_ref[:, sl]
        for k in range(SSM_CONV):
            acc = acc + w_ref[k, :, sl] * ext[t, pl.ds(8 - (SSM_CONV - 1) + k, CHUNK), :]
        return _silu(acc)

    stage(ext_x, M_XS, n_xt)
    stage(ext_b, M_BC, n_bt)
    for t in range(n_xt):
        tick_conv()
        xc_ref[:, t * LANE:(t + 1) * LANE] = conv_tile(ext_x, cwx_ref, cbx_ref, t)
    bcv = []
    for t in range(n_bt):
        tick_conv()
        bcv.append(conv_tile(ext_b, cwb_ref, cbb_ref, t))

    li = lax.broadcasted_iota(jnp.int32, (CHUNK, CHUNK), 0)
    si = lax.broadcasted_iota(jnp.int32, (CHUNK, CHUNK), 1)
    causal = si <= li
    x_dt_t = (src[:, M_DT:M_DT + LANE] + dtb_ref[...]).T[0:SSM_HEADS, :]
    dt_t = _softplus(x_dt_t)
    a_dt_t = dt_t * (-jnp.exp(alog_ref[0:SSM_HEADS, :]))
    upper = jnp.where(li <= si, 1.0, 0.0).astype(BF16)
    parts = _dot(jnp.concatenate(_split3(a_dt_t), axis=0), upper)
    a_cs_t = (parts[0:SSM_HEADS] + parts[SSM_HEADS:2 * SSM_HEADS]
              + parts[2 * SSM_HEADS:3 * SSM_HEADS])
    a2_t = a_cs_t * LOG2E
    a2_last = a2_t[:, CHUNK - 1:CHUNK]
    w_t = dt_t * jnp.exp2(a2_last - a2_t)
    cd_t = jnp.exp2(a2_last)
    a2 = jnp.concatenate([a2_t, jnp.zeros((CHUNK - SSM_HEADS, CHUNK), F32)], axis=0).T
    e2 = jnp.exp2(a2)
    low = lax.broadcasted_iota(jnp.int32, (CHUNK, LANE), 1) < SSM_HEAD_DIM
    low_row = low[0:1]
    keep_lo = jnp.where(low, 1.0, 0.0).astype(BF16)
    keep_hi = jnp.where(low, 0.0, 1.0).astype(BF16)
    gw = SSM_WIDTH // SSM_GROUPS

    zeros = jnp.zeros((SSM_STATE, CHUNK), BF16)
    b_gts = [bcv[g].T for g in range(SSM_GROUPS)]
    cbs = []
    for g in range(0, SSM_GROUPS, 2):
        c2 = jnp.concatenate([bcv[SSM_GROUPS + g], bcv[SSM_GROUPS + g + 1]], axis=1).astype(BF16)
        bt_bd = jnp.concatenate([jnp.concatenate([b_gts[g].astype(BF16), zeros], axis=1),
                                 jnp.concatenate([zeros, b_gts[g + 1].astype(BF16)], axis=1)], axis=0)
        cb2 = _dot(c2, bt_bd)
        cbs += [cb2[:, 0:CHUNK], cb2[:, CHUNK:2 * CHUNK]]

    for g in range(SSM_GROUPS):
        c_g = bcv[SSM_GROUPS + g]
        cb = jnp.where(causal, cbs[g], 0.0)
        b_gt = b_gts[g]
        y_off = _dot(c_g.astype(BF16), h_ref[:, g * gw:(g + 1) * gw].astype(BF16))
        for pr in range(g * SSM_HPG // 2, (g + 1) * SSM_HPG // 2):
            tick()
            sl = slice(pr * LANE, (pr + 1) * LANE)
            x_pair = xc_ref[:, sl].astype(BF16)
            x_bd = jnp.concatenate([x_pair * keep_lo, x_pair * keep_hi], axis=0)
            m_cols, bw_cols, e_cols = [], [], []
            for j in (2 * pr, 2 * pr + 1):
                if j % 2:
                    tick()
                a_col = jnp.broadcast_to(a2[:, j:j + 1], (CHUNK, CHUNK))
                decay = jnp.exp2(jnp.minimum(a_col - a2_t[j:j + 1, :], 0.0))
                m_cols.append((cb * decay * dt_t[j:j + 1, :]).astype(BF16))
                bw_cols.append((b_gt * w_t[j:j + 1, :]).astype(BF16))
                e_cols.append(jnp.broadcast_to(e2[:, j:j + 1], (CHUNK, LANE)))
            y_diag = _dot(jnp.concatenate(m_cols, axis=1), x_bd)
            up = _dot(jnp.concatenate(bw_cols, axis=1), x_bd)
            k = pr - g * SSM_HPG // 2
            y_ref[:, sl] = y_diag + jnp.where(low, e_cols[0], e_cols[1]) * y_off[:, k * LANE:(k + 1) * LANE]
            cd_pair = jnp.where(low_row, cd_t[2 * pr:2 * pr + 1, :], cd_t[2 * pr + 1:2 * pr + 2, :])
            h_ref[:, sl] = h_ref[:, sl] * cd_pair + up

    out = []
    for g in range(SSM_GROUPS):
        tick()
        sl = slice(g * gw, (g + 1) * gw)
        yg = (y_ref[:, sl] + drow_ref[:, sl] * xc_ref[:, sl]) * _silu(src[:, M_ZM + g * gw:M_ZM + (g + 1) * gw])
        yg = yg * lax.rsqrt(jnp.mean(yg * yg, axis=-1, keepdims=True) + EPS)
        out.append(yg * ng_ref[:, sl])
    return jnp.concatenate(out, axis=-1)


def _ssd_kernel(x0_ref, xa_ref, xb_ref, xres_ref, acc_ref, gpre_ref, win_hbm, wbr_hbm, wout_hbm,
                gpost_ref, cwx_ref, cbx_ref, cwb_ref, cbb_ref, dtb_ref, alog_ref, drow_ref, ng_ref,
                o_ref, w_ref, wg_ref, wbr_ref, wout_ref, stage, stage_t, sem, slot_a, slot_b, hn_ref, ext_x, ext_b,
                h_ref, xc_ref, y_ref, ybf_ref, mg_ref, out_ref, *, nb, layer):
    i = pl.program_id(0)
    plan = _proj_plan([(w_ref, M_G), (wg_ref, D_MODEL)])

    @pl.when(i == 0)
    def _():
        ext_x[...] = jnp.zeros_like(ext_x)
        ext_b[...] = jnp.zeros_like(ext_b)
        _load_weights(_weight_jobs_t(win_hbm, layer, W_SSM_COL0, M_G, w_ref)
                      + _weight_jobs_t(win_hbm, layer, W_GATE_COL0 + 2 * D_MODEL, D_MODEL, wg_ref)
                      + _weight_jobs(wbr_hbm, layer, SSM_WIDTH, 0, D_MODEL, wbr_ref)
                      + _weight_jobs(wout_hbm, layer, D_MODEL, 0, D_MODEL, wout_ref), stage, stage_t, sem)
        _rms_proj(x0_ref, gpre_ref, slot_a, hn_ref, plan)

    n_ticks = SSM_HEADS + SSM_GROUPS
    out_cols = _col_chunks(D_MODEL, D_MODEL // 256)

    def epilogue(src, rows):
        def branch(c0, c1):
            gate = _sigmoid(src[:, M_G + c0:M_G + c1])
            mg_ref[:, c0:c1] = (acc_ref[rows, c0:c1]
                                + gate * _dot(ybf_ref[...], wbr_ref[:, c0:c1])).astype(BF16)

        def outproj(c0, c1):
            out_ref[:, c0:c1] = _dot(mg_ref[...], wout_ref[:, c0:c1])

        def finish():
            out = out_ref[...]
            ms = jnp.mean(out * out, axis=-1, keepdims=True)
            o_ref[rows, :] = xres_ref[rows, :] + out * lax.rsqrt(ms + EPS) * gpost_ref[...]

        return ([functools.partial(branch, c0, c1) for c0, c1 in out_cols]
                + [functools.partial(outproj, c0, c1) for c0, c1 in out_cols] + [finish])

    def mix(src, first, x_next_ref, dst, pending):
        _rms_norm_to(x_next_ref, gpre_ref, hn_ref)
        spread = _Spread(_proj_thunks(hn_ref, dst, plan), n_ticks)
        spread_conv = _Spread(pending, (SSM_WIDTH + SSM_BC) // LANE)
        y = _ssd_block(src, first, cwx_ref, cbx_ref, cwb_ref, cbb_ref, dtb_ref, alog_ref, drow_ref, ng_ref,
                       ext_x, ext_b, h_ref, xc_ref, y_ref, tick=spread.tick, tick_conv=spread_conv.tick)
        spread.flush()
        ybf_ref[...] = y.astype(BF16)

    first = ((2 * i) % nb) == 0
    mix(slot_a, first, xa_ref, slot_b, [])
    mix(slot_b, None, xb_ref, slot_a, epilogue(slot_a, slice(0, CHUNK)))
    for thunk in epilogue(slot_b, slice(CHUNK, 2 * CHUNK)):
        thunk()


def _x_specs(n_blk, per_step=2):
    nxt = lambda k: pl.BlockSpec((CHUNK, D_MODEL), lambda i: (jnp.minimum(per_step * i + k, n_blk - 1), 0))
    return [pl.BlockSpec((CHUNK, D_MODEL), lambda i: (0, 0))] + [nxt(k) for k in range(1, per_step + 1)]


def _const(shape):
    return pl.BlockSpec(shape, lambda i: (0,) * len(shape))


def _row_blocks(per_step=2):
    return pl.BlockSpec((per_step * CHUNK, D_MODEL), lambda i: (i, 0))


def _weight_scratch(shapes):
    return ([pltpu.VMEM(s, BF16) for s in shapes]
            + [pltpu.VMEM((2, STAGE_ROWS, STAGE_COLS), F32), pltpu.VMEM((2, STAGE_COLS, STAGE_ROWS), F32),
               pltpu.SemaphoreType.DMA((2,))])


def _any():
    return pl.BlockSpec(memory_space=pl.ANY)


def _att_branch(x2, g_pre, w_in_t, w_br, layer, rel_bias, sinks, bucket, nb):
    t = x2.shape[0]
    n_blk = t // CHUNK
    smem = pl.BlockSpec(memory_space=pltpu.SMEM)
    return pl.pallas_call(
        functools.partial(_att_kernel, nb=nb, layer=layer),
        grid=(n_blk // ATT_PER_STEP,),
        in_specs=[smem, smem, _const((CHUNK, 2 * CHUNK))] + _x_specs(n_blk, ATT_PER_STEP)
                 + [_const((1, D_MODEL)), _any(), _any()],
        out_specs=_row_blocks(ATT_PER_STEP),
        out_shape=jax.ShapeDtypeStruct((t, D_MODEL), F32),
        scratch_shapes=_weight_scratch([(D_MODEL, A_G), (D_MODEL, W_PAD_COLS), (ATT_WIDTH, W_PAD_COLS)])
                       + [pltpu.VMEM((CHUNK, A_COLS), F32), pltpu.VMEM((CHUNK, A_COLS), F32),
                        pltpu.VMEM((CHUNK, D_MODEL), BF16),
                        pltpu.VMEM((CHUNK, 2 * LANE), F32),
                        pltpu.VMEM((2, ATT_HEADS, CHUNK, CHUNK), F32),
                        pltpu.VMEM((CHUNK, D_MODEL), BF16)],
        compiler_params=_cparams(),
        name="att",
    )(rel_bias, sinks, bucket, *([x2] * (1 + ATT_PER_STEP)), g_pre.reshape(1, -1), w_in_t, w_br)


def _sg_branch(x2, acc, g_pre, w_in_t, w_br, layer, ln_g, ln_b, w_s, b_full):
    t = x2.shape[0]
    n_blk = t // CHUNK
    return pl.pallas_call(
        functools.partial(_sg_kernel, layer=layer),
        grid=(n_blk // SG_PER_STEP,),
        in_specs=_x_specs(n_blk, SG_PER_STEP) + [_row_blocks(SG_PER_STEP), _const((1, D_MODEL)), _any(), _any(),
                                    _const((1, SG_WIDTH)), _const((1, SG_WIDTH)),
                                    _const((SG_GROUPS, CHUNK, CHUNK)), _const((CHUNK, SG_WIDTH))],
        out_specs=_row_blocks(SG_PER_STEP),
        out_shape=jax.ShapeDtypeStruct((t, D_MODEL), F32),
        scratch_shapes=_weight_scratch([(D_MODEL, S_MAIN_COLS), (D_MODEL, W_PAD_COLS), (SG_WIDTH, W_PAD_COLS)])
                       + [pltpu.VMEM((CHUNK, S_COLS), F32), pltpu.VMEM((CHUNK, S_COLS), F32),
                        pltpu.VMEM((CHUNK, D_MODEL), BF16),
                        pltpu.VMEM((SG_GROUPS // 2, CHUNK, 2 * CHUNK), BF16),
                        pltpu.VMEM((CHUNK, SG_WIDTH), BF16)],
        compiler_params=_cparams(),
        name="sg",
    )(*([x2] * (1 + SG_PER_STEP)), acc, g_pre.reshape(1, -1), w_in_t, w_br, ln_g.reshape(1, -1), ln_b.reshape(1, -1),
      w_s, b_full)


def _ssd_branch(x2, acc, g_pre, w_in_t, w_br, w_out, layer, g_post, cwx, cbx, cwb, cbb, dtb, alog, drow,
                ng, nb):
    t = x2.shape[0]
    n_blk = t // CHUNK
    return pl.pallas_call(
        functools.partial(_ssd_kernel, nb=nb, layer=layer),
        grid=(n_blk // 2,),
        in_specs=_x_specs(n_blk) + [_row_blocks(), _row_blocks(), _const((1, D_MODEL)), _any(), _any(), _any(),
                                    _const((1, D_MODEL)),
                                    _const((SSM_CONV, 1, SSM_WIDTH)), _const((1, SSM_WIDTH)),
                                    _const((SSM_CONV, 1, SSM_BC)), _const((1, SSM_BC)),
                                    _const((1, LANE)), _const((LANE, 1)),
                                    _const((1, SSM_WIDTH)), _const((1, SSM_WIDTH))],
        out_specs=_row_blocks(),
        out_shape=jax.ShapeDtypeStruct((t, D_MODEL), F32),
        scratch_shapes=_weight_scratch([(D_MODEL, M_G), (D_MODEL, W_PAD_COLS), (SSM_WIDTH, W_PAD_COLS),
                                        (D_MODEL, W_PAD_COLS)])
                       + [pltpu.VMEM((CHUNK, M_COLS), F32), pltpu.VMEM((CHUNK, M_COLS), F32),
                        pltpu.VMEM((CHUNK, D_MODEL), BF16),
                        pltpu.VMEM((SSM_WIDTH // LANE, 8 + CHUNK, LANE), F32),
                        pltpu.VMEM((SSM_BC // LANE, 8 + CHUNK, LANE), F32),
                        pltpu.VMEM((SSM_STATE, SSM_WIDTH), F32),
                        pltpu.VMEM((CHUNK, SSM_WIDTH), F32),
                        pltpu.VMEM((CHUNK, SSM_WIDTH), F32),
                        pltpu.VMEM((CHUNK, SSM_WIDTH), BF16),
                        pltpu.VMEM((CHUNK, D_MODEL), BF16),
                        pltpu.VMEM((CHUNK, D_MODEL), F32)],
        compiler_params=_cparams(),
        name="ssd",
    )(x2, x2, x2, x2, acc, g_pre.reshape(1, -1), w_in_t, w_br, w_out, g_post.reshape(1, -1),
      cwx, cbx, cwb, cbb, dtb, alog, drow, ng)


def _rel_bucket_table():
    qi = jnp.arange(CHUNK, dtype=jnp.int32)[:, None]
    kj = jnp.arange(2 * CHUNK, dtype=jnp.int32)[None, :]
    dist = jnp.maximum(qi + CHUNK - kj, 0)
    max_exact = REL_BUCKETS // 2
    dist_f = jnp.maximum(dist, 1).astype(F32)
    large = max_exact + (jnp.log(dist_f / max_exact) / math.log(REL_MAX_DIST / max_exact)
                         * (REL_BUCKETS - max_exact)).astype(jnp.int32)
    large = jnp.minimum(large, REL_BUCKETS - 1)
    return jnp.where(dist < max_exact, dist, large)


def _pad_lanes(v):
    return jnp.pad(v, (0, LANE - v.shape[0])).reshape(1, LANE)


def kernel(x, w_in, norm_pre, norm_post, rel_bias, att_sinks, sg_ln_g, sg_ln_b, sg_w, sg_b, ssm_conv_w, ssm_conv_b, ssm_dt_bias, ssm_a_log, ssm_d, ssm_norm_g, w_br_att, w_br_sg, w_br_ssm, w_out):
    bsz, seq, d = x.shape
    depth = w_in.shape[0]
    assert d == D_MODEL and seq % (2 * CHUNK) == 0 and seq % (SG_PER_STEP * CHUNK) == 0
    nb = seq // CHUNK
    x2 = x.reshape(bsz * seq, d)
    bucket = _rel_bucket_table()
    w_in_t = jnp.swapaxes(w_in, 1, 2)
    for l in range(depth):
        acc = _att_branch(x2, norm_pre[l], w_in_t, w_br_att, l, rel_bias, att_sinks[l], bucket, nb)
        b_full = jnp.repeat(jnp.transpose(sg_b[l]), CHUNK, axis=1)
        acc = _sg_branch(x2, acc, norm_pre[l], w_in_t, w_br_sg, l, sg_ln_g[l], sg_ln_b[l], sg_w[l], b_full)
        cw, cb = ssm_conv_w[l], ssm_conv_b[l]
        x2 = _ssd_branch(x2, acc, norm_pre[l], w_in_t, w_br_ssm, w_out, l, norm_post[l],
                         cw[:, None, :SSM_WIDTH], cb[:SSM_WIDTH].reshape(1, -1),
                         cw[:, None, SSM_WIDTH:], cb[SSM_WIDTH:].reshape(1, -1),
                         _pad_lanes(ssm_dt_bias[l]), _pad_lanes(ssm_a_log[l]).reshape(LANE, 1),
                         jnp.repeat(ssm_d[l], SSM_HEAD_DIM).reshape(1, -1),
                         ssm_norm_g[l].reshape(1, -1), nb)
    return x2.reshape(bsz, seq, d)
```

```python
import functools
import math

import jax
import jax.numpy as jnp
from jax import lax
from jax.experimental import pallas as pl
from jax.experimental.pallas import tpu as pltpu

F32 = jnp.float32
BF16 = jnp.bfloat16

D_MODEL = 1024
ATT_HEADS = 16
ATT_KV_HEADS = 2
ATT_HEAD_DIM = 64
ATT_WIDTH = ATT_HEADS * ATT_HEAD_DIM
ATT_GROUP = ATT_HEADS // ATT_KV_HEADS
CHUNK = 128
REL_BUCKETS = 32
REL_MAX_DIST = 128
SG_GROUPS = 8
SG_WIDTH = 1024
SSM_WIDTH = 2048
SSM_HEAD_DIM = 64
SSM_HEADS = 32
SSM_GROUPS = 4
SSM_STATE = 128
SSM_HPG = SSM_HEADS // SSM_GROUPS
SSM_CONV = 4
SSM_BC = 2 * SSM_GROUPS * SSM_STATE
EPS = 1e-6
NEG = -1e30
LOG2E = 1.4426950408889634
LANE = 128

A_Q, A_K, A_V, A_ZA, A_G, A_COLS = 0, 1024, 1152, 1280, 2304, 3328
S_U, S_VS, S_ZS, S_G, S_COLS = 0, 1024, 2048, 3072, 4096
M_ZM, M_XS, M_BC, M_DT, M_G, M_COLS = 0, 2048, 4096, 5120, 5248, 6272
VMEM_LIMIT = 56 * 1024 * 1024
W_PAD_COLS = D_MODEL + LANE
S_MAIN_COLS = S_G + LANE
W_SG_COL0, W_SSM_COL0, W_GATE_COL0 = 2304, 5376, 10528
ATT_PER_STEP = SG_PER_STEP = 4


def _cparams():
    return pltpu.CompilerParams(dimension_semantics=("arbitrary",), vmem_limit_bytes=VMEM_LIMIT)


def _sigmoid(x):
    return 0.5 * jnp.tanh(0.5 * x) + 0.5


def _silu(x):
    h = 0.5 * x
    return h * jnp.tanh(h) + h


def _softplus(x):
    return jnp.maximum(x, 0.0) + jnp.log1p(jnp.exp(-jnp.abs(x)))


def _dot(a, b):
    return jnp.dot(a, b, preferred_element_type=F32)


def _dot_nt(a, b):
    return lax.dot_general(a, b, (((1,), (1,)), ((), ())), preferred_element_type=F32)


def _rms_norm_to(x_ref, g_ref, h_ref):
    x = x_ref[...]
    ms = jnp.mean(x * x, axis=-1, keepdims=True)
    h_ref[...] = (x * lax.rsqrt(ms + EPS) * g_ref[...]).astype(BF16)


def _proj_cols(h_ref, w_ref, dst_ref, c0, c1, d0):
    dst_ref[:, d0:d0 + c1 - c0] = _dot(h_ref[...], w_ref[:, c0:c1])


def _proj_plan(parts, width=256):
    plan, d0 = [], 0
    for w_ref, n_cols in parts:
        for c0 in range(0, n_cols, width):
            c1 = min(c0 + width, n_cols)
            plan.append((w_ref, c0, c1, d0 + c0))
        d0 += n_cols
    return plan


def _proj_thunks(h_ref, dst_ref, plan):
    return [functools.partial(_proj_cols, h_ref, w_ref, dst_ref, c0, c1, d0) for w_ref, c0, c1, d0 in plan]


def _rms_proj(x_ref, g_ref, dst_ref, h_ref, plan):
    _rms_norm_to(x_ref, g_ref, h_ref)
    for thunk in _proj_thunks(h_ref, dst_ref, plan):
        thunk()


STAGE_ROWS, STAGE_COLS = 1024, 512


def _weight_jobs(src_hbm, layer, k_rows, col0, n_cols, dst_ref):
    jobs = []
    for r0 in range(0, k_rows, STAGE_ROWS):
        for c in range(0, n_cols, STAGE_COLS):
            w = min(STAGE_COLS, n_cols - c)
            jobs.append((src_hbm.at[layer, r0:r0 + STAGE_ROWS, col0 + c:col0 + c + w], dst_ref, r0, c, w))
    return jobs


def _weight_jobs_t(src_hbm, layer, row0, n_rows, dst_ref):
    jobs = []
    for c in range(0, n_rows, STAGE_COLS):
        w = min(STAGE_COLS, n_rows - c)
        jobs.append((src_hbm.at[layer, row0 + c:row0 + c + w, :], dst_ref, None, c, w))
    return jobs


def _load_weights(jobs, stage, stage_t, sem):
    def dma(k):
        src, _, r0, _, w = jobs[k]
        dst = stage_t.at[k % 2, 0:w, :] if r0 is None else stage.at[k % 2, :, 0:w]
        return pltpu.make_async_copy(src, dst, sem.at[k % 2])

    dma(0).start()
    for k, (_, dst, r0, d0, w) in enumerate(jobs):
        if k + 1 < len(jobs):
            dma(k + 1).start()
        dma(k).wait()
        if r0 is None:
            dst[:, d0:d0 + w] = stage_t[k % 2, 0:w, :].T.astype(BF16)
        else:
            dst[r0:r0 + STAGE_ROWS, d0:d0 + w] = stage[k % 2, :, 0:w].astype(BF16)


class _Spread:
    def __init__(self, thunks, n_ticks):
        self.thunks, self.n_ticks, self.ticks, self.done = thunks, n_ticks, 0, 0

    def tick(self):
        self.ticks += 1
        while self.done < len(self.thunks) and self.done * self.n_ticks < self.ticks * len(self.thunks):
            self.thunks[self.done]()
            self.done += 1

    def flush(self):
        self.ticks = self.n_ticks
        self.tick()


def _col_chunks(n_cols, n_chunks, width=256):
    tiles = -(-n_cols // width)
    per = [tiles // n_chunks + (1 if k < tiles % n_chunks else 0) for k in range(n_chunks)]
    out, c = [], 0
    for n_tiles in per:
        out.append((c, min(c + n_tiles * width, n_cols)))
        c = min(c + n_tiles * width, n_cols)
    return out


def _split3(a):
    hi = a.astype(BF16)
    r1 = a - hi.astype(F32)
    mid = r1.astype(BF16)
    lo = (r1 - mid.astype(F32)).astype(BF16)
    return hi, mid, lo


def _att_block(src, kv_prev, bias_ref, sink_ref, variant, tick=lambda: None):
    qi = lax.broadcasted_iota(jnp.int32, (CHUNK, CHUNK), 0)
    ci = lax.broadcasted_iota(jnp.int32, (CHUNK, CHUNK), 1)
    own = ci <= qi
    low = lax.broadcasted_iota(jnp.int32, (2 * CHUNK, LANE), 1) < ATT_HEAD_DIM
    low_q = lax.broadcasted_iota(jnp.int32, (CHUNK, LANE), 1) < ATT_HEAD_DIM
    scale = ATT_HEAD_DIM ** -0.5
    kcat = jnp.concatenate([src[:, A_K:A_K + LANE], kv_prev[:, 0:LANE]], axis=0)
    vcat = jnp.concatenate([src[:, A_V:A_V + LANE], kv_prev[:, LANE:2 * LANE]], axis=0)
    k_roll = pltpu.roll(kcat, ATT_HEAD_DIM, 1)
    v_roll = pltpu.roll(vcat, ATT_HEAD_DIM, 1)
    n_pair = ATT_GROUP // 2
    pairs = []
    for kvh in range(ATT_KV_HEADS):
        k_lo, k_hi = (kcat, k_roll) if kvh == 0 else (k_roll, kcat)
        v_lo, v_hi = (vcat, v_roll) if kvh == 0 else (v_roll, vcat)
        kz = (jnp.where(low, k_lo, 0.0).astype(BF16), jnp.where(low, 0.0, k_hi).astype(BF16))
        vz = (jnp.where(low, v_lo, 0.0).astype(BF16), jnp.where(low, 0.0, v_hi).astype(BF16))
        pr0 = kvh * n_pair
        q4 = jnp.concatenate([src[:, A_Q + pr * LANE:A_Q + (pr + 1) * LANE]
                              for pr in range(pr0, pr0 + n_pair)], axis=0)
        q4 = (q4 * scale).astype(BF16)
        lhs_cols = []
        recips = [[None, None] for _ in range(n_pair)]
        for par in range(2):
            tick()
            s2 = _dot_nt(q4, kz[par])
            rows = []
            for k in range(n_pair):
                h = 2 * (pr0 + k) + par
                s2k = s2[k * CHUNK:(k + 1) * CHUNK]
                s = jnp.where(own, s2k[:, 0:CHUNK], s2k[:, CHUNK:2 * CHUNK]) + bias_ref[variant, h]
                sink = sink_ref[h]
                m = jnp.maximum(jnp.max(s, axis=-1, keepdims=True), sink)
                p = jnp.exp(s - m)
                den = jnp.sum(p, axis=-1, keepdims=True) + jnp.exp(sink - m)
                recips[k][par] = 1.0 / den
                rows.append(jnp.concatenate([jnp.where(own, p, 0.0), jnp.where(own, 0.0, p)],
                                            axis=1).astype(BF16))
            lhs_cols.append(jnp.concatenate(rows, axis=0))
        tick()
        o4 = _dot(jnp.concatenate(lhs_cols, axis=1), jnp.concatenate(vz, axis=0))
        for k in range(n_pair):
            r_pair = jnp.where(low_q, jnp.broadcast_to(recips[k][0], (CHUNK, LANE)),
                               jnp.broadcast_to(recips[k][1], (CHUNK, LANE)))
            pairs.append(o4[k * CHUNK:(k + 1) * CHUNK] * r_pair)
    y = jnp.concatenate(pairs, axis=-1) * _silu(src[:, A_ZA:A_ZA + ATT_WIDTH])
    tick()
    tick()
    return y


def _att_kernel(rb_ref, sink_ref, bkt_ref, x0_ref, *refs, nb, layer):
    x_next, refs = refs[:ATT_PER_STEP], refs[ATT_PER_STEP:]
    (gpre_ref, win_hbm, wbr_hbm, o_ref, w_ref, wg_ref, wbr_ref, stage, stage_t, sem, slot_a, slot_b, h_ref,
     kv_prev, bias_ref, ybf_ref) = refs
    i = pl.program_id(0)
    plan = _proj_plan([(w_ref, A_G), (wg_ref, D_MODEL)])
    n_ticks = 2 * (ATT_KV_HEADS + 1) + 2

    @pl.when(i == 0)
    def _():
        qi = lax.broadcasted_iota(jnp.int32, (CHUNK, CHUNK), 0)
        ci = lax.broadcasted_iota(jnp.int32, (CHUNK, CHUNK), 1)
        own = ci <= qi
        bkt = jnp.where(own, bkt_ref[:, CHUNK:2 * CHUNK], bkt_ref[:, 0:CHUNK])
        for h in range(ATT_HEADS):
            acc = jnp.zeros((CHUNK, CHUNK), F32)
            for b in range(REL_BUCKETS):
                acc = jnp.where(bkt == b, rb_ref[b, h], acc)
            bias_ref[1, h] = acc
            bias_ref[0, h] = jnp.where(own, acc, NEG)
        kv_prev[...] = jnp.zeros_like(kv_prev)
        _load_weights(_weight_jobs_t(win_hbm, layer, 0, A_G, w_ref)
                      + _weight_jobs_t(win_hbm, layer, W_GATE_COL0, D_MODEL, wg_ref)
                      + _weight_jobs(wbr_hbm, layer, ATT_WIDTH, 0, D_MODEL, wbr_ref), stage, stage_t, sem)
        _rms_proj(x0_ref, gpre_ref, slot_a, h_ref, plan)

    def epilogue(src, rows):
        def branch(c0, c1):
            o_ref[rows, c0:c1] = (_sigmoid(src[:, A_G + c0:A_G + c1])
                                  * _dot(ybf_ref[...], wbr_ref[:, c0:c1]))
        return [functools.partial(branch, c0, c1) for c0, c1 in _col_chunks(D_MODEL, D_MODEL // 256)]

    def mix(src, variant, x_next_ref, dst, pending):
        _rms_norm_to(x_next_ref, gpre_ref, h_ref)
        spread = _Spread(pending + _proj_thunks(h_ref, dst, plan), n_ticks)
        y = _att_block(src, kv_prev, bias_ref, sink_ref, variant, tick=spread.tick)
        spread.flush()
        kv_prev[...] = src[:, A_K:A_K + 2 * LANE]
        ybf_ref[...] = y.astype(BF16)

    first = ((ATT_PER_STEP * i) % nb) == 0
    slots = (slot_a, slot_b)
    pending = []
    for k in range(ATT_PER_STEP):
        mix(slots[k % 2], jnp.where(first, 0, 1) if k == 0 else 1, x_next[k], slots[(k + 1) % 2], pending)
        pending = epilogue(slots[k % 2], slice(k * CHUNK, (k + 1) * CHUNK))
    for thunk in pending:
        thunk()


def _sg_block(src, lng_ref, lnb_ref, wt_ref, bfull_ref, tick=lambda: None):
    v = src[:, S_VS:S_VS + SG_WIDTH]
    mu = jnp.mean(v, axis=-1, keepdims=True)
    vc = v - mu
    var = jnp.mean(vc * vc, axis=-1, keepdims=True)
    vn = (vc * lax.rsqrt(var + EPS) * lng_ref[...] + lnb_ref[...]).astype(BF16)
    out = []
    zeros = jnp.zeros((CHUNK, LANE), BF16)
    for p in range(SG_GROUPS // 2):
        sl2 = slice(2 * p * LANE, (2 * p + 2) * LANE)
        v_bd = jnp.concatenate([jnp.concatenate([vn[:, 2 * p * LANE:(2 * p + 1) * LANE], zeros], axis=1),
                                jnp.concatenate([zeros, vn[:, (2 * p + 1) * LANE:(2 * p + 2) * LANE]], axis=1)],
                               axis=0)
        tick()
        mixed = _dot(wt_ref[p], v_bd) + bfull_ref[:, sl2]
        tick()
        out.append(src[:, S_U + 2 * p * LANE:S_U + (2 * p + 2) * LANE] * mixed
                   * _silu(src[:, S_ZS + 2 * p * LANE:S_ZS + (2 * p + 2) * LANE]))
    return jnp.concatenate(out, axis=-1)


def _sg_kernel(x0_ref, *refs, layer):
    x_next, refs = refs[:SG_PER_STEP], refs[SG_PER_STEP:]
    (acc_ref, gpre_ref, win_hbm, wbr_hbm, lng_ref, lnb_ref, ws_ref, bfull_ref, o_ref, w_ref, wg_ref, wbr_ref,
     stage, stage_t, sem, slot_a, slot_b, h_ref, wt_ref, ybf_ref) = refs
    i = pl.program_id(0)
    plan = _proj_plan([(w_ref, S_G), (wg_ref, D_MODEL)])

    @pl.when(i == 0)
    def _():
        ti = lax.broadcasted_iota(jnp.int32, (CHUNK, CHUNK), 0)
        si = lax.broadcasted_iota(jnp.int32, (CHUNK, CHUNK), 1)
        for p in range(SG_GROUPS // 2):
            wt_ref[p] = jnp.concatenate([jnp.where(si <= ti, ws_ref[2 * p], 0.0),
                                         jnp.where(si <= ti, ws_ref[2 * p + 1], 0.0)], axis=1).astype(BF16)
        _load_weights(_weight_jobs_t(win_hbm, layer, W_SG_COL0, S_G, w_ref)
                      + _weight_jobs_t(win_hbm, layer, W_GATE_COL0 + D_MODEL, D_MODEL, wg_ref)
                      + _weight_jobs(wbr_hbm, layer, SG_WIDTH, 0, D_MODEL, wbr_ref), stage, stage_t, sem)
        _rms_proj(x0_ref, gpre_ref, slot_a, h_ref, plan)

    def epilogue(src, rows):
        def branch(c0, c1):
            o_ref[rows, c0:c1] = (acc_ref[rows, c0:c1] + _sigmoid(src[:, S_G + c0:S_G + c1])
                                  * _dot(ybf_ref[...], wbr_ref[:, c0:c1]))
        return [functools.partial(branch, c0, c1) for c0, c1 in _col_chunks(D_MODEL, D_MODEL // 256)]

    def mix(src, x_next_ref, dst, pending):
        _rms_norm_to(x_next_ref, gpre_ref, h_ref)
        spread = _Spread(pending + _proj_thunks(h_ref, dst, plan), SG_GROUPS)
        y = _sg_block(src, lng_ref, lnb_ref, wt_ref, bfull_ref, tick=spread.tick)
        spread.flush()
        ybf_ref[...] = y.astype(BF16)

    slots = (slot_a, slot_b)
    pending = []
    for k in range(SG_PER_STEP):
        mix(slots[k % 2], x_next[k], slots[(k + 1) % 2], pending)
        pending = epilogue(slots[k % 2], slice(k * CHUNK, (k + 1) * CHUNK))
    for thunk in pending:
        thunk()


def _attsg_kernel(rb_ref, sink_ref, bkt_ref, x0_ref, *refs, nb, layer):
    x_next, refs = refs[:ATT_PER_STEP], refs[ATT_PER_STEP:]
    (gpre_ref, win_hbm, wbra_hbm, wbrs_hbm, lng_ref, lnb_ref, ws_ref, bfull_ref, o_ref,
     wa_ref, wga_ref, wbra_ref, wsg_ref, wgs_ref, wbrs_ref, stage, stage_t, sem,
     slot_a, slot_b, h_ref, kv_prev, bias_ref, ybf_ref, sslot_a, sslot_b, sh_ref, wt_ref, sybf_ref) = refs
    _att_kernel(rb_ref, sink_ref, bkt_ref, x0_ref, *x_next, gpre_ref, win_hbm, wbra_hbm, o_ref,
                wa_ref, wga_ref, wbra_ref, stage, stage_t, sem, slot_a, slot_b, h_ref, kv_prev, bias_ref, ybf_ref,
                nb=nb, layer=layer)
    _sg_kernel(x0_ref, *x_next, o_ref, gpre_ref, win_hbm, wbrs_hbm, lng_ref, lnb_ref, ws_ref, bfull_ref, o_ref,
               wsg_ref, wgs_ref, wbrs_ref, stage, stage_t, sem, sslot_a, sslot_b, sh_ref, wt_ref, sybf_ref,
               layer=layer)


def _ssd_block(src, first, cwx_ref, cbx_ref, cwb_ref, cbb_ref, dtb_ref, alog_ref, drow_ref, ng_ref,
               ext_x, ext_b, h_ref, xc_ref, y_ref, tick=lambda: None, tick_conv=lambda: None):
    n_xt, n_bt = SSM_WIDTH // LANE, SSM_BC // LANE
    tail = slice(CHUNK, CHUNK + 8)

    if first is not None:
        @pl.when(first)
        def _():
            h_ref[...] = jnp.zeros_like(h_ref)

    def stage(ext, col0, n_tiles):
        for t in range(n_tiles):
            prev = ext[t, tail, :]
            ext[t, 0:8, :] = prev if first is None else jnp.where(first, 0.0, prev)
            ext[t, 8:8 + CHUNK, :] = src[:, col0 + t * LANE:col0 + (t + 1) * LANE]

    def conv_tile(ext, w_ref, b_ref, t):
        sl = slice(t * LANE, (t + 1) * LANE)
        acc = b_ref[:, sl]
        for k in range(SSM_CONV):
            acc = acc + w_ref[k, :, sl] * ext[t, pl.ds(8 - (SSM_CONV - 1) + k, CHUNK), :]
        return _silu(acc)

    stage(ext_x, M_XS, n_xt)
    stage(ext_b, M_BC, n_bt)
    for t in range(n_xt):
        tick_conv()
        xc_ref[:, t * LANE:(t + 1) * LANE] = conv_tile(ext_x, cwx_ref, cbx_ref, t)
    bcv = []
    for t in range(n_bt):
        tick_conv()
        bcv.append(conv_tile(ext_b, cwb_ref, cbb_ref, t))

    li = lax.broadcasted_iota(jnp.int32, (CHUNK, CHUNK), 0)
    si = lax.broadcasted_iota(jnp.int32, (CHUNK, CHUNK), 1)
    causal = si <= li
    x_dt_t = (src[:, M_DT:M_DT + LANE] + dtb_ref[...]).T[0:SSM_HEADS, :]
    dt_t = _softplus(x_dt_t)
    a_dt_t = dt_t * (-jnp.exp(alog_ref[0:SSM_HEADS, :]))
    upper = jnp.where(li <= si, 1.0, 0.0).astype(BF16)
    parts = _dot(jnp.concatenate(_split3(a_dt_t), axis=0), upper)
    a_cs_t = (parts[0:SSM_HEADS] + parts[SSM_HEADS:2 * SSM_HEADS]
              + parts[2 * SSM_HEADS:3 * SSM_HEADS])
    a2_t = a_cs_t * LOG2E
    a2_last = a2_t[:, CHUNK - 1:CHUNK]
    w_t = dt_t * jnp.exp2(a2_last - a2_t)
    cd_t = jnp.exp2(a2_last)
    a2 = jnp.concatenate([a2_t, jnp.zeros((CHUNK - SSM_HEADS, CHUNK), F32)], axis=0).T
    e2 = jnp.exp2(a2)
    low = lax.broadcasted_iota(jnp.int32, (CHUNK, LANE), 1) < SSM_HEAD_DIM
    low_row = low[0:1]
    keep_lo = jnp.where(low, 1.0, 0.0).astype(BF16)
    keep_hi = jnp.where(low, 0.0, 1.0).astype(BF16)
    gw = SSM_WIDTH // SSM_GROUPS

    zeros = jnp.zeros((CHUNK, SSM_STATE), BF16)
    cbs = []
    for g in range(0, SSM_GROUPS, 2):
        c2 = jnp.concatenate([bcv[SSM_GROUPS + g], bcv[SSM_GROUPS + g + 1]], axis=1).astype(BF16)
        b_bd = jnp.concatenate([jnp.concatenate([bcv[g].astype(BF16), zeros], axis=1),
                                jnp.concatenate([zeros, bcv[g + 1].astype(BF16)], axis=1)], axis=0)
        cb2 = _dot_nt(c2, b_bd)
        cbs += [cb2[:, 0:CHUNK], cb2[:, CHUNK:2 * CHUNK]]

    for g in range(SSM_GROUPS):
        b_g, c_g = bcv[g], bcv[SSM_GROUPS + g]
        cb = jnp.where(causal, cbs[g], 0.0)
        b_gt = b_g.T
        y_off = _dot(c_g.astype(BF16), h_ref[:, g * gw:(g + 1) * gw].astype(BF16))
        for pr in range(g * SSM_HPG // 2, (g + 1) * SSM_HPG // 2):
            tick()
            sl = slice(pr * LANE, (pr + 1) * LANE)
            x_pair = xc_ref[:, sl].astype(BF16)
            x_bd = jnp.concatenate([x_pair * keep_lo, x_pair * keep_hi], axis=0)
            m_cols, bw_cols, e_cols = [], [], []
            for j in (2 * pr, 2 * pr + 1):
                if j % 2:
                    tick()
                a_col = jnp.broadcast_to(a2[:, j:j + 1], (CHUNK, CHUNK))
                decay = jnp.exp2(jnp.minimum(a_col - a2_t[j:j + 1, :], 0.0))
                m_cols.append((cb * decay * dt_t[j:j + 1, :]).astype(BF16))
                bw_cols.append((b_gt * w_t[j:j + 1, :]).astype(BF16))
                e_cols.append(jnp.broadcast_to(e2[:, j:j + 1], (CHUNK, LANE)))
            y_diag = _dot(jnp.concatenate(m_cols, axis=1), x_bd)
            up = _dot(jnp.concatenate(bw_cols, axis=1), x_bd)
            k = pr - g * SSM_HPG // 2
            y_ref[:, sl] = y_diag + jnp.where(low, e_cols[0], e_cols[1]) * y_off[:, k * LANE:(k + 1) * LANE]
            cd_pair = jnp.where(low_row, cd_t[2 * pr:2 * pr + 1, :], cd_t[2 * pr + 1:2 * pr + 2, :])
            h_ref[:, sl] = h_ref[:, sl] * cd_pair + up

    out = []
    for g in range(SSM_GROUPS):
        tick()
        sl = slice(g * gw, (g + 1) * gw)
        yg = (y_ref[:, sl] + drow_ref[:, sl] * xc_ref[:, sl]) * _silu(src[:, M_ZM + g * gw:M_ZM + (g + 1) * gw])
        yg = yg * lax.rsqrt(jnp.mean(yg * yg, axis=-1, keepdims=True) + EPS)
        out.append(yg * ng_ref[:, sl])
    return jnp.concatenate(out, axis=-1)


def _ssd_kernel(x0_ref, xa_ref, xb_ref, xres_ref, acc_ref, gpre_ref, win_hbm, wbr_hbm, wout_hbm,
                gpost_ref, cwx_ref, cbx_ref, cwb_ref, cbb_ref, dtb_ref, alog_ref, drow_ref, ng_ref,
                o_ref, w_ref, wg_ref, wbr_ref, wout_ref, stage, stage_t, sem, slot_a, slot_b, hn_ref, ext_x, ext_b,
                h_ref, xc_ref, y_ref, ybf_ref, mg_ref, out_ref, *, nb, layer):
    i = pl.program_id(0)
    plan = _proj_plan([(w_ref, M_G), (wg_ref, D_MODEL)])

    @pl.when(i == 0)
    def _():
        ext_x[...] = jnp.zeros_like(ext_x)
        ext_b[...] = jnp.zeros_like(ext_b)
        _load_weights(_weight_jobs_t(win_hbm, layer, W_SSM_COL0, M_G, w_ref)
                      + _weight_jobs_t(win_hbm, layer, W_GATE_COL0 + 2 * D_MODEL, D_MODEL, wg_ref)
                      + _weight_jobs(wbr_hbm, layer, SSM_WIDTH, 0, D_MODEL, wbr_ref)
                      + _weight_jobs(wout_hbm, layer, D_MODEL, 0, D_MODEL, wout_ref), stage, stage_t, sem)
        _rms_proj(x0_ref, gpre_ref, slot_a, hn_ref, plan)

    n_ticks = SSM_HEADS + SSM_GROUPS
    out_cols = _col_chunks(D_MODEL, D_MODEL // 256)

    def epilogue(src, rows):
        def branch(c0, c1):
            gate = _sigmoid(src[:, M_G + c0:M_G + c1])
            mg_ref[:, c0:c1] = (acc_ref[rows, c0:c1]
                                + gate * _dot(ybf_ref[...], wbr_ref[:, c0:c1])).astype(BF16)

        def outproj(c0, c1):
            out_ref[:, c0:c1] = _dot(mg_ref[...], wout_ref[:, c0:c1])

        def finish():
            out = out_ref[...]
            ms = jnp.mean(out * out, axis=-1, keepdims=True)
            o_ref[rows, :] = xres_ref[rows, :] + out * lax.rsqrt(ms + EPS) * gpost_ref[...]

        return ([functools.partial(branch, c0, c1) for c0, c1 in out_cols]
                + [functools.partial(outproj, c0, c1) for c0, c1 in out_cols] + [finish])

    def mix(src, first, x_next_ref, dst, pending):
        _rms_norm_to(x_next_ref, gpre_ref, hn_ref)
        spread = _Spread(_proj_thunks(hn_ref, dst, plan), n_ticks)
        spread_conv = _Spread(pending, (SSM_WIDTH + SSM_BC) // LANE)
        y = _ssd_block(src, first, cwx_ref, cbx_ref, cwb_ref, cbb_ref, dtb_ref, alog_ref, drow_ref, ng_ref,
                       ext_x, ext_b, h_ref, xc_ref, y_ref, tick=spread.tick, tick_conv=spread_conv.tick)
        spread.flush()
        ybf_ref[...] = y.astype(BF16)

    first = ((2 * i) % nb) == 0
    mix(slot_a, first, xa_ref, slot_b, [])
    mix(slot_b, None, xb_ref, slot_a, epilogue(slot_a, slice(0, CHUNK)))
    for thunk in epilogue(slot_b, slice(CHUNK, 2 * CHUNK)):
        thunk()


def _x_specs(n_blk, per_step=2):
    nxt = lambda k: pl.BlockSpec((CHUNK, D_MODEL), lambda i: (jnp.minimum(per_step * i + k, n_blk - 1), 0))
    return [pl.BlockSpec((CHUNK, D_MODEL), lambda i: (0, 0))] + [nxt(k) for k in range(1, per_step + 1)]


def _const(shape):
    return pl.BlockSpec(shape, lambda i: (0,) * len(shape))


def _row_blocks(per_step=2):
    return pl.BlockSpec((per_step * CHUNK, D_MODEL), lambda i: (i, 0))


def _weight_scratch(shapes):
    return ([pltpu.VMEM(s, BF16) for s in shapes]
            + [pltpu.VMEM((2, STAGE_ROWS, STAGE_COLS), F32), pltpu.VMEM((2, STAGE_COLS, STAGE_ROWS), F32),
               pltpu.SemaphoreType.DMA((2,))])


def _any():
    return pl.BlockSpec(memory_space=pl.ANY)


def _att_branch(x2, g_pre, w_in_t, w_br, layer, rel_bias, sinks, bucket, nb):
    t = x2.shape[0]
    n_blk = t // CHUNK
    smem = pl.BlockSpec(memory_space=pltpu.SMEM)
    return pl.pallas_call(
        functools.partial(_att_kernel, nb=nb, layer=layer),
        grid=(n_blk // ATT_PER_STEP,),
        in_specs=[smem, smem, _const((CHUNK, 2 * CHUNK))] + _x_specs(n_blk, ATT_PER_STEP)
                 + [_const((1, D_MODEL)), _any(), _any()],
        out_specs=_row_blocks(ATT_PER_STEP),
        out_shape=jax.ShapeDtypeStruct((t, D_MODEL), F32),
        scratch_shapes=_weight_scratch([(D_MODEL, A_G), (D_MODEL, W_PAD_COLS), (ATT_WIDTH, W_PAD_COLS)])
                       + [pltpu.VMEM((CHUNK, A_COLS), F32), pltpu.VMEM((CHUNK, A_COLS), F32),
                        pltpu.VMEM((CHUNK, D_MODEL), BF16),
                        pltpu.VMEM((CHUNK, 2 * LANE), F32),
                        pltpu.VMEM((2, ATT_HEADS, CHUNK, CHUNK), F32),
                        pltpu.VMEM((CHUNK, D_MODEL), BF16)],
        compiler_params=_cparams(),
        name="att",
    )(rel_bias, sinks, bucket, *([x2] * (1 + ATT_PER_STEP)), g_pre.reshape(1, -1), w_in_t, w_br)


def _sg_branch(x2, acc, g_pre, w_in_t, w_br, layer, ln_g, ln_b, w_s, b_full):
    t = x2.shape[0]
    n_blk = t // CHUNK
    return pl.pallas_call(
        functools.partial(_sg_kernel, layer=layer),
        grid=(n_blk // SG_PER_STEP,),
        in_specs=_x_specs(n_blk, SG_PER_STEP) + [_row_blocks(SG_PER_STEP), _const((1, D_MODEL)), _any(), _any(),
                                    _const((1, SG_WIDTH)), _const((1, SG_WIDTH)),
                                    _const((SG_GROUPS, CHUNK, CHUNK)), _const((CHUNK, SG_WIDTH))],
        out_specs=_row_blocks(SG_PER_STEP),
        out_shape=jax.ShapeDtypeStruct((t, D_MODEL), F32),
        scratch_shapes=_weight_scratch([(D_MODEL, S_MAIN_COLS), (D_MODEL, W_PAD_COLS), (SG_WIDTH, W_PAD_COLS)])
                       + [pltpu.VMEM((CHUNK, S_COLS), F32), pltpu.VMEM((CHUNK, S_COLS), F32),
                        pltpu.VMEM((CHUNK, D_MODEL), BF16),
                        pltpu.VMEM((SG_GROUPS // 2, CHUNK, 2 * CHUNK), BF16),
                        pltpu.VMEM((CHUNK, SG_WIDTH), BF16)],
        compiler_params=_cparams(),
        name="sg",
    )(*([x2] * (1 + SG_PER_STEP)), acc, g_pre.reshape(1, -1), w_in_t, w_br, ln_g.reshape(1, -1), ln_b.reshape(1, -1),
      w_s, b_full)


def _attsg_branch(x2, g_pre, w_in_t, w_br_att, w_br_sg, layer, rel_bias, sinks, bucket, ln_g, ln_b, w_s, b_full, nb):
    assert ATT_PER_STEP == SG_PER_STEP
    t = x2.shape[0]
    n_blk = t // CHUNK
    smem = pl.BlockSpec(memory_space=pltpu.SMEM)
    weights = [(D_MODEL, A_G), (D_MODEL, W_PAD_COLS), (ATT_WIDTH, W_PAD_COLS),
               (D_MODEL, S_MAIN_COLS), (D_MODEL, W_PAD_COLS), (SG_WIDTH, W_PAD_COLS)]
    return pl.pallas_call(
        functools.partial(_attsg_kernel, nb=nb, layer=layer),
        grid=(n_blk // ATT_PER_STEP,),
        in_specs=[smem, smem, _const((CHUNK, 2 * CHUNK))] + _x_specs(n_blk, ATT_PER_STEP)
                 + [_const((1, D_MODEL)), _any(), _any(), _any(), _const((1, SG_WIDTH)), _const((1, SG_WIDTH)),
                    _const((SG_GROUPS, CHUNK, CHUNK)), _const((CHUNK, SG_WIDTH))],
        out_specs=_row_blocks(ATT_PER_STEP),
        out_shape=jax.ShapeDtypeStruct((t, D_MODEL), F32),
        scratch_shapes=_weight_scratch(weights)
                       + [pltpu.VMEM((CHUNK, A_COLS), F32), pltpu.VMEM((CHUNK, A_COLS), F32),
                          pltpu.VMEM((CHUNK, D_MODEL), BF16),
                          pltpu.VMEM((CHUNK, 2 * LANE), F32),
                          pltpu.VMEM((2, ATT_HEADS, CHUNK, CHUNK), F32),
                          pltpu.VMEM((CHUNK, D_MODEL), BF16),
                          pltpu.VMEM((CHUNK, S_COLS), F32), pltpu.VMEM((CHUNK, S_COLS), F32),
                          pltpu.VMEM((CHUNK, D_MODEL), BF16),
                          pltpu.VMEM((SG_GROUPS // 2, CHUNK, 2 * CHUNK), BF16),
                          pltpu.VMEM((CHUNK, SG_WIDTH), BF16)],
        compiler_params=_cparams(),
        name="attsg",
    )(rel_bias, sinks, bucket, *([x2] * (1 + ATT_PER_STEP)), g_pre.reshape(1, -1), w_in_t, w_br_att, w_br_sg,
      ln_g.reshape(1, -1), ln_b.reshape(1, -1), w_s, b_full)


def _ssd_branch(x2, acc, g_pre, w_in_t, w_br, w_out, layer, g_post, cwx, cbx, cwb, cbb, dtb, alog, drow,
                ng, nb):
    t = x2.shape[0]
    n_blk = t // CHUNK
    return pl.pallas_call(
        functools.partial(_ssd_kernel, nb=nb, layer=layer),
        grid=(n_blk // 2,),
        in_specs=_x_specs(n_blk) + [_row_blocks(), _row_blocks(), _const((1, D_MODEL)), _any(), _any(), _any(),
                                    _const((1, D_MODEL)),
                                    _const((SSM_CONV, 1, SSM_WIDTH)), _const((1, SSM_WIDTH)),
                                    _const((SSM_CONV, 1, SSM_BC)), _const((1, SSM_BC)),
                                    _const((1, LANE)), _const((LANE, 1)),
                                    _const((1, SSM_WIDTH)), _const((1, SSM_WIDTH))],
        out_specs=_row_blocks(),
        out_shape=jax.ShapeDtypeStruct((t, D_MODEL), F32),
        scratch_shapes=_weight_scratch([(D_MODEL, M_G), (D_MODEL, W_PAD_COLS), (SSM_WIDTH, W_PAD_COLS),
                                        (D_MODEL, W_PAD_COLS)])
                       + [pltpu.VMEM((CHUNK, M_COLS), F32), pltpu.VMEM((CHUNK, M_COLS), F32),
                        pltpu.VMEM((CHUNK, D_MODEL), BF16),
                        pltpu.VMEM((SSM_WIDTH // LANE, 8 + CHUNK, LANE), F32),
                        pltpu.VMEM((SSM_BC // LANE, 8 + CHUNK, LANE), F32),
                        pltpu.VMEM((SSM_STATE, SSM_WIDTH), F32),
                        pltpu.VMEM((CHUNK, SSM_WIDTH), F32),
                        pltpu.VMEM((CHUNK, SSM_WIDTH), F32),
                        pltpu.VMEM((CHUNK, SSM_WIDTH), BF16),
                        pltpu.VMEM((CHUNK, D_MODEL), BF16),
                        pltpu.VMEM((CHUNK, D_MODEL), F32)],
        compiler_params=_cparams(),
        name="ssd",
    )(x2, x2, x2, x2, acc, g_pre.reshape(1, -1), w_in_t, w_br, w_out, g_post.reshape(1, -1),
      cwx, cbx, cwb, cbb, dtb, alog, drow, ng)


def _rel_bucket_table():
    qi = jnp.arange(CHUNK, dtype=jnp.int32)[:, None]
    kj = jnp.arange(2 * CHUNK, dtype=jnp.int32)[None, :]
    dist = jnp.maximum(qi + CHUNK - kj, 0)
    max_exact = REL_BUCKETS // 2
    dist_f = jnp.maximum(dist, 1).astype(F32)
    large = max_exact + (jnp.log(dist_f / max_exact) / math.log(REL_MAX_DIST / max_exact)
                         * (REL_BUCKETS - max_exact)).astype(jnp.int32)
    large = jnp.minimum(large, REL_BUCKETS - 1)
    return jnp.where(dist < max_exact, dist, large)


def _pad_lanes(v):
    return jnp.pad(v, (0, LANE - v.shape[0])).reshape(1, LANE)


def kernel(x, w_in, norm_pre, norm_post, rel_bias, att_sinks, sg_ln_g, sg_ln_b, sg_w, sg_b, ssm_conv_w, ssm_conv_b, ssm_dt_bias, ssm_a_log, ssm_d, ssm_norm_g, w_br_att, w_br_sg, w_br_ssm, w_out):
    bsz, seq, d = x.shape
    depth = w_in.shape[0]
    assert d == D_MODEL and seq % (2 * CHUNK) == 0 and seq % (SG_PER_STEP * CHUNK) == 0
    nb = seq // CHUNK
    x2 = x.reshape(bsz * seq, d)
    bucket = _rel_bucket_table()
    w_in_t = jnp.swapaxes(w_in, 1, 2)
    for l in range(depth):
        b_full = jnp.repeat(jnp.transpose(sg_b[l]), CHUNK, axis=1)
        acc = _attsg_branch(x2, norm_pre[l], w_in_t, w_br_att, w_br_sg, l, rel_bias, att_sinks[l], bucket,
                            sg_ln_g[l], sg_ln_b[l], sg_w[l], b_full, nb)
        cw, cb = ssm_conv_w[l], ssm_conv_b[l]
        x2 = _ssd_branch(x2, acc, norm_pre[l], w_in_t, w_br_ssm, w_out, l, norm_post[l],
                         cw[:, None, :SSM_WIDTH], cb[:SSM_WIDTH].reshape(1, -1),
                         cw[:, None, SSM_WIDTH:], cb[SSM_WIDTH:].reshape(1, -1),
                         _pad_lanes(ssm_dt_bias[l]), _pad_lanes(ssm_a_log[l]).reshape(LANE, 1),
                         jnp.repeat(ssm_d[l], SSM_HEAD_DIM).reshape(1, -1),
                         ssm_norm_g[l].reshape(1, -1), nb)
    return x2.reshape(bsz, seq, d)
```
